```python
import math, functools
import jax, jax.numpy as jnp
from jax import lax
import numpy as np

D_MODEL = 2048
BATCH = 2
SEQ = 4096
DEPTH = 1
DEC_BATCH = 32
DEC_SEQ = 1
PAST_LEN = 8192
PAGE_SIZE = 128

POOL_WINDOWS = (2, 4, 8, 16)
POOL_GROUPS = 4
POOL_WIDTH = D_MODEL // 2
POOL_GROUP_WIDTH = POOL_WIDTH // POOL_GROUPS
POOL_STATE = 15
N_HEADS = 16
N_KV_HEADS = 4
HEAD_DIM = 64
ATTN_WIDTH = N_HEADS * HEAD_DIM
KV_WIDTH = N_KV_HEADS * HEAD_DIM
ROT_DIM = HEAD_DIM // 4
ROPE_THETA = 500000.0
IDX_HEADS = 16
IDX_DIM = 64
IDX_ROT_DIM = IDX_DIM // 4
IDX_W_SCALE = (IDX_HEADS * IDX_DIM) ** -0.5
TOPK_MAX = 256
Q_BLOCK = 128
IN_SIZES = (POOL_WIDTH, ATTN_WIDTH, KV_WIDTH, KV_WIDTH, IDX_HEADS * IDX_DIM, IDX_DIM, IDX_HEADS, D_MODEL, D_MODEL)
IN_WIDTH = sum(IN_SIZES)
V_OFFSET = POOL_WIDTH + ATTN_WIDTH + KV_WIDTH
N_GROUPS = 4
EXPERTS_PER_GROUP = 8
N_EXPERTS = N_GROUPS * EXPERTS_PER_GROUP
D_EXPERT = 256
TOP_K_EXPERTS = 2
ALPHA = (2 * DEPTH) ** 0.25
BETA = (8 * DEPTH) ** -0.25
LN_EPS = 1e-5

kernel_name = 'hybrid_pool_dsa_hmoe_step'


def layer_norm(x, g, b):
    xf = x.astype(jnp.float32)
    mu = jnp.mean(xf, axis=-1, keepdims=True)
    var = jnp.mean(jnp.square(xf - mu), axis=-1, keepdims=True)
    return ((xf - mu) * lax.rsqrt(var + LN_EPS)).astype(x.dtype) * g + b


def partial_rope(x, pos, rot_dim):
    half = rot_dim // 2
    inv = jnp.exp(-jnp.arange(half, dtype=jnp.float32) * (math.log(ROPE_THETA) / half))
    ang = pos.astype(jnp.float32)[:, None] * inv[None, :]
    cos = jnp.cos(ang)[None, :, None, :].astype(x.dtype)
    sin = jnp.sin(ang)[None, :, None, :].astype(x.dtype)
    x1 = x[..., :half]
    x2 = x[..., half:rot_dim]
    return jnp.concatenate([x1 * cos - x2 * sin, x2 * cos + x1 * sin, x[..., rot_dim:]], axis=-1)


def in_projection(x, pos, w_in):
    b, t, _ = x.shape
    z = jnp.einsum('btd,de->bte', x, w_in)
    offs = [int(o) for o in np.cumsum(IN_SIZES)[:-1]]
    u, q, k, v, qi, ki, wi, ga, gb = jnp.split(z, offs, axis=-1)
    q = partial_rope(q.reshape(b, t, N_HEADS, HEAD_DIM), pos, ROT_DIM)
    k = partial_rope(k.reshape(b, t, N_KV_HEADS, HEAD_DIM), pos, ROT_DIM)
    v = v.reshape(b, t, N_KV_HEADS, HEAD_DIM)
    qi = partial_rope(qi.reshape(b, t, IDX_HEADS, IDX_DIM), pos, IDX_ROT_DIM)
    ki = partial_rope(ki[:, :, None, :], pos, IDX_ROT_DIM)[:, :, 0]
    wi = wi * IDX_W_SCALE
    return u, q, k, v, qi, ki, wi, ga, gb


def pool_mix(u_ext, pos, w_pool, pool_scale):
    b = u_ext.shape[0]
    t = pos.shape[0]
    uf = u_ext.astype(jnp.float32)
    cs = jnp.concatenate([jnp.zeros_like(uf[:, :1]), jnp.cumsum(uf, axis=1)], axis=1)
    end = cs[:, POOL_STATE + 1:]
    u_new = uf[:, POOL_STATE:]
    outs = []
    for g, w in enumerate(POOL_WINDOWS):
        sl = slice(g * POOL_GROUP_WIDTH, (g + 1) * POOL_GROUP_WIDTH)
        start = cs[:, POOL_STATE + 1 - w:POOL_STATE + 1 - w + t, sl]
        cnt = jnp.minimum(w, pos + 1).astype(jnp.float32)[None, :, None]
        outs.append((end[..., sl] - start) / cnt - u_new[..., sl])
    d = jnp.stack(outs, axis=2).astype(u_ext.dtype)
    z = jnp.einsum('btgc,gce->btge', d, w_pool).reshape(b, t, POOL_WIDTH)
    return z * pool_scale


def indexer_scores(qi, wi, ki):
    dots = jnp.einsum('bthd,bsd->bths', qi, ki).astype(jnp.float32)
    return jnp.einsum('bths,bth->bts', jax.nn.relu(dots), wi.astype(jnp.float32))


def gathered_attention(q, k_sel, v_sel, valid):
    b, tq = q.shape[:2]
    qg = q.reshape(b, tq, N_KV_HEADS, N_HEADS // N_KV_HEADS, HEAD_DIM)
    s = jnp.einsum('btgrd,btkgd->btgrk', qg, k_sel).astype(jnp.float32) * (HEAD_DIM ** -0.5)
    s = jnp.where(valid[:, :, None, None, :], s, -jnp.inf)
    p = jax.nn.softmax(s, axis=-1).astype(v_sel.dtype)
    o = jnp.einsum('btgrk,btkgd->btgrd', p, v_sel)
    return o.reshape(b, tq, ATTN_WIDTH)


def prompt_attention(q, qi, wi, k, v, ki):
    b, s = q.shape[:2]
    topk = min(TOPK_MAX, s // 4)
    key_pos = jnp.arange(s)

    def block(i):
        t0 = i * Q_BLOCK
        qb = lax.dynamic_slice_in_dim(q, t0, Q_BLOCK, axis=1)
        qib = lax.dynamic_slice_in_dim(qi, t0, Q_BLOCK, axis=1)
        wib = lax.dynamic_slice_in_dim(wi, t0, Q_BLOCK, axis=1)
        qpos = t0 + jnp.arange(Q_BLOCK)
        sc = indexer_scores(qib, wib, ki)
        sc = jnp.where(key_pos[None, None, :] <= qpos[None, :, None], sc, -jnp.inf)
        _, idx = lax.top_k(sc, topk)
        k_sel = jax.vmap(lambda kk, ii: kk[ii])(k, idx)
        v_sel = jax.vmap(lambda vv, ii: vv[ii])(v, idx)
        return gathered_attention(qb, k_sel, v_sel, idx <= qpos[None, :, None])

    out = lax.map(block, jnp.arange(s // Q_BLOCK))
    return jnp.swapaxes(out, 0, 1).reshape(b, s, ATTN_WIDTH)


def sample_attention(q, qi, wi, k, v, ki, cache_k, cache_v, cache_kidx, page_table):
    db, t = q.shape[:2]
    past = PAST_LEN
    n_keys = past + t
    topk = min(TOPK_MAX, n_keys // 4)
    ki_past = cache_kidx[page_table].reshape(db, past, IDX_DIM)
    ki_all = jnp.concatenate([ki_past, ki], axis=1)
    qpos = past + jnp.arange(t)
    sc = indexer_scores(qi, wi, ki_all)
    sc = jnp.where(jnp.arange(n_keys)[None, None, :] <= qpos[None, :, None], sc, -jnp.inf)
    _, idx = lax.top_k(sc, topk)
    in_past = idx < past
    pidx = jnp.minimum(idx, past - 1)
    page = jax.vmap(lambda pt, pg: pt[pg])(page_table, pidx // PAGE_SIZE)
    phys = page * PAGE_SIZE + pidx % PAGE_SIZE
    k_past = cache_k.reshape(-1, N_KV_HEADS, HEAD_DIM)[phys]
    v_past = cache_v.reshape(-1, N_KV_HEADS, HEAD_DIM)[phys]
    nidx = jnp.clip(idx - past, 0, t - 1)
    k_cur = jax.vmap(lambda kk, ii: kk[ii])(k, nidx)
    v_cur = jax.vmap(lambda vv, ii: vv[ii])(v, nidx)
    sel = in_past[..., None, None]
    k_sel = jnp.where(sel, k_past, k_cur)
    v_sel = jnp.where(sel, v_past, v_cur)
    return gathered_attention(q, k_sel, v_sel, idx <= qpos[None, :, None])


def hier_moe(x, w_group, w_expert_router, w_gate, w_up, w_down):
    n = x.shape[0]
    gp = jax.nn.softmax(jnp.dot(x, w_group).astype(jnp.float32), axis=-1)
    g_p, g_idx = lax.top_k(gp, 1)
    el = jnp.dot(x, w_expert_router).astype(jnp.float32).reshape(n, N_GROUPS, EXPERTS_PER_GROUP)
    el = jnp.take_along_axis(el, g_idx[:, :, None], axis=1)[:, 0]
    e_p, e_idx = lax.top_k(jax.nn.softmax(el, axis=-1), TOP_K_EXPERTS)
    comb = g_p * e_p / jnp.sum(e_p, axis=-1, keepdims=True)
    eid = g_idx * EXPERTS_PER_GROUP + e_idx
    dense = jnp.sum(jax.nn.one_hot(eid, N_EXPERTS, dtype=jnp.float32) * comb[..., None], axis=1)
    h = jax.nn.silu(jnp.einsum('nd,edf->nef', x, w_gate)) * jnp.einsum('nd,edf->nef', x, w_up)
    h = h * dense[:, :, None].astype(h.dtype)
    return jnp.einsum('nef,efd->nd', h, w_down).astype(x.dtype)


def decoder_layer(x, pos, u_prefix, attend, w_in, w_pool, pool_scale, w_branch_a, w_branch_b, w_out,
                  ln1_g, ln1_b, w_group, w_expert_router, w_gate, w_up, w_down, ln2_g, ln2_b):
    u, q, k, v, qi, ki, wi, ga, gb = in_projection(x, pos, w_in)
    u_ext = jnp.concatenate([u_prefix.astype(u.dtype), u], axis=1)
    a = pool_mix(u_ext, pos, w_pool, pool_scale)
    o = attend(q, qi, wi, k, v, ki)
    merged = jax.nn.sigmoid(ga) * (a @ w_branch_a) + jax.nn.sigmoid(gb) * (o @ w_branch_b)
    h = layer_norm(ALPHA * x + merged @ w_out, ln1_g, ln1_b)
    b, t, d = h.shape
    f = hier_moe(h.reshape(b * t, d), w_group, w_expert_router, w_gate, w_up, w_down).reshape(b, t, d)
    y = layer_norm(ALPHA * h + f, ln2_g, ln2_b)
    return y, k, v, ki, u_ext[:, -POOL_STATE:]


def setup_inputs(seed: int = 0) -> dict:
    key = jax.random.key(seed)
    ks = jax.random.split(key, 24)
    n_pages = PAST_LEN // PAGE_SIZE
    n_used = DEC_BATCH * n_pages
    n_phys = n_used + n_used // 4

    def nrm(k, shape, scale):
        return jax.random.normal(k, shape, jnp.float32) * scale

    col_scale = jnp.ones((IN_WIDTH,), jnp.float32).at[V_OFFSET:V_OFFSET + KV_WIDTH].set(BETA)
    return {
        'x_prompt': nrm(ks[0], (BATCH, SEQ, D_MODEL), 1.0),
        'x_sample': nrm(ks[1], (DEC_BATCH, DEC_SEQ, D_MODEL), 1.0),
        'cache_k': nrm(ks[2], (DEPTH, n_phys, PAGE_SIZE, N_KV_HEADS, HEAD_DIM), 1.0),
        'cache_v': nrm(ks[3], (DEPTH, n_phys, PAGE_SIZE, N_KV_HEADS, HEAD_DIM), BETA),
        'cache_kidx': nrm(ks[4], (DEPTH, n_phys, PAGE_SIZE, IDX_DIM), 1.0),
        'state_pool': nrm(ks[5], (DEPTH, DEC_BATCH, POOL_STATE, POOL_WIDTH), 1.0),
        'page_table': jax.random.permutation(ks[6], n_phys)[:n_used].reshape(DEC_BATCH, n_pages).astype(jnp.int32),
        'w_in': nrm(ks[7], (DEPTH, D_MODEL, IN_WIDTH), D_MODEL ** -0.5) * col_scale,
        'w_pool': nrm(ks[8], (DEPTH, POOL_GROUPS, POOL_GROUP_WIDTH, POOL_GROUP_WIDTH), POOL_GROUP_WIDTH ** -0.5),
        'pool_scale': 1.0 + nrm(ks[9], (DEPTH, POOL_WIDTH), 0.02),
        'w_branch_a': nrm(ks[10], (DEPTH, POOL_WIDTH, D_MODEL), POOL_WIDTH ** -0.5),
        'w_branch_b': nrm(ks[11], (DEPTH, ATTN_WIDTH, D_MODEL), ATTN_WIDTH ** -0.5),
        'w_out': nrm(ks[12], (DEPTH, D_MODEL, D_MODEL), BETA * D_MODEL ** -0.5),
        'ln1_g': 1.0 + nrm(ks[13], (DEPTH, D_MODEL), 0.02),
        'ln1_b': nrm(ks[14], (DEPTH, D_MODEL), 0.02),
        'w_group': nrm(ks[15], (DEPTH, D_MODEL, N_GROUPS), D_MODEL ** -0.5),
        'w_expert_router': nrm(ks[16], (DEPTH, D_MODEL, N_EXPERTS), D_MODEL ** -0.5),
        'w_gate': nrm(ks[17], (DEPTH, N_EXPERTS, D_MODEL, D_EXPERT), D_MODEL ** -0.5),
        'w_up': nrm(ks[18], (DEPTH, N_EXPERTS, D_MODEL, D_EXPERT), D_MODEL ** -0.5),
        'w_down': nrm(ks[19], (DEPTH, N_EXPERTS, D_EXPERT, D_MODEL), BETA * D_EXPERT ** -0.5),
        'ln2_g': 1.0 + nrm(ks[20], (DEPTH, D_MODEL), 0.02),
        'ln2_b': nrm(ks[21], (DEPTH, D_MODEL), 0.02),
    }


def reference(x_prompt, x_sample, cache_k, cache_v, cache_kidx, state_pool, page_table,
              w_in, w_pool, pool_scale, w_branch_a, w_branch_b, w_out, ln1_g, ln1_b,
              w_group, w_expert_router, w_gate, w_up, w_down, ln2_g, ln2_b):
    pos_p = jnp.arange(x_prompt.shape[1], dtype=jnp.int32)
    pos_s = PAST_LEN + jnp.arange(x_sample.shape[1], dtype=jnp.int32)
    xp, xs = x_prompt, x_sample
    kp_l, vp_l, kip_l, pp_l, ks_l, vs_l, kis_l, ps_l = [], [], [], [], [], [], [], []
    for l in range(DEPTH):
        params = (w_in[l], w_pool[l], pool_scale[l], w_branch_a[l], w_branch_b[l], w_out[l],
                  ln1_g[l], ln1_b[l], w_group[l], w_expert_router[l], w_gate[l], w_up[l], w_down[l],
                  ln2_g[l], ln2_b[l])
        zero_prefix = jnp.zeros((xp.shape[0], POOL_STATE, POOL_WIDTH), xp.dtype)
        xp, kp, vp, kip, pp = decoder_layer(xp, pos_p, zero_prefix, prompt_attention, *params)
        attend_s = functools.partial(sample_attention, cache_k=cache_k[l], cache_v=cache_v[l],
                                     cache_kidx=cache_kidx[l], page_table=page_table)
        xs, ks_, vs_, kis, ps = decoder_layer(xs, pos_s, state_pool[l], attend_s, *params)
        kp_l.append(kp); vp_l.append(vp); kip_l.append(kip); pp_l.append(pp)
        ks_l.append(ks_); vs_l.append(vs_); kis_l.append(kis); ps_l.append(ps)
    return (xp, xs, jnp.stack(kp_l), jnp.stack(vp_l), jnp.stack(kip_l), jnp.stack(pp_l),
            jnp.stack(ks_l), jnp.stack(vs_l), jnp.stack(kis_l), jnp.stack(ps_l))
```

```python
import functools
import math

import jax
import jax.numpy as jnp
import numpy as np
from jax import lax
from jax.experimental import pallas as pl
from jax.experimental.pallas import tpu as pltpu

BF16 = jnp.bfloat16
F32 = jnp.float32
I32 = jnp.int32

PAGE_SIZE = 128
POOL_WINDOWS = (2, 4, 8, 16)
POOL_STATE = 15
N_HEADS = 16
N_KV_HEADS = 4
HEAD_DIM = 64
ROT_DIM = HEAD_DIM // 4
ROPE_THETA = 500000.0
IDX_HEADS = 16
IDX_DIM = 64
IDX_W_SCALE = (IDX_HEADS * IDX_DIM) ** -0.5
TOPK_MAX = 256
N_GROUPS = 4
EXPERTS_PER_GROUP = 8
N_EXPERTS = N_GROUPS * EXPERTS_PER_GROUP
LN_EPS = 1e-5
DEPTH = 1
ALPHA = (2 * DEPTH) ** 0.25

LANES = 128
Q_TILE = 128
KEY_CHUNK = 512
ROW_TILE = 256
MOE_ROW_TILE = 512
VMEM_LIMIT = 56 * 1024 * 1024

INT_MIN = -2 ** 31
NEG_INF_KEY = int(np.int32(np.uint32(0xFF800000) ^ np.uint32(0x7FFFFFFF)))


def _sortable_key(x):
    bits = lax.bitcast_convert_type(x, I32)
    return bits ^ ((bits >> 31) & 0x7FFFFFFF)


def _const_spec(shape):
    nd = len(shape)
    return pl.BlockSpec(shape, lambda *_: (0,) * nd, pipeline_mode=pl.Buffered(1))


def _inproj_body(x_ref, w_ref, tc_ref, ts1_ref, ts2_ref, wpool_ref, pscale_ref, state_ref,
                 u_ref, a_ref, qz_ref, qiz_ref, k_ref, v_ref, kb_ref, vb_ref, kiwi_ref, kib_ref,
                 uext_ref, *, tm, tiles_per_batch, n_prompt_tiles, dec_batch, sample_pos, pool_width):
    i = pl.program_id(0)
    xb = x_ref[...].astype(BF16)
    cos = tc_ref[...]
    nsin = ts1_ref[...]
    psin = ts2_ref[...]
    lane = lax.broadcasted_iota(I32, (tm, LANES), 1)
    lo = lane < HEAD_DIM
    gw = pool_width // len(POOL_WINDOWS)
    attn_w = N_HEADS * HEAD_DIM
    kv_w = N_KV_HEADS * HEAD_DIM
    off_q = pool_width
    off_k = off_q + attn_w
    off_v = off_k + kv_w
    off_qi = off_v + kv_w
    off_kiwi = off_qi + IDX_HEADS * IDX_DIM

    def proj(c0, width):
        return jnp.dot(xb, w_ref[:, c0:c0 + width], preferred_element_type=F32)

    def rope(z, c, s1, s2):
        return z * c + pltpu.roll(z, LANES - ROT_DIM // 2, 1) * s1 + pltpu.roll(z, ROT_DIM // 2, 1) * s2

    def store_heads(dst_ref, h, tile):
        for blk in range(tm // Q_TILE):
            dst_ref[blk, h * Q_TILE:(h + 1) * Q_TILE, :] = tile[blk * Q_TILE:(blk + 1) * Q_TILE].astype(BF16)

    u = proj(0, pool_width)
    u_ref[...] = u

    zq = proj(off_q, attn_w)
    for j in range(N_HEADS // 2):
        p = rope(zq[:, j * LANES:(j + 1) * LANES], cos, nsin, psin) * (HEAD_DIM ** -0.5)
        pr = pltpu.roll(p, HEAD_DIM, 1)
        if (j // 2) % 2 == 0:
            store_heads(qz_ref, 2 * j, jnp.where(lo, p, 0.0))
            store_heads(qz_ref, 2 * j + 1, jnp.where(lo, pr, 0.0))
        else:
            store_heads(qz_ref, 2 * j, jnp.where(lo, 0.0, pr))
            store_heads(qz_ref, 2 * j + 1, jnp.where(lo, 0.0, p))

    zqi = proj(off_qi, IDX_HEADS * IDX_DIM)
    for j in range(IDX_HEADS // 2):
        p = rope(zqi[:, j * LANES:(j + 1) * LANES], cos, nsin, psin)
        pr = pltpu.roll(p, IDX_DIM, 1)
        store_heads(qiz_ref, 2 * j, jnp.where(lo, p, 0.0))
        store_heads(qiz_ref, 2 * j + 1, jnp.where(lo, pr, 0.0))

    zk = proj(off_k, kv_w)
    for j in range(kv_w // LANES):
        kr = rope(zk[:, j * LANES:(j + 1) * LANES], cos, nsin, psin)
        k_ref[:, j * LANES:(j + 1) * LANES] = kr
        kb_ref[:, j * LANES:(j + 1) * LANES] = kr.astype(BF16)

    zv = proj(off_v, kv_w)
    v_ref[...] = zv
    vb_ref[...] = zv.astype(BF16)

    zkw = proj(off_kiwi, LANES)
    kw = rope(zkw, jnp.where(lo, cos, IDX_W_SCALE), jnp.where(lo, nsin, 0.0), jnp.where(lo, psin, 0.0))
    kiwi_ref[...] = kw
    kib_ref[...] = kw.astype(BF16)

    def mix(d, g):
        lanes = slice(g * gw, (g + 1) * gw)
        z = jnp.dot(d.astype(BF16), wpool_ref[g], preferred_element_type=F32)
        return z * pscale_ref[:, lanes]

    @pl.when(i < n_prompt_tiles)
    def _():
        first = (i % tiles_per_batch) == 0

        @pl.when(first)
        def _():
            uext_ref[0:16, :] = jnp.zeros((16, pool_width), F32)

        @pl.when(jnp.logical_not(first))
        def _():
            uext_ref[0:16, :] = uext_ref[tm:tm + 16, :]

        uext_ref[16:16 + tm, :] = u
        pos = (i % tiles_per_batch) * tm + lax.broadcasted_iota(I32, (tm, 1), 0)
        for g, w in enumerate(POOL_WINDOWS):
            lanes = slice(g * gw, (g + 1) * gw)
            acc = u[:, lanes]
            for jj in range(1, w):
                acc = acc + uext_ref[16 - jj:16 - jj + tm, lanes]
            cnt = jnp.minimum(w, pos + 1).astype(F32)
            d = acc / cnt - u[:, lanes]
            a_ref[:, lanes] = mix(d, g).astype(BF16)

    @pl.when(i == n_prompt_tiles)
    def _():
        a_ref[...] = jnp.zeros((tm, pool_width), BF16)
        us = u[0:dec_batch, :]
        for g, w in enumerate(POOL_WINDOWS):
            lanes = slice(g * gw, (g + 1) * gw)
            acc = us[:, lanes]
            for jj in range(1, w):
                acc = acc + state_ref[POOL_STATE - jj, :, lanes]
            d = acc / float(min(w, sample_pos + 1)) - us[:, lanes]
            a_ref[0:dec_batch, lanes] = mix(d, g).astype(BF16)


def _in_projection(x_all, w_a, tab_c, tab_s1, tab_s2, wpool, pscale, state_t, *, seq, n_prompt, sample_pos):
    n_rows, d_model = x_all.shape
    tm = ROW_TILE
    pool_width = wpool.shape[0] * wpool.shape[1]
    dec_batch = state_t.shape[1]
    n_tiles = n_rows // tm
    nblk = n_rows // Q_TILE
    kv_w = N_KV_HEADS * HEAD_DIM
    row = lambda width: pl.BlockSpec((tm, width), lambda i: (i, 0))
    hm = pl.BlockSpec((tm // Q_TILE, N_HEADS * Q_TILE, LANES), lambda i: (i, 0, 0))
    body = functools.partial(_inproj_body, tm=tm, tiles_per_batch=seq // tm, n_prompt_tiles=n_prompt // tm,
                             dec_batch=dec_batch, sample_pos=sample_pos, pool_width=pool_width)
    return pl.pallas_call(
        body,
        grid=(n_tiles,),
        in_specs=[row(d_model), _const_spec(w_a.shape), row(LANES), row(LANES), row(LANES),
                  _const_spec(wpool.shape), _const_spec(pscale.shape), _const_spec(state_t.shape)],
        out_specs=[row(pool_width), row(pool_width), hm, hm, row(kv_w), row(kv_w), row(kv_w), row(kv_w),
                   row(LANES), row(LANES)],
        out_shape=[
            jax.ShapeDtypeStruct((n_rows, pool_width), F32),
            jax.ShapeDtypeStruct((n_rows, pool_width), BF16),
            jax.ShapeDtypeStruct((nblk, N_HEADS * Q_TILE, LANES), BF16),
            jax.ShapeDtypeStruct((nblk, IDX_HEADS * Q_TILE, LANES), BF16),
            jax.ShapeDtypeStruct((n_rows, kv_w), F32),
            jax.ShapeDtypeStruct((n_rows, kv_w), F32),
            jax.ShapeDtypeStruct((n_rows, kv_w), BF16),
            jax.ShapeDtypeStruct((n_rows, kv_w), BF16),
            jax.ShapeDtypeStruct((n_rows, LANES), F32),
            jax.ShapeDtypeStruct((n_rows, LANES), BF16),
        ],
        scratch_shapes=[pltpu.VMEM((tm + 16, pool_width), F32)],
        compiler_params=pltpu.CompilerParams(dimension_semantics=("arbitrary",), vmem_limit_bytes=VMEM_LIMIT),
        name="in_projection",
    )(x_all, w_a, tab_c, tab_s1, tab_s2, wpool, pscale, state_t)


def _rope_tables(pos):
    n = pos.shape[0]
    half = ROT_DIM // 2
    inv = jnp.exp(-jnp.arange(half, dtype=F32) * (math.log(ROPE_THETA) / half))
    ang = pos.astype(F32)[:, None] * inv[None, :]
    cos, sin = jnp.cos(ang), jnp.sin(ang)
    zeros = jnp.zeros((n, half), F32)
    rest = HEAD_DIM - ROT_DIM
    c = jnp.concatenate([cos, cos, jnp.ones((n, rest), F32)], axis=1)
    s1 = jnp.concatenate([-sin, zeros, jnp.zeros((n, rest), F32)], axis=1)
    s2 = jnp.concatenate([zeros, sin, jnp.zeros((n, rest), F32)], axis=1)
    return tuple(jnp.concatenate([t, t], axis=1) for t in (c, s1, s2))


def _stage1(x_all, batch, seq, state_pool, w_in, w_pool, pool_scale, past_len):
    d_model = x_all.shape[1]
    dec_batch = state_pool.shape[0]
    n_prompt = batch * seq
    pad = x_all.shape[0] - n_prompt - dec_batch
    pos = jnp.concatenate([jnp.tile(jnp.arange(seq, dtype=I32), batch), jnp.full((dec_batch,), past_len, I32),
                           jnp.zeros((pad,), I32)])
    tab_c, tab_s1, tab_s2 = _rope_tables(pos)
    pool_width = w_pool.shape[0] * w_pool.shape[1]
    a_width = pool_width + N_HEADS * HEAD_DIM + 2 * N_KV_HEADS * HEAD_DIM + IDX_HEADS * IDX_DIM + IDX_DIM + IDX_HEADS
    w_a = jnp.concatenate([w_in[:, :a_width], jnp.zeros((d_model, -a_width % LANES), F32)], axis=1).astype(BF16)
    state_t = jnp.transpose(state_pool, (1, 0, 2))
    return _in_projection(x_all, w_a, tab_c, tab_s1, tab_s2, w_pool.astype(BF16),
                          pool_scale.reshape(1, pool_width), state_t,
                          seq=seq, n_prompt=n_prompt, sample_pos=past_len)


def _topk_threshold(count_ge, init, topk):
    def step(it, t):
        cand = t + jnp.left_shift(jnp.int32(1), 31 - it)
        return jnp.where(count_ge(cand) >= topk, cand, t)
    return lax.fori_loop(0, 32, step, init)


def _flash_update(s, v_tile, m_ref, l_ref, acc_ref, rows):
    m_old = m_ref[rows, :]
    m_new = jnp.maximum(m_old, jnp.max(s, axis=1, keepdims=True))
    m_safe = jnp.where(m_new == -jnp.inf, 0.0, m_new)
    alpha = jnp.exp(m_old - m_safe)
    p = jnp.exp(s - m_safe)
    l_ref[rows, :] = alpha * l_ref[rows, :] + jnp.sum(p, axis=1, keepdims=True)
    acc_ref[rows, :] = alpha * acc_ref[rows, :] + jnp.dot(p.astype(BF16), v_tile, preferred_element_type=F32)
    m_ref[rows, :] = m_new


def _prompt_attn_body(qiz_ref, w_ref, qz_ref, kib_ref, kb_ref, vb_ref, o_ref,
                      sc_ref, thr_ref, m_ref, l_ref, acc_ref, *, topk):
    i = pl.program_id(1)
    tq, kc = Q_TILE, KEY_CHUNK
    n_keys = (i + 1) * tq
    n_chunks = (n_keys + kc - 1) // kc
    qpos = i * tq + lax.broadcasted_iota(I32, (tq, 1), 0)
    heads_per_dot = 4
    nt = (((1,), (1,)), ((), ()))

    def score_chunk(c, carry):
        k0 = pl.multiple_of(c * kc, kc)
        kchunk = kib_ref[pl.ds(k0, kc), :]
        acc = jnp.zeros((tq, kc), F32)
        for hg in range(IDX_HEADS // heads_per_dot):
            d = lax.dot_general(qiz_ref[0, hg * heads_per_dot * tq:(hg + 1) * heads_per_dot * tq, :], kchunk, nt,
                                preferred_element_type=F32)
            for r in range(heads_per_dot):
                h = hg * heads_per_dot + r
                wcol = w_ref[:, IDX_DIM + h:IDX_DIM + h + 1]
                acc = acc + jnp.maximum(d[r * tq:(r + 1) * tq], 0.0) * wcol
        kpos = k0 + lax.broadcasted_iota(I32, (tq, kc), 1)
        sc = jnp.where(kpos <= qpos, acc, -jnp.inf)
        sc_ref[:, pl.ds(k0, kc)] = _sortable_key(sc)
        return carry

    lax.fori_loop(0, n_chunks, score_chunk, 0)

    thr_ref[...] = jnp.full((tq, 1), NEG_INF_KEY + 1, I32)

    @pl.when(n_keys > topk)
    def _():
        def count_ge(cand):
            candb = jnp.broadcast_to(cand, (tq, LANES))

            def blk(c, acc):
                for cb in range(kc // LANES):
                    k0 = pl.multiple_of(c * kc + cb * LANES, LANES)
                    acc = acc + jnp.where(sc_ref[:, pl.ds(k0, LANES)] >= candb, 1.0, 0.0)
                return acc
            acc = lax.fori_loop(0, n_chunks, blk, jnp.zeros((tq, LANES), F32))
            return jnp.sum(acc, axis=1, keepdims=True)

        t = _topk_threshold(count_ge, jnp.full((tq, 1), INT_MIN, I32), float(topk))
        thr_ref[...] = jnp.where(qpos + 1 > topk, t, NEG_INF_KEY + 1)

    m_ref[...] = jnp.full(m_ref.shape, -jnp.inf, F32)
    l_ref[...] = jnp.zeros(l_ref.shape, F32)
    acc_ref[...] = jnp.zeros(acc_ref.shape, F32)
    group = N_HEADS // N_KV_HEADS
    grows = group * tq

    def attn_chunk(c, carry):
        k0 = pl.multiple_of(c * kc, kc)
        sel = sc_ref[:, pl.ds(k0, kc)] >= thr_ref[...]
        for g in range(N_KV_HEADS):
            pair = slice((g // 2) * LANES, (g // 2 + 1) * LANES)
            rows = slice(g * grows, (g + 1) * grows)
            s = lax.dot_general(qz_ref[0, rows, :], kb_ref[pl.ds(k0, kc), pair], nt, preferred_element_type=F32)
            s = jnp.concatenate([jnp.where(sel, s[r * tq:(r + 1) * tq], -jnp.inf) for r in range(group)], axis=0)
            _flash_update(s, vb_ref[pl.ds(k0, kc), pair], m_ref, l_ref, acc_ref, rows)
        return carry

    lax.fori_loop(0, n_chunks, attn_chunk, 0)

    lane = lax.broadcasted_iota(I32, (tq, LANES), 1)
    lo = lane < HEAD_DIM
    for j in range(N_HEADS // 2):
        r0 = slice(2 * j * tq, (2 * j + 1) * tq)
        r1 = slice((2 * j + 1) * tq, (2 * j + 2) * tq)
        o0 = acc_ref[r0, :] / l_ref[r0, :]
        o1 = acc_ref[r1, :] / l_ref[r1, :]
        if (j // 2) % 2 == 0:
            tile = jnp.where(lo, o0, pltpu.roll(o1, HEAD_DIM, 1))
        else:
            tile = jnp.where(lo, pltpu.roll(o0, HEAD_DIM, 1), o1)
        o_ref[:, j * LANES:(j + 1) * LANES] = tile.astype(BF16)


def _prompt_attention(qiz, kiwi, qz, kib, kb, vb, *, batch, seq):
    nq = seq // Q_TILE
    topk = min(TOPK_MAX, seq // 4)
    kv_w = N_KV_HEADS * HEAD_DIM
    per_q = lambda b, i: (b * nq + i, 0, 0)
    body = functools.partial(_prompt_attn_body, topk=topk)
    return pl.pallas_call(
        body,
        grid=(batch, nq),
        in_specs=[
            pl.BlockSpec((1, IDX_HEADS * Q_TILE, LANES), per_q),
            pl.BlockSpec((Q_TILE, LANES), lambda b, i: (b * nq + i, 0)),
            pl.BlockSpec((1, N_HEADS * Q_TILE, LANES), per_q),
            pl.BlockSpec((seq, LANES), lambda b, i: (b, 0)),
            pl.BlockSpec((seq, kv_w), lambda b, i: (b, 0)),
            pl.BlockSpec((seq, kv_w), lambda b, i: (b, 0)),
        ],
        out_specs=pl.BlockSpec((Q_TILE, N_HEADS * HEAD_DIM), lambda b, i: (b * nq + i, 0)),
        out_shape=jax.ShapeDtypeStruct((batch * seq, N_HEADS * HEAD_DIM), BF16),
        scratch_shapes=[
            pltpu.VMEM((Q_TILE, seq), I32),
            pltpu.VMEM((Q_TILE, 1), I32),
            pltpu.VMEM((N_HEADS * Q_TILE, 1), F32),
            pltpu.VMEM((N_HEADS * Q_TILE, 1), F32),
            pltpu.VMEM((N_HEADS * Q_TILE, LANES), F32),
        ],
        compiler_params=pltpu.CompilerParams(dimension_semantics=("arbitrary", "arbitrary"),
                                             vmem_limit_bytes=VMEM_LIMIT),
        name="prompt_attention",
    )(qiz, kiwi, qz, kib, kb, vb)


SAMPLE_PAGES_PER_CHUNK = 16


def _sample_attn_body(pt_ref, qi_ref, w_ref, qs_ref, kicur_ref, kcur_ref, vcur_ref, ckidx_hbm, ck_hbm, cv_hbm,
                      o_ref, kidx_buf, k_buf, v_buf, sc_ref, sem_idx, sem_k, sem_v, *, n_pages, topk):
    b = pl.program_id(0)
    ppc = SAMPLE_PAGES_PER_CHUNK
    n_chunks = n_pages // ppc
    ck = ppc * PAGE_SIZE
    past = n_pages * PAGE_SIZE
    nt = (((1,), (1,)), ((), ()))

    def idx_copy(p):
        return pltpu.make_async_copy(ckidx_hbm.at[pt_ref[b, p]], kidx_buf.at[p], sem_idx.at[0])

    def kv_copies(c, p):
        slot = c % 2
        page = pt_ref[b, c * ppc + p]
        return (pltpu.make_async_copy(ck_hbm.at[page], k_buf.at[slot, p], sem_k.at[slot]),
                pltpu.make_async_copy(cv_hbm.at[page], v_buf.at[slot, p], sem_v.at[slot]))

    def start_kv(c):
        for p in range(ppc):
            for cp in kv_copies(c, p):
                cp.start()

    def wait_kv(c):
        for p in range(ppc):
            for cp in kv_copies(c, p):
                cp.wait()

    for p in range(n_pages):
        idx_copy(p).start()
    start_kv(0)

    row = lax.broadcasted_iota(I32, (PAGE_SIZE, 1), 0)
    kidx_buf[n_pages] = jnp.where(row == 0, kicur_ref[0][:, :IDX_DIM], 0.0)
    for p in range(n_pages):
        idx_copy(p).wait()

    qi = qi_ref[0][:, :IDX_DIM]
    w = w_ref[0]

    def scores(kflat):
        d = lax.dot_general(qi, kflat.astype(BF16), nt, preferred_element_type=F32)
        return jnp.sum(jnp.maximum(d, 0.0) * w, axis=0, keepdims=True)

    for c in range(n_chunks):
        kflat = kidx_buf[c * ppc:(c + 1) * ppc].reshape(ck, IDX_DIM)
        sc_ref[:, c * ck:(c + 1) * ck] = _sortable_key(scores(kflat))
    tail_lane = lax.broadcasted_iota(I32, (1, PAGE_SIZE), 1)
    tail_sc = jnp.where(tail_lane == 0, scores(kidx_buf[n_pages]), -jnp.inf)
    sc_ref[:, past:past + PAGE_SIZE] = _sortable_key(tail_sc)

    if past + 1 > topk:
        def count_ge(cand):
            return jnp.sum(jnp.where(sc_ref[...] >= cand, 1.0, 0.0), axis=1, keepdims=True)
        thr = _topk_threshold(count_ge, jnp.full((1, 1), INT_MIN, I32), float(topk))
    else:
        thr = jnp.full((1, 1), NEG_INF_KEY + 1, I32)

    qs = qs_ref[0]
    m = jnp.full((N_HEADS, 1), -jnp.inf, F32)
    l = jnp.zeros((N_HEADS, 1), F32)
    acc = jnp.zeros((N_HEADS, N_KV_HEADS * HEAD_DIM), F32)

    def update(m, l, acc, k_tile, v_tile, sel):
        s = lax.dot_general(qs, k_tile.astype(BF16), nt, preferred_element_type=F32)
        s = jnp.where(sel, s, -jnp.inf)
        m_new = jnp.maximum(m, jnp.max(s, axis=1, keepdims=True))
        m_safe = jnp.where(m_new == -jnp.inf, 0.0, m_new)
        alpha = jnp.exp(m - m_safe)
        p = jnp.exp(s - m_safe)
        l = alpha * l + jnp.sum(p, axis=1, keepdims=True)
        acc = alpha * acc + jnp.dot(p.astype(BF16), v_tile.astype(BF16), preferred_element_type=F32)
        return m_new, l, acc

    for c in range(n_chunks):
        if c + 1 < n_chunks:
            start_kv(c + 1)
        wait_kv(c)
        slot = c % 2
        sel = sc_ref[:, c * ck:(c + 1) * ck] >= thr
        m, l, acc = update(m, l, acc, k_buf[slot].reshape(ck, N_KV_HEADS * HEAD_DIM),
                           v_buf[slot].reshape(ck, N_KV_HEADS * HEAD_DIM), sel)
    k_tail = jnp.where(row == 0, kcur_ref[0], 0.0)
    v_tail = jnp.where(row == 0, vcur_ref[0], 0.0)
    m, l, acc = update(m, l, acc, k_tail, v_tail, sc_ref[:, past:past + PAGE_SIZE] >= thr)
    o_ref[0] = acc / l


def _sample_attention(page_table, qi_s, w_s, qs, kicur, kcur, vcur, cache_kidx, cache_k, cache_v):
    dec_batch, n_pages = page_table.shape
    past = n_pages * PAGE_SIZE
    topk = min(TOPK_MAX, (past + 1) // 4)
    kv_w = N_KV_HEADS * HEAD_DIM
    ppc = SAMPLE_PAGES_PER_CHUNK
    per_b = lambda b, pt: (b, 0, 0)
    any_spec = pl.BlockSpec(memory_space=pl.ANY)
    grid_spec = pltpu.PrefetchScalarGridSpec(
        num_scalar_prefetch=1,
        grid=(dec_batch,),
        in_specs=[
            pl.BlockSpec((1, IDX_HEADS, LANES), per_b),
            pl.BlockSpec((1, IDX_HEADS, 1), per_b),
            pl.BlockSpec((1, N_HEADS, kv_w), per_b),
            pl.BlockSpec((1, 1, LANES), per_b),
            pl.BlockSpec((1, 1, kv_w), per_b),
            pl.BlockSpec((1, 1, kv_w), per_b),
            any_spec, any_spec, any_spec,
        ],
        out_specs=pl.BlockSpec((1, N_HEADS, kv_w), per_b),
        scratch_shapes=[
            pltpu.VMEM((n_pages + 1, PAGE_SIZE, IDX_DIM), F32),
            pltpu.VMEM((2, ppc, PAGE_SIZE, kv_w), F32),
            pltpu.VMEM((2, ppc, PAGE_SIZE, kv_w), F32),
            pltpu.VMEM((1, past + PAGE_SIZE), I32),
            pltpu.SemaphoreType.DMA((1,)),
            pltpu.SemaphoreType.DMA((2,)),
            pltpu.SemaphoreType.DMA((2,)),
        ],
    )
    body = functools.partial(_sample_attn_body, n_pages=n_pages, topk=topk)
    return pl.pallas_call(
        body,
        grid_spec=grid_spec,
        out_shape=jax.ShapeDtypeStruct((dec_batch, N_HEADS, kv_w), F32),
        compiler_params=pltpu.CompilerParams(dimension_semantics=("arbitrary",), vmem_limit_bytes=VMEM_LIMIT),
        name="sample_attention",
    )(page_table, qi_s, w_s, qs, kicur, kcur, vcur, cache_kidx, cache_k, cache_v)


MERGE_COL_CHUNK = 512


def _merge_body(x_ref, a_ref, op_ref, os_ref, wga_ref, wgb_ref, wba_ref, wbb_ref, out_ref, *, n_prompt_tiles):
    i = pl.program_id(0)
    xb = x_ref[...].astype(BF16)
    a = a_ref[...]
    o = jnp.where(i < n_prompt_tiles, op_ref[...], os_ref[...])
    cc = MERGE_COL_CHUNK
    for c in range(out_ref.shape[1] // cc):
        cols = slice(c * cc, (c + 1) * cc)
        ga = jnp.dot(xb, wga_ref[:, cols], preferred_element_type=F32)
        gb = jnp.dot(xb, wgb_ref[:, cols], preferred_element_type=F32)
        ya = jnp.dot(a, wba_ref[:, cols], preferred_element_type=F32)
        yo = jnp.dot(o, wbb_ref[:, cols], preferred_element_type=F32)
        out_ref[:, cols] = (jax.nn.sigmoid(ga) * ya + jax.nn.sigmoid(gb) * yo).astype(BF16)


def _merge(x_all, a_all, o_prompt, o_sample, wga, wgb, wba, wbb):
    n_rows, d_model = x_all.shape
    tm = ROW_TILE
    n_prompt_tiles = o_prompt.shape[0] // tm
    row = lambda width: pl.BlockSpec((tm, width), lambda i: (i, 0))
    body = functools.partial(_merge_body, n_prompt_tiles=n_prompt_tiles)
    return pl.pallas_call(
        body,
        grid=(n_rows // tm,),
        in_specs=[row(d_model), row(a_all.shape[1]),
                  pl.BlockSpec((tm, o_prompt.shape[1]), lambda i: (jnp.minimum(i, n_prompt_tiles - 1), 0)),
                  _const_spec(o_sample.shape), _const_spec(wga.shape), _const_spec(wgb.shape),
                  _const_spec(wba.shape), _const_spec(wbb.shape)],
        out_specs=row(d_model),
        out_shape=jax.ShapeDtypeStruct((n_rows, d_model), BF16),
        compiler_params=pltpu.CompilerParams(dimension_semantics=("arbitrary",), vmem_limit_bytes=VMEM_LIMIT),
        name="branch_merge",
    )(x_all, a_all, o_prompt, o_sample, wga, wgb, wba, wbb)


def _layer_norm(x, g, b):
    mu = jnp.mean(x, axis=-1, keepdims=True)
    var = jnp.mean(jnp.square(x - mu), axis=-1, keepdims=True)
    return (x - mu) * lax.rsqrt(var + LN_EPS) * g + b


def _route(logits):
    lane = lax.broadcasted_iota(I32, logits.shape, 1)
    lane_f = lane.astype(F32)
    big = float(LANES)

    def masked_softmax(mask):
        x = jnp.where(mask, logits, -jnp.inf)
        e = jnp.exp(x - jnp.max(x, axis=1, keepdims=True))
        return jnp.where(mask, e / jnp.sum(e, axis=1, keepdims=True), -1.0)

    def top1(p):
        best = jnp.max(p, axis=1, keepdims=True)
        idx = jnp.min(jnp.where(p == best, lane_f, big), axis=1, keepdims=True)
        return best, idx

    gp = masked_softmax((lane >= N_EXPERTS) & (lane < N_EXPERTS + N_GROUPS))
    g_p, g_lane = top1(gp)
    g_idx = g_lane.astype(I32) - N_EXPERTS
    ep = masked_softmax((lane < N_EXPERTS) & ((lane >> int(math.log2(EXPERTS_PER_GROUP))) == g_idx))
    p1, i1 = top1(ep)
    p2, i2 = top1(jnp.where(lane_f == i1, -1.0, ep))
    denom = p1 + p2
    return jnp.where(lane_f == i1, g_p * p1 / denom, 0.0) + jnp.where(lane_f == i2, g_p * p2 / denom, 0.0)


def _outproj_body(m_ref, x_ref, wout_ref, g_ref, b_ref, wr_ref, h_ref, hb_ref, dense_ref):
    mo = jnp.dot(m_ref[...], wout_ref[...], preferred_element_type=F32)
    h = _layer_norm(ALPHA * x_ref[...] + mo, g_ref[...], b_ref[...])
    h_ref[...] = h
    hb = h.astype(BF16)
    hb_ref[...] = hb
    dense_ref[...] = _route(jnp.dot(hb, wr_ref[...], preferred_element_type=F32))


def _out_projection(merged, x_all, wout, ln_g, ln_b, w_route):
    n_rows, d_model = x_all.shape
    tm = ROW_TILE
    row = lambda width: pl.BlockSpec((tm, width), lambda i: (i, 0))
    return pl.pallas_call(
        _outproj_body,
        grid=(n_rows // tm,),
        in_specs=[row(d_model), row(d_model), _const_spec(wout.shape), _const_spec(ln_g.shape),
                  _const_spec(ln_b.shape), _const_spec(w_route.shape)],
        out_specs=[row(d_model), row(d_model), row(LANES)],
        out_shape=[jax.ShapeDtypeStruct((n_rows, d_model), F32),
                   jax.ShapeDtypeStruct((n_rows, d_model), BF16),
                   jax.ShapeDtypeStruct((n_rows, LANES), F32)],
        compiler_params=pltpu.CompilerParams(dimension_semantics=("arbitrary",), vmem_limit_bytes=VMEM_LIMIT),
        name="out_projection",
    )(merged, x_all, wout, ln_g, ln_b, w_route)


def _moe_body(hb_ref, dense_ref, h_ref, wg_ref, wu_ref, wd_ref, g_ref, b_ref, y_ref, acc_ref):
    e = pl.program_id(1)

    @pl.when(e == 0)
    def _():
        acc_ref[...] = jnp.zeros(acc_ref.shape, F32)

    hb = hb_ref[...]
    gate = jnp.dot(hb, wg_ref[0], preferred_element_type=F32)
    up = jnp.dot(hb, wu_ref[0], preferred_element_type=F32)
    lane = lax.broadcasted_iota(I32, dense_ref.shape, 1)
    dcol = jnp.sum(jnp.where(lane == e, dense_ref[...], 0.0), axis=1, keepdims=True)
    hh = (gate * jax.nn.sigmoid(gate)) * up * dcol
    acc_ref[...] += jnp.dot(hh.astype(BF16), wd_ref[0], preferred_element_type=F32)

    @pl.when(e == pl.num_programs(1) - 1)
    def _():
        y_ref[...] = _layer_norm(ALPHA * h_ref[...] + acc_ref[...], g_ref[...], b_ref[...])


def _moe(hb, dense, h, wg, wu, wd, ln_g, ln_b, *, tm, first_tile, n_tiles):
    d_model = h.shape[1]
    n_experts, _, d_expert = wg.shape
    row = lambda width: pl.BlockSpec((tm, width), lambda i, e: (i + first_tile, 0))
    return pl.pallas_call(
        _moe_body,
        grid=(n_tiles, n_experts),
        in_specs=[row(d_model), row(LANES), row(d_model),
                  pl.BlockSpec((1, d_model, d_expert), lambda i, e: (e, 0, 0)),
                  pl.BlockSpec((1, d_model, d_expert), lambda i, e: (e, 0, 0)),
                  pl.BlockSpec((1, d_expert, d_model), lambda i, e: (e, 0, 0)),
                  _const_spec(ln_g.shape), _const_spec(ln_b.shape)],
        out_specs=pl.BlockSpec((tm, d_model), lambda i, e: (i, 0)),
        out_shape=jax.ShapeDtypeStruct((n_tiles * tm, d_model), F32),
        scratch_shapes=[pltpu.VMEM((tm, d_model), F32)],
        compiler_params=pltpu.CompilerParams(dimension_semantics=("arbitrary", "arbitrary"),
                                             vmem_limit_bytes=VMEM_LIMIT),
        name="expert_mlp",
    )(hb, dense, h, wg, wu, wd, ln_g, ln_b)


def _layer(x_prompt, x_sample, cache_k, cache_v, cache_kidx, state_pool, page_table, w_in, w_pool, pool_scale,
           w_branch_a, w_branch_b, w_out, ln1_g, ln1_b, w_group, w_expert_router, w_gate, w_up, w_down,
           ln2_g, ln2_b):
    batch, seq, d_model = x_prompt.shape
    dec_batch = x_sample.shape[0]
    n_pages = page_table.shape[1]
    past = n_pages * PAGE_SIZE
    n_prompt = batch * seq
    kv_w = N_KV_HEADS * HEAD_DIM
    pool_width = w_pool.shape[0] * w_pool.shape[1]
    assert x_sample.shape[1] == 1 and seq % KEY_CHUNK == 0 and seq % MOE_ROW_TILE == 0
    assert dec_batch <= ROW_TILE and dec_batch % 8 == 0 and n_pages % SAMPLE_PAGES_PER_CHUNK == 0

    x_all = jnp.concatenate([x_prompt.reshape(n_prompt, d_model), x_sample.reshape(dec_batch, d_model),
                             jnp.zeros((ROW_TILE - dec_batch, d_model), F32)], axis=0)
    u, a_all, qz, qiz, k, v, kb, vb, kiwi, kib = _stage1(x_all, batch, seq, state_pool, w_in, w_pool,
                                                         pool_scale, past)
    o_prompt = _prompt_attention(qiz, kiwi, qz, kib, kb, vb, batch=batch, seq=seq)

    srows = slice(n_prompt, n_prompt + dec_batch)
    sblk = n_prompt // Q_TILE
    qi_s = jnp.transpose(qiz[sblk].reshape(IDX_HEADS, Q_TILE, LANES)[:, :dec_batch], (1, 0, 2))
    q_s = jnp.transpose(qz[sblk].reshape(N_HEADS, Q_TILE, LANES)[:, :dec_batch], (1, 0, 2))
    low_pair = (jnp.arange(N_HEADS) // (N_HEADS // N_KV_HEADS) < 2)[None, :, None]
    qs = jnp.concatenate([jnp.where(low_pair, q_s, 0), jnp.where(low_pair, 0, q_s)], axis=-1).astype(BF16)
    w_s = kiwi[srows, IDX_DIM:IDX_DIM + IDX_HEADS].reshape(dec_batch, IDX_HEADS, 1)
    o_s = _sample_attention(page_table, qi_s, w_s, qs, kiwi[srows].reshape(dec_batch, 1, LANES),
                            k[srows].reshape(dec_batch, 1, kv_w), v[srows].reshape(dec_batch, 1, kv_w),
                            cache_kidx, cache_k.reshape(cache_k.shape[0], PAGE_SIZE, kv_w),
                            cache_v.reshape(cache_v.shape[0], PAGE_SIZE, kv_w))
    group = N_HEADS // N_KV_HEADS
    o_s = o_s.reshape(dec_batch, N_KV_HEADS, group, N_KV_HEADS, HEAD_DIM)
    o_s = jnp.transpose(jnp.diagonal(o_s, axis1=1, axis2=3), (0, 3, 1, 2)).reshape(dec_batch, N_HEADS * HEAD_DIM)
    o_sample = jnp.concatenate([o_s, jnp.zeros((ROW_TILE - dec_batch, N_HEADS * HEAD_DIM), F32)]).astype(BF16)

    g_off = w_in.shape[1] - 2 * d_model
    merged = _merge(x_all, a_all, o_prompt, o_sample, w_in[:, g_off:g_off + d_model].astype(BF16),
                    w_in[:, g_off + d_model:].astype(BF16), w_branch_a.astype(BF16), w_branch_b.astype(BF16))
    w_route = jnp.concatenate([w_expert_router, w_group,
                               jnp.zeros((d_model, LANES - N_EXPERTS - N_GROUPS), F32)], axis=1).astype(BF16)
    h, hb, dense = _out_projection(merged, x_all, w_out.astype(BF16), ln1_g.reshape(1, d_model),
                                   ln1_b.reshape(1, d_model), w_route)
    wg, wu, wd = w_gate.astype(BF16), w_up.astype(BF16), w_down.astype(BF16)
    g2, b2 = ln2_g.reshape(1, d_model), ln2_b.reshape(1, d_model)
    y_prompt = _moe(hb, dense, h, wg, wu, wd, g2, b2, tm=MOE_ROW_TILE, first_tile=0,
                    n_tiles=n_prompt // MOE_ROW_TILE)
    y_sample = _moe(hb, dense, h, wg, wu, wd, g2, b2, tm=ROW_TILE, first_tile=n_prompt // ROW_TILE, n_tiles=1)

    u_p = u[:n_prompt].reshape(batch, seq, pool_width)
    pool_sample = jnp.concatenate([state_pool[:, 1:], u[srows][:, None, :]], axis=1)
    return (y_prompt.reshape(batch, seq, d_model), y_sample[:dec_batch].reshape(dec_batch, 1, d_model),
            k[:n_prompt].reshape(batch, seq, N_KV_HEADS, HEAD_DIM), v[:n_prompt].reshape(batch, seq, N_KV_HEADS, HEAD_DIM),
            kiwi[:n_prompt, :IDX_DIM].reshape(batch, seq, IDX_DIM), u_p[:, seq - POOL_STATE:],
            k[srows].reshape(dec_batch, 1, N_KV_HEADS, HEAD_DIM), v[srows].reshape(dec_batch, 1, N_KV_HEADS, HEAD_DIM),
            kiwi[srows, :IDX_DIM].reshape(dec_batch, 1, IDX_DIM), pool_sample)


def kernel(x_prompt, x_sample, cache_k, cache_v, cache_kidx, state_pool, page_table, w_in, w_pool, pool_scale,
           w_branch_a, w_branch_b, w_out, ln1_g, ln1_b, w_group, w_expert_router, w_gate, w_up, w_down,
           ln2_g, ln2_b):
    assert w_in.shape[0] == DEPTH
    outs = _layer(x_prompt, x_sample, cache_k[0], cache_v[0], cache_kidx[0], state_pool[0], page_table, w_in[0],
                  w_pool[0], pool_scale[0], w_branch_a[0], w_branch_b[0], w_out[0], ln1_g[0], ln1_b[0],
                  w_group[0], w_expert_router[0], w_gate[0], w_up[0], w_down[0], ln2_g[0], ln2_b[0])
    y_p, y_s = outs[0], outs[1]
    return (y_p, y_s) + tuple(o[None] for o in outs[2:])
```

```python
import functools
import math

import jax
import jax.numpy as jnp
import numpy as np
from jax import lax
from jax.experimental import pallas as pl
from jax.experimental.pallas import tpu as pltpu

BF16 = jnp.bfloat16
F32 = jnp.float32
I32 = jnp.int32

PAGE_SIZE = 128
POOL_WINDOWS = (2, 4, 8, 16)
POOL_STATE = 15
N_HEADS = 16
N_KV_HEADS = 4
HEAD_DIM = 64
ROT_DIM = HEAD_DIM // 4
ROPE_THETA = 500000.0
IDX_HEADS = 16
IDX_DIM = 64
IDX_W_SCALE = (IDX_HEADS * IDX_DIM) ** -0.5
TOPK_MAX = 256
N_GROUPS = 4
EXPERTS_PER_GROUP = 8
N_EXPERTS = N_GROUPS * EXPERTS_PER_GROUP
LN_EPS = 1e-5
DEPTH = 1
ALPHA = (2 * DEPTH) ** 0.25

LANES = 128
Q_TILE = 128
KEY_CHUNK = 512
ROW_TILE = 256
MOE_ROW_TILE = 512
VMEM_LIMIT = 56 * 1024 * 1024

INT_MIN = -2 ** 31
NEG_INF_KEY = int(np.int32(np.uint32(0xFF800000) ^ np.uint32(0x7FFFFFFF)))


def _sortable_key(x):
    bits = lax.bitcast_convert_type(x, I32)
    return bits ^ ((bits >> 31) & 0x7FFFFFFF)


def _const_spec(shape):
    nd = len(shape)
    return pl.BlockSpec(shape, lambda *_: (0,) * nd, pipeline_mode=pl.Buffered(1))


def _inproj_body(x_ref, w_ref, tc_ref, ts1_ref, ts2_ref, wpool_ref, pscale_ref, state_ref,
                 u_ref, a_ref, qz_ref, qiz_ref, k_ref, v_ref, kb_ref, vb_ref, kiwi_ref, kib_ref,
                 uext_ref, *, tm, tiles_per_batch, n_prompt_tiles, dec_batch, sample_pos, pool_width):
    i = pl.program_id(0)
    xb = x_ref[...].astype(BF16)
    cos = tc_ref[...]
    nsin = ts1_ref[...]
    psin = ts2_ref[...]
    lane = lax.broadcasted_iota(I32, (tm, LANES), 1)
    lo = lane < HEAD_DIM
    gw = pool_width // len(POOL_WINDOWS)
    attn_w = N_HEADS * HEAD_DIM
    kv_w = N_KV_HEADS * HEAD_DIM
    off_q = pool_width
    off_k = off_q + attn_w
    off_v = off_k + kv_w
    off_qi = off_v + kv_w
    off_kiwi = off_qi + IDX_HEADS * IDX_DIM

    def proj(c0, width):
        return jnp.dot(xb, w_ref[:, c0:c0 + width], preferred_element_type=F32)

    def rope(z, c, s1, s2):
        return z * c + pltpu.roll(z, LANES - ROT_DIM // 2, 1) * s1 + pltpu.roll(z, ROT_DIM // 2, 1) * s2

    def store_heads(dst_ref, h, tile):
        for blk in range(tm // Q_TILE):
            dst_ref[blk, h * Q_TILE:(h + 1) * Q_TILE, :] = tile[blk * Q_TILE:(blk + 1) * Q_TILE].astype(BF16)

    u = proj(0, pool_width)
    u_ref[...] = u

    zq = proj(off_q, attn_w)
    for j in range(N_HEADS // 2):
        p = rope(zq[:, j * LANES:(j + 1) * LANES], cos, nsin, psin) * (HEAD_DIM ** -0.5)
        pr = pltpu.roll(p, HEAD_DIM, 1)
        if (j // 2) % 2 == 0:
            store_heads(qz_ref, 2 * j, jnp.where(lo, p, 0.0))
            store_heads(qz_ref, 2 * j + 1, jnp.where(lo, pr, 0.0))
        else:
            store_heads(qz_ref, 2 * j, jnp.where(lo, 0.0, pr))
            store_heads(qz_ref, 2 * j + 1, jnp.where(lo, 0.0, p))

    zqi = proj(off_qi, IDX_HEADS * IDX_DIM)
    for j in range(IDX_HEADS // 2):
        p = rope(zqi[:, j * LANES:(j + 1) * LANES], cos, nsin, psin)
        pr = pltpu.roll(p, IDX_DIM, 1)
        store_heads(qiz_ref, 2 * j, jnp.where(lo, p, 0.0))
        store_heads(qiz_ref, 2 * j + 1, jnp.where(lo, pr, 0.0))

    zk = proj(off_k, kv_w)
    for j in range(kv_w // LANES):
        kr = rope(zk[:, j * LANES:(j + 1) * LANES], cos, nsin, psin)
        k_ref[:, j * LANES:(j + 1) * LANES] = kr
        kb_ref[:, j * LANES:(j + 1) * LANES] = kr.astype(BF16)

    zv = proj(off_v, kv_w)
    v_ref[...] = zv
    vb_ref[...] = zv.astype(BF16)

    zkw = proj(off_kiwi, LANES)
    kw = rope(zkw, jnp.where(lo, cos, IDX_W_SCALE), jnp.where(lo, nsin, 0.0), jnp.where(lo, psin, 0.0))
    kiwi_ref[...] = kw
    kib_ref[...] = kw.astype(BF16)

    def mix(d, g):
        lanes = slice(g * gw, (g + 1) * gw)
        z = jnp.dot(d.astype(BF16), wpool_ref[g], preferred_element_type=F32)
        return z * pscale_ref[:, lanes]

    @pl.when(i < n_prompt_tiles)
    def _():
        first = (i % tiles_per_batch) == 0

        @pl.when(first)
        def _():
            uext_ref[0:16, :] = jnp.zeros((16, pool_width), F32)

        @pl.when(jnp.logical_not(first))
        def _():
            uext_ref[0:16, :] = uext_ref[tm:tm + 16, :]

        uext_ref[16:16 + tm, :] = u
        pos = (i % tiles_per_batch) * tm + lax.broadcasted_iota(I32, (tm, 1), 0)
        for g, w in enumerate(POOL_WINDOWS):
            lanes = slice(g * gw, (g + 1) * gw)
            acc = u[:, lanes]
            for jj in range(1, w):
                acc = acc + uext_ref[16 - jj:16 - jj + tm, lanes]
            cnt = jnp.minimum(w, pos + 1).astype(F32)
            d = acc / cnt - u[:, lanes]
            a_ref[:, lanes] = mix(d, g).astype(BF16)

    @pl.when(i == n_prompt_tiles)
    def _():
        a_ref[...] = jnp.zeros((tm, pool_width), BF16)
        us = u[0:dec_batch, :]
        for g, w in enumerate(POOL_WINDOWS):
            lanes = slice(g * gw, (g + 1) * gw)
            acc = us[:, lanes]
            for jj in range(1, w):
                acc = acc + state_ref[POOL_STATE - jj, :, lanes]
            d = acc / float(min(w, sample_pos + 1)) - us[:, lanes]
            a_ref[0:dec_batch, lanes] = mix(d, g).astype(BF16)


def _in_projection(x_all, w_a, tab_c, tab_s1, tab_s2, wpool, pscale, state_t, *, seq, n_prompt, sample_pos):
    n_rows, d_model = x_all.shape
    tm = ROW_TILE
    pool_width = wpool.shape[0] * wpool.shape[1]
    dec_batch = state_t.shape[1]
    n_tiles = n_rows // tm
    nblk = n_rows // Q_TILE
    kv_w = N_KV_HEADS * HEAD_DIM
    row = lambda width: pl.BlockSpec((tm, width), lambda i: (i, 0))
    hm = pl.BlockSpec((tm // Q_TILE, N_HEADS * Q_TILE, LANES), lambda i: (i, 0, 0))
    body = functools.partial(_inproj_body, tm=tm, tiles_per_batch=seq // tm, n_prompt_tiles=n_prompt // tm,
                             dec_batch=dec_batch, sample_pos=sample_pos, pool_width=pool_width)
    return pl.pallas_call(
        body,
        grid=(n_tiles,),
        in_specs=[row(d_model), _const_spec(w_a.shape), row(LANES), row(LANES), row(LANES),
                  _const_spec(wpool.shape), _const_spec(pscale.shape), _const_spec(state_t.shape)],
        out_specs=[row(pool_width), row(pool_width), hm, hm, row(kv_w), row(kv_w), row(kv_w), row(kv_w),
                   row(LANES), row(LANES)],
        out_shape=[
            jax.ShapeDtypeStruct((n_rows, pool_width), F32),
            jax.ShapeDtypeStruct((n_rows, pool_width), BF16),
            jax.ShapeDtypeStruct((nblk, N_HEADS * Q_TILE, LANES), BF16),
            jax.ShapeDtypeStruct((nblk, IDX_HEADS * Q_TILE, LANES), BF16),
            jax.ShapeDtypeStruct((n_rows, kv_w), F32),
            jax.ShapeDtypeStruct((n_rows, kv_w), F32),
            jax.ShapeDtypeStruct((n_rows, kv_w), BF16),
            jax.ShapeDtypeStruct((n_rows, kv_w), BF16),
            jax.ShapeDtypeStruct((n_rows, LANES), F32),
            jax.ShapeDtypeStruct((n_rows, LANES), BF16),
        ],
        scratch_shapes=[pltpu.VMEM((tm + 16, pool_width), F32)],
        compiler_params=pltpu.CompilerParams(dimension_semantics=("arbitrary",), vmem_limit_bytes=VMEM_LIMIT),
        name="in_projection",
    )(x_all, w_a, tab_c, tab_s1, tab_s2, wpool, pscale, state_t)


def _rope_tables(pos):
    n = pos.shape[0]
    half = ROT_DIM // 2
    inv = jnp.exp(-jnp.arange(half, dtype=F32) * (math.log(ROPE_THETA) / half))
    ang = pos.astype(F32)[:, None] * inv[None, :]
    cos, sin = jnp.cos(ang), jnp.sin(ang)
    zeros = jnp.zeros((n, half), F32)
    rest = HEAD_DIM - ROT_DIM
    c = jnp.concatenate([cos, cos, jnp.ones((n, rest), F32)], axis=1)
    s1 = jnp.concatenate([-sin, zeros, jnp.zeros((n, rest), F32)], axis=1)
    s2 = jnp.concatenate([zeros, sin, jnp.zeros((n, rest), F32)], axis=1)
    return tuple(jnp.concatenate([t, t], axis=1) for t in (c, s1, s2))


def _stage1(x_all, batch, seq, state_pool, w_in, w_pool, pool_scale, past_len):
    d_model = x_all.shape[1]
    dec_batch = state_pool.shape[0]
    n_prompt = batch * seq
    pad = x_all.shape[0] - n_prompt - dec_batch
    pos = jnp.concatenate([jnp.tile(jnp.arange(seq, dtype=I32), batch), jnp.full((dec_batch,), past_len, I32),
                           jnp.zeros((pad,), I32)])
    tab_c, tab_s1, tab_s2 = _rope_tables(pos)
    pool_width = w_pool.shape[0] * w_pool.shape[1]
    a_width = pool_width + N_HEADS * HEAD_DIM + 2 * N_KV_HEADS * HEAD_DIM + IDX_HEADS * IDX_DIM + IDX_DIM + IDX_HEADS
    w_a = jnp.concatenate([w_in[:, :a_width], jnp.zeros((d_model, -a_width % LANES), F32)], axis=1).astype(BF16)
    state_t = jnp.transpose(state_pool, (1, 0, 2))
    return _in_projection(x_all, w_a, tab_c, tab_s1, tab_s2, w_pool.astype(BF16),
                          pool_scale.reshape(1, pool_width), state_t,
                          seq=seq, n_prompt=n_prompt, sample_pos=past_len)


def _topk_threshold(count_ge, init, topk):
    def step(it, t):
        cand = t + jnp.left_shift(jnp.int32(1), 31 - it)
        return jnp.where(count_ge(cand) >= topk, cand, t)
    return lax.fori_loop(0, 32, step, init)


def _flash_update(s, v_tile, m_ref, l_ref, acc_ref, rows):
    m_old = m_ref[rows, :]
    m_new = jnp.maximum(m_old, jnp.max(s, axis=1, keepdims=True))
    m_safe = jnp.where(m_new == -jnp.inf, 0.0, m_new)
    alpha = jnp.exp(m_old - m_safe)
    p = jnp.exp(s - m_safe)
    l_ref[rows, :] = alpha * l_ref[rows, :] + jnp.sum(p, axis=1, keepdims=True)
    acc_ref[rows, :] = alpha * acc_ref[rows, :] + jnp.dot(p.astype(BF16), v_tile, preferred_element_type=F32)
    m_ref[rows, :] = m_new


def _prompt_attn_body(qiz_ref, w_ref, qz_ref, kib_ref, kb_ref, vb_ref, o_ref,
                      sc_ref, thr_ref, m_ref, l_ref, acc_ref, *, topk):
    i = pl.program_id(1)
    tq, kc = Q_TILE, KEY_CHUNK
    n_keys = (i + 1) * tq
    n_chunks = (n_keys + kc - 1) // kc
    qpos = i * tq + lax.broadcasted_iota(I32, (tq, 1), 0)
    heads_per_dot = 4
    nt = (((1,), (1,)), ((), ()))

    def score_chunk(c, carry):
        k0 = pl.multiple_of(c * kc, kc)
        kchunk = kib_ref[pl.ds(k0, kc), :]
        acc = jnp.zeros((tq, kc), F32)
        for hg in range(IDX_HEADS // heads_per_dot):
            d = lax.dot_general(qiz_ref[0, hg * heads_per_dot * tq:(hg + 1) * heads_per_dot * tq, :], kchunk, nt,
                                preferred_element_type=F32)
            for r in range(heads_per_dot):
                h = hg * heads_per_dot + r
                wcol = w_ref[:, IDX_DIM + h:IDX_DIM + h + 1]
                acc = acc + jnp.maximum(d[r * tq:(r + 1) * tq], 0.0) * wcol
        kpos = k0 + lax.broadcasted_iota(I32, (tq, kc), 1)
        sc = jnp.where(kpos <= qpos, acc, -jnp.inf)
        sc_ref[:, pl.ds(k0, kc)] = _sortable_key(sc)
        return carry

    lax.fori_loop(0, n_chunks, score_chunk, 0)

    thr_ref[...] = jnp.full((tq, 1), NEG_INF_KEY + 1, I32)

    @pl.when(n_keys > topk)
    def _():
        def count_ge(cand):
            candb = jnp.broadcast_to(cand, (tq, LANES))

            def blk(c, acc):
                for cb in range(kc // LANES):
                    k0 = pl.multiple_of(c * kc + cb * LANES, LANES)
                    acc = acc + jnp.where(sc_ref[:, pl.ds(k0, LANES)] >= candb, 1.0, 0.0)
                return acc
            acc = lax.fori_loop(0, n_chunks, blk, jnp.zeros((tq, LANES), F32))
            return jnp.sum(acc, axis=1, keepdims=True)

        t = _topk_threshold(count_ge, jnp.full((tq, 1), INT_MIN, I32), float(topk))
        thr_ref[...] = jnp.where(qpos + 1 > topk, t, NEG_INF_KEY + 1)

    m_ref[...] = jnp.full(m_ref.shape, -jnp.inf, F32)
    l_ref[...] = jnp.zeros(l_ref.shape, F32)
    acc_ref[...] = jnp.zeros(acc_ref.shape, F32)
    group = N_HEADS // N_KV_HEADS
    grows = group * tq

    def attn_chunk(c, carry):
        k0 = pl.multiple_of(c * kc, kc)
        sel = sc_ref[:, pl.ds(k0, kc)] >= thr_ref[...]
        for g in range(N_KV_HEADS):
            pair = slice((g // 2) * LANES, (g // 2 + 1) * LANES)
            rows = slice(g * grows, (g + 1) * grows)
            s = lax.dot_general(qz_ref[0, rows, :], kb_ref[pl.ds(k0, kc), pair], nt, preferred_element_type=F32)
            s = jnp.concatenate([jnp.where(sel, s[r * tq:(r + 1) * tq], -jnp.inf) for r in range(group)], axis=0)
            _flash_update(s, vb_ref[pl.ds(k0, kc), pair], m_ref, l_ref, acc_ref, rows)
        return carry

    lax.fori_loop(0, n_chunks, attn_chunk, 0)

    lane = lax.broadcasted_iota(I32, (tq, LANES), 1)
    lo = lane < HEAD_DIM
    for j in range(N_HEADS // 2):
        r0 = slice(2 * j * tq, (2 * j + 1) * tq)
        r1 = slice((2 * j + 1) * tq, (2 * j + 2) * tq)
        o0 = acc_ref[r0, :] / l_ref[r0, :]
        o1 = acc_ref[r1, :] / l_ref[r1, :]
        if (j // 2) % 2 == 0:
            tile = jnp.where(lo, o0, pltpu.roll(o1, HEAD_DIM, 1))
        else:
            tile = jnp.where(lo, pltpu.roll(o0, HEAD_DIM, 1), o1)
        o_ref[:, j * LANES:(j + 1) * LANES] = tile.astype(BF16)


def _prompt_attention(qiz, kiwi, qz, kib, kb, vb, *, batch, seq):
    nq = seq // Q_TILE
    topk = min(TOPK_MAX, seq // 4)
    kv_w = N_KV_HEADS * HEAD_DIM
    per_q = lambda b, i: (b * nq + i, 0, 0)
    body = functools.partial(_prompt_attn_body, topk=topk)
    return pl.pallas_call(
        body,
        grid=(batch, nq),
        in_specs=[
            pl.BlockSpec((1, IDX_HEADS * Q_TILE, LANES), per_q),
            pl.BlockSpec((Q_TILE, LANES), lambda b, i: (b * nq + i, 0)),
            pl.BlockSpec((1, N_HEADS * Q_TILE, LANES), per_q),
            pl.BlockSpec((seq, LANES), lambda b, i: (b, 0)),
            pl.BlockSpec((seq, kv_w), lambda b, i: (b, 0)),
            pl.BlockSpec((seq, kv_w), lambda b, i: (b, 0)),
        ],
        out_specs=pl.BlockSpec((Q_TILE, N_HEADS * HEAD_DIM), lambda b, i: (b * nq + i, 0)),
        out_shape=jax.ShapeDtypeStruct((batch * seq, N_HEADS * HEAD_DIM), BF16),
        scratch_shapes=[
            pltpu.VMEM((Q_TILE, seq), I32),
            pltpu.VMEM((Q_TILE, 1), I32),
            pltpu.VMEM((N_HEADS * Q_TILE, 1), F32),
            pltpu.VMEM((N_HEADS * Q_TILE, 1), F32),
            pltpu.VMEM((N_HEADS * Q_TILE, LANES), F32),
        ],
        compiler_params=pltpu.CompilerParams(dimension_semantics=("arbitrary", "arbitrary"),
                                             vmem_limit_bytes=VMEM_LIMIT),
        name="prompt_attention",
    )(qiz, kiwi, qz, kib, kb, vb)


SAMPLE_PAGES_PER_CHUNK = 16


def _sample_attn_body(pt_ref, qi_ref, w_ref, qs_ref, kicur_ref, kcur_ref, vcur_ref, ckidx_hbm, ck_hbm, cv_hbm,
                      o_ref, kidx_buf, k_buf, v_buf, sc_ref, sem_idx, sem_k, sem_v, *, n_pages, topk):
    b = pl.program_id(0)
    ppc = SAMPLE_PAGES_PER_CHUNK
    n_chunks = n_pages // ppc
    ck = ppc * PAGE_SIZE
    past = n_pages * PAGE_SIZE
    nt = (((1,), (1,)), ((), ()))

    def idx_copy(p):
        return pltpu.make_async_copy(ckidx_hbm.at[pt_ref[b, p]],
                                     kidx_buf.at[:, pl.ds(p * PAGE_SIZE, PAGE_SIZE)], sem_idx.at[0])

    def kv_copies(c, p):
        slot = c % 2
        page = pt_ref[b, c * ppc + p]
        window = pl.ds(p * PAGE_SIZE, PAGE_SIZE)
        return (pltpu.make_async_copy(ck_hbm.at[page], k_buf.at[slot, :, window], sem_k.at[slot]),
                pltpu.make_async_copy(cv_hbm.at[page], v_buf.at[slot, :, window], sem_v.at[slot]))

    def start_kv(c):
        for p in range(ppc):
            for cp in kv_copies(c, p):
                cp.start()

    def wait_kv(c):
        for p in range(ppc):
            for cp in kv_copies(c, p):
                cp.wait()

    for p in range(n_pages):
        idx_copy(p).start()
    start_kv(0)

    for p in range(n_pages):
        idx_copy(p).wait()

    qi = qi_ref[0][:, :IDX_DIM]
    w = w_ref[0]

    def head_mix(d):
        return jnp.sum(jnp.maximum(d, 0.0) * w, axis=0, keepdims=True)

    def bf16_products(a, row):
        return jnp.sum(a.astype(F32) * row.astype(BF16).astype(F32), axis=1, keepdims=True)

    for c in range(n_chunks):
        d = jnp.dot(qi, kidx_buf[:, c * ck:(c + 1) * ck].astype(BF16), preferred_element_type=F32)
        sc_ref[:, c * ck:(c + 1) * ck] = _sortable_key(head_mix(d))
    cur_key = _sortable_key(head_mix(bf16_products(qi, kicur_ref[0][:, :IDX_DIM])))
    tail_lane = lax.broadcasted_iota(I32, (1, PAGE_SIZE), 1)
    sc_ref[:, past:past + PAGE_SIZE] = jnp.where(tail_lane == 0, cur_key, INT_MIN)

    if past + 1 > topk:
        def count_ge(cand):
            return jnp.sum(jnp.where(sc_ref[...] >= cand, 1.0, 0.0), axis=1, keepdims=True)
        thr = _topk_threshold(count_ge, jnp.full((1, 1), INT_MIN, I32), float(topk))
    else:
        thr = jnp.full((1, 1), NEG_INF_KEY + 1, I32)

    qs = qs_ref[0]
    m = jnp.full((N_HEADS, 1), -jnp.inf, F32)
    l = jnp.zeros((N_HEADS, 1), F32)
    acc = jnp.zeros((N_HEADS, N_KV_HEADS * HEAD_DIM), F32)

    def update(m, l, acc, s, sel, pv):
        s = jnp.where(sel, s, -jnp.inf)
        m_new = jnp.maximum(m, jnp.max(s, axis=1, keepdims=True))
        m_safe = jnp.where(m_new == -jnp.inf, 0.0, m_new)
        alpha = jnp.exp(m - m_safe)
        p = jnp.exp(s - m_safe)
        l = alpha * l + jnp.sum(p, axis=1, keepdims=True)
        acc = alpha * acc + pv(p.astype(BF16))
        return m_new, l, acc

    for c in range(n_chunks):
        if c + 1 < n_chunks:
            start_kv(c + 1)
        wait_kv(c)
        slot = c % 2
        s = jnp.dot(qs, k_buf[slot].astype(BF16), preferred_element_type=F32)
        vt = v_buf[slot].astype(BF16)
        m, l, acc = update(m, l, acc, s, sc_ref[:, c * ck:(c + 1) * ck] >= thr,
                           lambda p: lax.dot_general(p, vt, nt, preferred_element_type=F32))
    v_cur = vcur_ref[0].astype(BF16).astype(F32)
    m, l, acc = update(m, l, acc, bf16_products(qs, kcur_ref[0]), cur_key >= thr,
                       lambda p: p.astype(F32) * v_cur)
    o_ref[0] = acc / l


def _keys_minor(cache):
    pages, page_size = cache.shape[:2]
    return jnp.moveaxis(cache, 1, -1).reshape(pages, -1, page_size)


def _sample_attention(page_table, qi_s, w_s, qs, kicur, kcur, vcur, cache_kidx, cache_k, cache_v):
    dec_batch, n_pages = page_table.shape
    past = n_pages * PAGE_SIZE
    topk = min(TOPK_MAX, (past + 1) // 4)
    kv_w = N_KV_HEADS * HEAD_DIM
    ppc = SAMPLE_PAGES_PER_CHUNK
    per_b = lambda b, pt: (b, 0, 0)
    any_spec = pl.BlockSpec(memory_space=pl.ANY)
    grid_spec = pltpu.PrefetchScalarGridSpec(
        num_scalar_prefetch=1,
        grid=(dec_batch,),
        in_specs=[
            pl.BlockSpec((1, IDX_HEADS, LANES), per_b),
            pl.BlockSpec((1, IDX_HEADS, 1), per_b),
            pl.BlockSpec((1, N_HEADS, kv_w), per_b),
            pl.BlockSpec((1, 1, LANES), per_b),
            pl.BlockSpec((1, 1, kv_w), per_b),
            pl.BlockSpec((1, 1, kv_w), per_b),
            any_spec, any_spec, any_spec,
        ],
        out_specs=pl.BlockSpec((1, N_HEADS, kv_w), per_b),
        scratch_shapes=[
            pltpu.VMEM((IDX_DIM, past), F32),
            pltpu.VMEM((2, kv_w, ppc * PAGE_SIZE), F32),
            pltpu.VMEM((2, kv_w, ppc * PAGE_SIZE), F32),
            pltpu.VMEM((1, past + PAGE_SIZE), I32),
            pltpu.SemaphoreType.DMA((1,)),
            pltpu.SemaphoreType.DMA((2,)),
            pltpu.SemaphoreType.DMA((2,)),
        ],
    )
    body = functools.partial(_sample_attn_body, n_pages=n_pages, topk=topk)
    return pl.pallas_call(
        body,
        grid_spec=grid_spec,
        out_shape=jax.ShapeDtypeStruct((dec_batch, N_HEADS, kv_w), F32),
        compiler_params=pltpu.CompilerParams(dimension_semantics=("arbitrary",), vmem_limit_bytes=VMEM_LIMIT),
        name="sample_attention",
    )(page_table, qi_s, w_s, qs, kicur, kcur, vcur, cache_kidx, cache_k, cache_v)


MERGE_COL_CHUNK = 512


def _merge_body(x_ref, a_ref, op_ref, os_ref, wga_ref, wgb_ref, wba_ref, wbb_ref, out_ref, *, n_prompt_tiles):
    i = pl.program_id(0)
    xb = x_ref[...].astype(BF16)
    a = a_ref[...]
    o = jnp.where(i < n_prompt_tiles, op_ref[...], os_ref[...])
    cc = MERGE_COL_CHUNK
    for c in range(out_ref.shape[1] // cc):
        cols = slice(c * cc, (c + 1) * cc)
        ga = jnp.dot(xb, wga_ref[:, cols], preferred_element_type=F32)
        gb = jnp.dot(xb, wgb_ref[:, cols], preferred_element_type=F32)
        ya = jnp.dot(a, wba_ref[:, cols], preferred_element_type=F32)
        yo = jnp.dot(o, wbb_ref[:, cols], preferred_element_type=F32)
        out_ref[:, cols] = (jax.nn.sigmoid(ga) * ya + jax.nn.sigmoid(gb) * yo).astype(BF16)


def _merge(x_all, a_all, o_prompt, o_sample, wga, wgb, wba, wbb):
    n_rows, d_model = x_all.shape
    tm = ROW_TILE
    n_prompt_tiles = o_prompt.shape[0] // tm
    row = lambda width: pl.BlockSpec((tm, width), lambda i: (i, 0))
    body = functools.partial(_merge_body, n_prompt_tiles=n_prompt_tiles)
    return pl.pallas_call(
        body,
        grid=(n_rows // tm,),
        in_specs=[row(d_model), row(a_all.shape[1]),
                  pl.BlockSpec((tm, o_prompt.shape[1]), lambda i: (jnp.minimum(i, n_prompt_tiles - 1), 0)),
                  _const_spec(o_sample.shape), _const_spec(wga.shape), _const_spec(wgb.shape),
                  _const_spec(wba.shape), _const_spec(wbb.shape)],
        out_specs=row(d_model),
        out_shape=jax.ShapeDtypeStruct((n_rows, d_model), BF16),
        compiler_params=pltpu.CompilerParams(dimension_semantics=("arbitrary",), vmem_limit_bytes=VMEM_LIMIT),
        name="branch_merge",
    )(x_all, a_all, o_prompt, o_sample, wga, wgb, wba, wbb)


def _layer_norm(x, g, b):
    mu = jnp.mean(x, axis=-1, keepdims=True)
    var = jnp.mean(jnp.square(x - mu), axis=-1, keepdims=True)
    return (x - mu) * lax.rsqrt(var + LN_EPS) * g + b


def _route(logits):
    lane = lax.broadcasted_iota(I32, logits.shape, 1)
    lane_f = lane.astype(F32)
    big = float(LANES)

    def masked_softmax(mask):
        x = jnp.where(mask, logits, -jnp.inf)
        e = jnp.exp(x - jnp.max(x, axis=1, keepdims=True))
        return jnp.where(mask, e / jnp.sum(e, axis=1, keepdims=True), -1.0)

    def top1(p):
        best = jnp.max(p, axis=1, keepdims=True)
        idx = jnp.min(jnp.where(p == best, lane_f, big), axis=1, keepdims=True)
        return best, idx

    gp = masked_softmax((lane >= N_EXPERTS) & (lane < N_EXPERTS + N_GROUPS))
    g_p, g_lane = top1(gp)
    g_idx = g_lane.astype(I32) - N_EXPERTS
    ep = masked_softmax((lane < N_EXPERTS) & ((lane >> int(math.log2(EXPERTS_PER_GROUP))) == g_idx))
    p1, i1 = top1(ep)
    p2, i2 = top1(jnp.where(lane_f == i1, -1.0, ep))
    denom = p1 + p2
    return jnp.where(lane_f == i1, g_p * p1 / denom, 0.0) + jnp.where(lane_f == i2, g_p * p2 / denom, 0.0)


def _outproj_body(m_ref, x_ref, wout_ref, g_ref, b_ref, wr_ref, h_ref, hb_ref, dense_ref):
    mo = jnp.dot(m_ref[...], wout_ref[...], preferred_element_type=F32)
    h = _layer_norm(ALPHA * x_ref[...] + mo, g_ref[...], b_ref[...])
    h_ref[...] = h
    hb = h.astype(BF16)
    hb_ref[...] = hb
    dense_ref[...] = _route(jnp.dot(hb, wr_ref[...], preferred_element_type=F32))


def _out_projection(merged, x_all, wout, ln_g, ln_b, w_route):
    n_rows, d_model = x_all.shape
    tm = ROW_TILE
    row = lambda width: pl.BlockSpec((tm, width), lambda i: (i, 0))
    return pl.pallas_call(
        _outproj_body,
        grid=(n_rows // tm,),
        in_specs=[row(d_model), row(d_model), _const_spec(wout.shape), _const_spec(ln_g.shape),
                  _const_spec(ln_b.shape), _const_spec(w_route.shape)],
        out_specs=[row(d_model), row(d_model), row(LANES)],
        out_shape=[jax.ShapeDtypeStruct((n_rows, d_model), F32),
                   jax.ShapeDtypeStruct((n_rows, d_model), BF16),
                   jax.ShapeDtypeStruct((n_rows, LANES), F32)],
        compiler_params=pltpu.CompilerParams(dimension_semantics=("arbitrary",), vmem_limit_bytes=VMEM_LIMIT),
        name="out_projection",
    )(merged, x_all, wout, ln_g, ln_b, w_route)


def _moe_body(hb_ref, dense_ref, h_ref, wg_ref, wu_ref, wd_ref, g_ref, b_ref, y_ref, acc_ref):
    e = pl.program_id(1)

    @pl.when(e == 0)
    def _():
        acc_ref[...] = jnp.zeros(acc_ref.shape, F32)

    hb = hb_ref[...]
    gate = jnp.dot(hb, wg_ref[0], preferred_element_type=F32)
    up = jnp.dot(hb, wu_ref[0], preferred_element_type=F32)
    lane = lax.broadcasted_iota(I32, dense_ref.shape, 1)
    dcol = jnp.sum(jnp.where(lane == e, dense_ref[...], 0.0), axis=1, keepdims=True)
    hh = (gate * jax.nn.sigmoid(gate)) * up * dcol
    acc_ref[...] += jnp.dot(hh.astype(BF16), wd_ref[0], preferred_element_type=F32)

    @pl.when(e == pl.num_programs(1) - 1)
    def _():
        y_ref[...] = _layer_norm(ALPHA * h_ref[...] + acc_ref[...], g_ref[...], b_ref[...])


def _moe(hb, dense, h, wg, wu, wd, ln_g, ln_b, *, tm, first_tile, n_tiles):
    d_model = h.shape[1]
    n_experts, _, d_expert = wg.shape
    row = lambda width: pl.BlockSpec((tm, width), lambda i, e: (i + first_tile, 0))
    return pl.pallas_call(
        _moe_body,
        grid=(n_tiles, n_experts),
        in_specs=[row(d_model), row(LANES), row(d_model),
                  pl.BlockSpec((1, d_model, d_expert), lambda i, e: (e, 0, 0)),
                  pl.BlockSpec((1, d_model, d_expert), lambda i, e: (e, 0, 0)),
                  pl.BlockSpec((1, d_expert, d_model), lambda i, e: (e, 0, 0)),
                  _const_spec(ln_g.shape), _const_spec(ln_b.shape)],
        out_specs=pl.BlockSpec((tm, d_model), lambda i, e: (i, 0)),
        out_shape=jax.ShapeDtypeStruct((n_tiles * tm, d_model), F32),
        scratch_shapes=[pltpu.VMEM((tm, d_model), F32)],
        compiler_params=pltpu.CompilerParams(dimension_semantics=("arbitrary", "arbitrary"),
                                             vmem_limit_bytes=VMEM_LIMIT),
        name="expert_mlp",
    )(hb, dense, h, wg, wu, wd, ln_g, ln_b)


def _layer(x_prompt, x_sample, cache_k, cache_v, cache_kidx, state_pool, page_table, w_in, w_pool, pool_scale,
           w_branch_a, w_branch_b, w_out, ln1_g, ln1_b, w_group, w_expert_router, w_gate, w_up, w_down,
           ln2_g, ln2_b):
    batch, seq, d_model = x_prompt.shape
    dec_batch = x_sample.shape[0]
    n_pages = page_table.shape[1]
    past = n_pages * PAGE_SIZE
    n_prompt = batch * seq
    kv_w = N_KV_HEADS * HEAD_DIM
    pool_width = w_pool.shape[0] * w_pool.shape[1]
    assert x_sample.shape[1] == 1 and seq % KEY_CHUNK == 0 and seq % MOE_ROW_TILE == 0
    assert dec_batch <= ROW_TILE and dec_batch % 8 == 0 and n_pages % SAMPLE_PAGES_PER_CHUNK == 0

    x_all = jnp.concatenate([x_prompt.reshape(n_prompt, d_model), x_sample.reshape(dec_batch, d_model),
                             jnp.zeros((ROW_TILE - dec_batch, d_model), F32)], axis=0)
    u, a_all, qz, qiz, k, v, kb, vb, kiwi, kib = _stage1(x_all, batch, seq, state_pool, w_in, w_pool,
                                                         pool_scale, past)
    o_prompt = _prompt_attention(qiz, kiwi, qz, kib, kb, vb, batch=batch, seq=seq)

    srows = slice(n_prompt, n_prompt + dec_batch)
    sblk = n_prompt // Q_TILE
    qi_s = jnp.transpose(qiz[sblk].reshape(IDX_HEADS, Q_TILE, LANES)[:, :dec_batch], (1, 0, 2))
    q_s = jnp.transpose(qz[sblk].reshape(N_HEADS, Q_TILE, LANES)[:, :dec_batch], (1, 0, 2))
    low_pair = (jnp.arange(N_HEADS) // (N_HEADS // N_KV_HEADS) < 2)[None, :, None]
    qs = jnp.concatenate([jnp.where(low_pair, q_s, 0), jnp.where(low_pair, 0, q_s)], axis=-1).astype(BF16)
    w_s = kiwi[srows, IDX_DIM:IDX_DIM + IDX_HEADS].reshape(dec_batch, IDX_HEADS, 1)
    o_s = _sample_attention(page_table, qi_s, w_s, qs, kiwi[srows].reshape(dec_batch, 1, LANES),
                            k[srows].reshape(dec_batch, 1, kv_w), v[srows].reshape(dec_batch, 1, kv_w),
                            _keys_minor(cache_kidx), _keys_minor(cache_k), _keys_minor(cache_v))
    group = N_HEADS // N_KV_HEADS
    o_s = o_s.reshape(dec_batch, N_KV_HEADS, group, N_KV_HEADS, HEAD_DIM)
    o_s = jnp.transpose(jnp.diagonal(o_s, axis1=1, axis2=3), (0, 3, 1, 2)).reshape(dec_batch, N_HEADS * HEAD_DIM)
    o_sample = jnp.concatenate([o_s, jnp.zeros((ROW_TILE - dec_batch, N_HEADS * HEAD_DIM), F32)]).astype(BF16)

    g_off = w_in.shape[1] - 2 * d_model
    merged = _merge(x_all, a_all, o_prompt, o_sample, w_in[:, g_off:g_off + d_model].astype(BF16),
                    w_in[:, g_off + d_model:].astype(BF16), w_branch_a.astype(BF16), w_branch_b.astype(BF16))
    w_route = jnp.concatenate([w_expert_router, w_group,
                               jnp.zeros((d_model, LANES - N_EXPERTS - N_GROUPS), F32)], axis=1).astype(BF16)
    h, hb, dense = _out_projection(merged, x_all, w_out.astype(BF16), ln1_g.reshape(1, d_model),
                                   ln1_b.reshape(1, d_model), w_route)
    wg, wu, wd = w_gate.astype(BF16), w_up.astype(BF16), w_down.astype(BF16)
    g2, b2 = ln2_g.reshape(1, d_model), ln2_b.reshape(1, d_model)
    y_prompt = _moe(hb, dense, h, wg, wu, wd, g2, b2, tm=MOE_ROW_TILE, first_tile=0,
                    n_tiles=n_prompt // MOE_ROW_TILE)
    y_sample = _moe(hb, dense, h, wg, wu, wd, g2, b2, tm=ROW_TILE, first_tile=n_prompt // ROW_TILE, n_tiles=1)

    u_p = u[:n_prompt].reshape(batch, seq, pool_width)
    pool_sample = jnp.concatenate([state_pool[:, 1:], u[srows][:, None, :]], axis=1)
    return (y_prompt.reshape(batch, seq, d_model), y_sample[:dec_batch].reshape(dec_batch, 1, d_model),
            k[:n_prompt].reshape(batch, seq, N_KV_HEADS, HEAD_DIM), v[:n_prompt].reshape(batch, seq, N_KV_HEADS, HEAD_DIM),
            kiwi[:n_prompt, :IDX_DIM].reshape(batch, seq, IDX_DIM), u_p[:, seq - POOL_STATE:],
            k[srows].reshape(dec_batch, 1, N_KV_HEADS, HEAD_DIM), v[srows].reshape(dec_batch, 1, N_KV_HEADS, HEAD_DIM),
            kiwi[srows, :IDX_DIM].reshape(dec_batch, 1, IDX_DIM), pool_sample)


def kernel(x_prompt, x_sample, cache_k, cache_v, cache_kidx, state_pool, page_table, w_in, w_pool, pool_scale,
           w_branch_a, w_branch_b, w_out, ln1_g, ln1_b, w_group, w_expert_router, w_gate, w_up, w_down,
           ln2_g, ln2_b):
    assert w_in.shape[0] == DEPTH
    outs = _layer(x_prompt, x_sample, cache_k[0], cache_v[0], cache_kidx[0], state_pool[0], page_table, w_in[0],
                  w_pool[0], pool_scale[0], w_branch_a[0], w_branch_b[0], w_out[0], ln1_g[0], ln1_b[0],
                  w_group[0], w_expert_router[0], w_gate[0], w_up[0], w_down[0], ln2_g[0], ln2_b[0])
    y_p, y_s = outs[0], outs[1]
    return (y_p, y_s) + tuple(o[None] for o in outs[2:])
```

```python
import functools
import math

import jax
import jax.numpy as jnp
import numpy as np
from jax import lax
from jax.experimental import pallas as pl
from jax.experimental.pallas import tpu as pltpu

BF16 = jnp.bfloat16
F32 = jnp.float32
I32 = jnp.int32

PAGE_SIZE = 128
POOL_WINDOWS = (2, 4, 8, 16)
POOL_STATE = 15
N_HEADS = 16
N_KV_HEADS = 4
HEAD_DIM = 64
ROT_DIM = HEAD_DIM // 4
ROPE_THETA = 500000.0
IDX_HEADS = 16
IDX_DIM = 64
IDX_W_SCALE = (IDX_HEADS * IDX_DIM) ** -0.5
TOPK_MAX = 256
N_GROUPS = 4
EXPERTS_PER_GROUP = 8
N_EXPERTS = N_GROUPS * EXPERTS_PER_GROUP
LN_EPS = 1e-5
DEPTH = 1
ALPHA = (2 * DEPTH) ** 0.25

LANES = 128
SUBLANES = 8
Q_TILE = 128
KEY_CHUNK = 512
ROW_TILE = 256
MOE_ROW_TILE = 512
VMEM_LIMIT = 56 * 1024 * 1024

INT_MIN = -2 ** 31
NEG_INF_KEY = int(np.int32(np.uint32(0xFF800000) ^ np.uint32(0x7FFFFFFF)))


def _sortable_key(x):
    bits = lax.bitcast_convert_type(x, I32)
    return bits ^ ((bits >> 31) & 0x7FFFFFFF)


def _const_spec(shape):
    nd = len(shape)
    return pl.BlockSpec(shape, lambda *_: (0,) * nd, pipeline_mode=pl.Buffered(1))


def _inproj_body(x_ref, w_ref, tc_ref, ts1_ref, ts2_ref, wpool_ref, pscale_ref, state_ref,
                 utail_ref, a_ref, qt_ref, qit_ref, k_ref, v_ref, kb_ref, vt_ref, kiwi_ref, kib_ref, wt_ref,
                 uext_ref, *, tm, tiles_per_batch, n_prompt_tiles, dec_batch, sample_pos, pool_width, tail):
    i = pl.program_id(0)
    xb = x_ref[...].astype(BF16)
    cos = tc_ref[...]
    nsin = ts1_ref[...]
    psin = ts2_ref[...]
    lane = lax.broadcasted_iota(I32, (tm, LANES), 1)
    lo = lane < HEAD_DIM
    gw = pool_width // len(POOL_WINDOWS)
    attn_w = N_HEADS * HEAD_DIM
    kv_w = N_KV_HEADS * HEAD_DIM
    off_q = pool_width
    off_k = off_q + attn_w
    off_v = off_k + kv_w
    off_qi = off_v + kv_w
    off_kiwi = off_qi + IDX_HEADS * IDX_DIM

    def proj(c0, width):
        return jnp.dot(xb, w_ref[:, c0:c0 + width], preferred_element_type=F32)

    def rope(z, c, s1, s2):
        return z * c + pltpu.roll(z, LANES - ROT_DIM // 2, 1) * s1 + pltpu.roll(z, ROT_DIM // 2, 1) * s2

    def blocks(tile):
        return [(blk, tile[blk * Q_TILE:(blk + 1) * Q_TILE].T) for blk in range(tm // Q_TILE)]

    def store_head_t(dst_ref, h, tile):
        for blk, t in blocks(tile):
            dst_ref[blk, :, h * Q_TILE:(h + 1) * Q_TILE] = t.astype(BF16)

    u = proj(0, pool_width)

    zq = proj(off_q, attn_w)
    for j in range(N_HEADS // 2):
        p = rope(zq[:, j * LANES:(j + 1) * LANES], cos, nsin, psin) * (HEAD_DIM ** -0.5)
        pr = pltpu.roll(p, HEAD_DIM, 1)
        if (j // 2) % 2 == 0:
            store_head_t(qt_ref, 2 * j, jnp.where(lo, p, 0.0))
            store_head_t(qt_ref, 2 * j + 1, jnp.where(lo, pr, 0.0))
        else:
            store_head_t(qt_ref, 2 * j, jnp.where(lo, 0.0, pr))
            store_head_t(qt_ref, 2 * j + 1, jnp.where(lo, 0.0, p))

    zqi = proj(off_qi, IDX_HEADS * IDX_DIM)
    for j in range(IDX_HEADS // 2):
        p = rope(zqi[:, j * LANES:(j + 1) * LANES], cos, nsin, psin)
        pr = pltpu.roll(p, IDX_DIM, 1)
        store_head_t(qit_ref, 2 * j, jnp.where(lo, p, 0.0))
        store_head_t(qit_ref, 2 * j + 1, jnp.where(lo, pr, 0.0))

    zk = proj(off_k, kv_w)
    for j in range(kv_w // LANES):
        kr = rope(zk[:, j * LANES:(j + 1) * LANES], cos, nsin, psin)
        k_ref[:, j * LANES:(j + 1) * LANES] = kr
        kb_ref[:, j * LANES:(j + 1) * LANES] = kr.astype(BF16)

    zv = proj(off_v, kv_w)
    v_ref[...] = zv
    for j in range(kv_w // LANES):
        for blk, t in blocks(zv[:, j * LANES:(j + 1) * LANES]):
            vt_ref[j * LANES:(j + 1) * LANES, blk * Q_TILE:(blk + 1) * Q_TILE] = t.astype(BF16)

    zkw = proj(off_kiwi, LANES)
    kw = rope(zkw, jnp.where(lo, cos, IDX_W_SCALE), jnp.where(lo, nsin, 0.0), jnp.where(lo, psin, 0.0))
    kiwi_ref[...] = kw
    kib_ref[...] = kw.astype(BF16)
    for blk, t in blocks(kw):
        wt_ref[:, blk * Q_TILE:(blk + 1) * Q_TILE] = t[IDX_DIM:IDX_DIM + IDX_HEADS, :]

    def mix(d, g):
        lanes = slice(g * gw, (g + 1) * gw)
        z = jnp.dot(d.astype(BF16), wpool_ref[g], preferred_element_type=F32)
        return z * pscale_ref[:, lanes]

    @pl.when(i < n_prompt_tiles)
    def _():
        first = (i % tiles_per_batch) == 0

        @pl.when(first)
        def _():
            uext_ref[0:16, :] = jnp.zeros((16, pool_width), F32)

        @pl.when(jnp.logical_not(first))
        def _():
            uext_ref[0:16, :] = uext_ref[tm:tm + 16, :]

        uext_ref[16:16 + tm, :] = u
        utail_ref[0] = u[tm - tail:tm, :]
        pos = (i % tiles_per_batch) * tm + lax.broadcasted_iota(I32, (tm, 1), 0)
        for g, w in enumerate(POOL_WINDOWS):
            lanes = slice(g * gw, (g + 1) * gw)
            acc = u[:, lanes]
            for jj in range(1, w):
                acc = acc + uext_ref[16 - jj:16 - jj + tm, lanes]
            cnt = jnp.minimum(w, pos + 1).astype(F32)
            d = acc / cnt - u[:, lanes]
            a_ref[:, lanes] = mix(d, g).astype(BF16)

    @pl.when(i == n_prompt_tiles)
    def _():
        a_ref[...] = jnp.zeros((tm, pool_width), BF16)
        utail_ref[0] = u[0:tail, :]
        us = u[0:dec_batch, :]
        for g, w in enumerate(POOL_WINDOWS):
            lanes = slice(g * gw, (g + 1) * gw)
            acc = us[:, lanes]
            for jj in range(1, w):
                acc = acc + state_ref[POOL_STATE - jj, :, lanes]
            d = acc / float(min(w, sample_pos + 1)) - us[:, lanes]
            a_ref[0:dec_batch, lanes] = mix(d, g).astype(BF16)


def _in_projection(x_all, w_a, tab_c, tab_s1, tab_s2, wpool, pscale, state_t, *, seq, n_prompt, sample_pos):
    n_rows, d_model = x_all.shape
    tm = ROW_TILE
    pool_width = wpool.shape[0] * wpool.shape[1]
    dec_batch = state_t.shape[1]
    n_tiles = n_rows // tm
    nblk = n_rows // Q_TILE
    kv_w = N_KV_HEADS * HEAD_DIM
    tiles_per_batch = seq // tm
    tail = max(16, dec_batch)
    row = lambda width: pl.BlockSpec((tm, width), lambda i: (i, 0))
    col = lambda height: pl.BlockSpec((height, tm), lambda i: (0, i))
    hm = pl.BlockSpec((tm // Q_TILE, LANES, N_HEADS * Q_TILE), lambda i: (i, 0, 0))
    body = functools.partial(_inproj_body, tm=tm, tiles_per_batch=tiles_per_batch, n_prompt_tiles=n_prompt // tm,
                             dec_batch=dec_batch, sample_pos=sample_pos, pool_width=pool_width, tail=tail)
    return pl.pallas_call(
        body,
        grid=(n_tiles,),
        in_specs=[row(d_model), _const_spec(w_a.shape), row(LANES), row(LANES), row(LANES),
                  _const_spec(wpool.shape), _const_spec(pscale.shape), _const_spec(state_t.shape)],
        out_specs=[pl.BlockSpec((1, tail, pool_width), lambda i: (i // tiles_per_batch, 0, 0)),
                   row(pool_width), hm, hm, row(kv_w), row(kv_w), row(kv_w), col(kv_w),
                   row(LANES), row(LANES), col(IDX_HEADS)],
        out_shape=[
            jax.ShapeDtypeStruct((n_tiles // tiles_per_batch + 1, tail, pool_width), F32),
            jax.ShapeDtypeStruct((n_rows, pool_width), BF16),
            jax.ShapeDtypeStruct((nblk, LANES, N_HEADS * Q_TILE), BF16),
            jax.ShapeDtypeStruct((nblk, LANES, IDX_HEADS * Q_TILE), BF16),
            jax.ShapeDtypeStruct((n_rows, kv_w), F32),
            jax.ShapeDtypeStruct((n_rows, kv_w), F32),
            jax.ShapeDtypeStruct((n_rows, kv_w), BF16),
            jax.ShapeDtypeStruct((kv_w, n_rows), BF16),
            jax.ShapeDtypeStruct((n_rows, LANES), F32),
            jax.ShapeDtypeStruct((n_rows, LANES), BF16),
            jax.ShapeDtypeStruct((IDX_HEADS, n_rows), F32),
        ],
        scratch_shapes=[pltpu.VMEM((tm + 16, pool_width), F32)],
        compiler_params=pltpu.CompilerParams(dimension_semantics=("arbitrary",), vmem_limit_bytes=VMEM_LIMIT),
        name="in_projection",
    )(x_all, w_a, tab_c, tab_s1, tab_s2, wpool, pscale, state_t)


def _rope_tables(pos):
    n = pos.shape[0]
    half = ROT_DIM // 2
    inv = jnp.exp(-jnp.arange(half, dtype=F32) * (math.log(ROPE_THETA) / half))
    ang = pos.astype(F32)[:, None] * inv[None, :]
    cos, sin = jnp.cos(ang), jnp.sin(ang)
    zeros = jnp.zeros((n, half), F32)
    rest = HEAD_DIM - ROT_DIM
    c = jnp.concatenate([cos, cos, jnp.ones((n, rest), F32)], axis=1)
    s1 = jnp.concatenate([-sin, zeros, jnp.zeros((n, rest), F32)], axis=1)
    s2 = jnp.concatenate([zeros, sin, jnp.zeros((n, rest), F32)], axis=1)
    return tuple(jnp.concatenate([t, t], axis=1) for t in (c, s1, s2))


def _stage1(x_all, batch, seq, state_pool, w_in, w_pool, pool_scale, past_len):
    d_model = x_all.shape[1]
    dec_batch = state_pool.shape[0]
    n_prompt = batch * seq
    pad = x_all.shape[0] - n_prompt - dec_batch
    pos = jnp.concatenate([jnp.tile(jnp.arange(seq, dtype=I32), batch), jnp.full((dec_batch,), past_len, I32),
                           jnp.zeros((pad,), I32)])
    tab_c, tab_s1, tab_s2 = _rope_tables(pos)
    pool_width = w_pool.shape[0] * w_pool.shape[1]
    a_width = pool_width + N_HEADS * HEAD_DIM + 2 * N_KV_HEADS * HEAD_DIM + IDX_HEADS * IDX_DIM + IDX_DIM + IDX_HEADS
    w_a = jnp.concatenate([w_in[:, :a_width], jnp.zeros((d_model, -a_width % LANES), F32)], axis=1).astype(BF16)
    state_t = jnp.transpose(state_pool, (1, 0, 2))
    return _in_projection(x_all, w_a, tab_c, tab_s1, tab_s2, w_pool.astype(BF16),
                          pool_scale.reshape(1, pool_width), state_t,
                          seq=seq, n_prompt=n_prompt, sample_pos=past_len)


def _topk_threshold(count_ge, init, topk):
    def step(it, t):
        cand = t + jnp.left_shift(jnp.int32(1), 31 - it)
        return jnp.where(count_ge(cand) >= topk, cand, t)
    return lax.fori_loop(0, 32, step, init)


def _fold_slabs(x, op):
    assert x.shape[0] & (x.shape[0] - 1) == 0
    while x.shape[0] > 1:
        half = x.shape[0] // 2
        x = op(x[:half], x[half:])
    return x[0]


def _prompt_attn_body(qit_ref, wt_ref, qt_ref, kib_ref, kb_ref, vt_ref, o_ref,
                      sc_ref, thr_ref, m_ref, l_ref, acc_ref, *, topk):
    i = pl.program_id(1)
    tq, kc = Q_TILE, KEY_CHUNK
    n_keys = (i + 1) * tq
    n_chunks = (n_keys + kc - 1) // kc
    qpos = i * tq + lax.broadcasted_iota(I32, (1, tq), 1)
    group = N_HEADS // N_KV_HEADS
    gcols = group * tq

    def score_chunk(c, carry):
        k0 = pl.multiple_of(c * kc, kc)
        kchunk = kib_ref[pl.ds(k0, kc), :]
        acc = jnp.zeros((kc, tq), F32)
        for hg in range(IDX_HEADS // group):
            d = jnp.dot(kchunk, qit_ref[0, :, hg * gcols:(hg + 1) * gcols], preferred_element_type=F32)
            for r in range(group):
                h = hg * group + r
                acc = acc + jnp.maximum(d[:, r * tq:(r + 1) * tq], 0.0) * wt_ref[h:h + 1, :]
        kpos = k0 + lax.broadcasted_iota(I32, (kc, 1), 0)
        sc_ref[pl.ds(k0, kc), :] = _sortable_key(jnp.where(kpos <= qpos, acc, -jnp.inf))
        return carry

    lax.fori_loop(0, n_chunks, score_chunk, 0)

    thr_ref[...] = jnp.full((1, tq), NEG_INF_KEY + 1, I32)

    @pl.when(n_keys > topk)
    def _():
        def count_ge(cand):
            candb = jnp.broadcast_to(cand, (SUBLANES, tq))

            def blk(c, acc):
                k0 = pl.multiple_of(c * kc, kc)
                hit = jnp.where(sc_ref[pl.ds(k0, kc), :].reshape(kc // SUBLANES, SUBLANES, tq) >= candb, 1.0, 0.0)
                return acc + _fold_slabs(hit, jnp.add)
            acc = lax.fori_loop(0, n_chunks, blk, jnp.zeros((SUBLANES, tq), F32))
            return jnp.sum(acc, axis=0, keepdims=True)

        t = _topk_threshold(count_ge, jnp.full((1, tq), INT_MIN, I32), float(topk))
        thr_ref[...] = jnp.where(qpos + 1 > topk, t, NEG_INF_KEY + 1)

    m_ref[...] = jnp.full(m_ref.shape, -jnp.inf, F32)
    l_ref[...] = jnp.zeros(l_ref.shape, F32)
    acc_ref[...] = jnp.zeros(acc_ref.shape, F32)

    def attn_chunk(c, carry):
        k0 = pl.multiple_of(c * kc, kc)
        sel = sc_ref[pl.ds(k0, kc), :] >= thr_ref[...]
        for g in range(N_KV_HEADS):
            pair = slice((g // 2) * LANES, (g // 2 + 1) * LANES)
            cols = slice(g * gcols, (g + 1) * gcols)
            s = jnp.dot(kb_ref[pl.ds(k0, kc), pair], qt_ref[0, :, cols], preferred_element_type=F32)
            s = jnp.concatenate([jnp.where(sel, s[:, r * tq:(r + 1) * tq], -jnp.inf) for r in range(group)], axis=1)
            m_old = m_ref[:, cols]
            m_new = jnp.maximum(m_old, jnp.max(s, axis=0, keepdims=True))
            m_safe = jnp.where(m_new == -jnp.inf, 0.0, m_new)
            alpha = jnp.exp(m_old - m_safe)
            p = jnp.exp(s - m_safe)
            l_ref[:, cols] = alpha * l_ref[:, cols] + jnp.sum(p, axis=0, keepdims=True)
            pv = jnp.dot(vt_ref[pair, pl.ds(k0, kc)], p.astype(BF16), preferred_element_type=F32)
            acc_ref[g] = alpha * acc_ref[g] + pv
            m_ref[:, cols] = m_new
        return carry

    lax.fori_loop(0, n_chunks, attn_chunk, 0)

    for j in range(N_HEADS // 2):
        halves = []
        for h in (2 * j, 2 * j + 1):
            g, r = h // group, h % group
            rows = slice((g % 2) * HEAD_DIM, (g % 2 + 1) * HEAD_DIM)
            cols = slice(r * tq, (r + 1) * tq)
            halves.append(acc_ref[g, rows, cols] / l_ref[:, g * gcols + r * tq:g * gcols + (r + 1) * tq])
        o_ref[:, j * LANES:(j + 1) * LANES] = jnp.concatenate(halves, axis=0).T.astype(BF16)


def _prompt_attention(qit, wt, qt, kib, kb, vt, *, batch, seq):
    nq = seq // Q_TILE
    topk = min(TOPK_MAX, seq // 4)
    kv_w = N_KV_HEADS * HEAD_DIM
    per_q = lambda b, i: (b * nq + i, 0, 0)
    body = functools.partial(_prompt_attn_body, topk=topk)
    return pl.pallas_call(
        body,
        grid=(batch, nq),
        in_specs=[
            pl.BlockSpec((1, LANES, IDX_HEADS * Q_TILE), per_q),
            pl.BlockSpec((IDX_HEADS, Q_TILE), lambda b, i: (0, b * nq + i)),
            pl.BlockSpec((1, LANES, N_HEADS * Q_TILE), per_q),
            pl.BlockSpec((seq, LANES), lambda b, i: (b, 0)),
            pl.BlockSpec((seq, kv_w), lambda b, i: (b, 0)),
            pl.BlockSpec((kv_w, seq), lambda b, i: (0, b)),
        ],
        out_specs=pl.BlockSpec((Q_TILE, N_HEADS * HEAD_DIM), lambda b, i: (b * nq + i, 0)),
        out_shape=jax.ShapeDtypeStruct((batch * seq, N_HEADS * HEAD_DIM), BF16),
        scratch_shapes=[
            pltpu.VMEM((seq, Q_TILE), I32),
            pltpu.VMEM((1, Q_TILE), I32),
            pltpu.VMEM((1, N_HEADS * Q_TILE), F32),
            pltpu.VMEM((1, N_HEADS * Q_TILE), F32),
            pltpu.VMEM((N_KV_HEADS, LANES, (N_HEADS // N_KV_HEADS) * Q_TILE), F32),
        ],
        compiler_params=pltpu.CompilerParams(dimension_semantics=("arbitrary", "arbitrary"),
                                             vmem_limit_bytes=VMEM_LIMIT),
        name="prompt_attention",
    )(qit, wt, qt, kib, kb, vt)


SAMPLE_PAGES_PER_CHUNK = 16


def _sample_attn_body(pt_ref, qi_ref, w_ref, qs_ref, kicur_ref, kcur_ref, vcur_ref, ckidx_hbm, ck_hbm, cv_hbm,
                      o_ref, kidx_buf, k_buf, v_buf, sc_ref, sem_idx, sem_k, sem_v, *, n_pages, topk):
    b = pl.program_id(0)
    ppc = SAMPLE_PAGES_PER_CHUNK
    n_chunks = n_pages // ppc
    ck = ppc * PAGE_SIZE
    past = n_pages * PAGE_SIZE
    nt = (((1,), (1,)), ((), ()))

    def idx_copy(p):
        return pltpu.make_async_copy(ckidx_hbm.at[pt_ref[b, p]],
                                     kidx_buf.at[:, pl.ds(p * PAGE_SIZE, PAGE_SIZE)], sem_idx.at[0])

    def kv_copies(c, p):
        slot = c % 2
        page = pt_ref[b, c * ppc + p]
        window = pl.ds(p * PAGE_SIZE, PAGE_SIZE)
        return (pltpu.make_async_copy(ck_hbm.at[page], k_buf.at[slot, :, window], sem_k.at[slot]),
                pltpu.make_async_copy(cv_hbm.at[page], v_buf.at[slot, :, window], sem_v.at[slot]))

    def start_kv(c):
        for p in range(ppc):
            for cp in kv_copies(c, p):
                cp.start()

    def wait_kv(c):
        for p in range(ppc):
            for cp in kv_copies(c, p):
                cp.wait()

    for p in range(n_pages):
        idx_copy(p).start()
    start_kv(0)

    for p in range(n_pages):
        idx_copy(p).wait()

    qi = qi_ref[0][:, :IDX_DIM]
    w = w_ref[0]

    def head_mix(d):
        return jnp.sum(jnp.maximum(d, 0.0) * w, axis=0, keepdims=True)

    def bf16_products(a, row):
        return jnp.sum(a.astype(F32) * row.astype(BF16).astype(F32), axis=1, keepdims=True)

    for c in range(n_chunks):
        d = jnp.dot(qi, kidx_buf[:, c * ck:(c + 1) * ck].astype(BF16), preferred_element_type=F32)
        sc_ref[:, c * ck:(c + 1) * ck] = _sortable_key(head_mix(d))
    cur_key = _sortable_key(head_mix(bf16_products(qi, kicur_ref[0][:, :IDX_DIM])))
    tail_lane = lax.broadcasted_iota(I32, (1, PAGE_SIZE), 1)
    sc_ref[:, past:past + PAGE_SIZE] = jnp.where(tail_lane == 0, cur_key, INT_MIN)

    if past + 1 > topk:
        def count_ge(cand):
            return jnp.sum(jnp.where(sc_ref[...] >= cand, 1.0, 0.0), axis=1, keepdims=True)
        thr = _topk_threshold(count_ge, jnp.full((1, 1), INT_MIN, I32), float(topk))
    else:
        thr = jnp.full((1, 1), NEG_INF_KEY + 1, I32)

    qs = qs_ref[0]
    m = jnp.full((N_HEADS, 1), -jnp.inf, F32)
    l = jnp.zeros((N_HEADS, 1), F32)
    acc = jnp.zeros((N_HEADS, N_KV_HEADS * HEAD_DIM), F32)

    def update(m, l, acc, s, sel, pv):
        s = jnp.where(sel, s, -jnp.inf)
        m_new = jnp.maximum(m, jnp.max(s, axis=1, keepdims=True))
        m_safe = jnp.where(m_new == -jnp.inf, 0.0, m_new)
        alpha = jnp.exp(m - m_safe)
        p = jnp.exp(s - m_safe)
        l = alpha * l + jnp.sum(p, axis=1, keepdims=True)
        acc = alpha * acc + pv(p.astype(BF16))
        return m_new, l, acc

    for c in range(n_chunks):
        if c + 1 < n_chunks:
            start_kv(c + 1)
        wait_kv(c)
        slot = c % 2
        s = jnp.dot(qs, k_buf[slot].astype(BF16), preferred_element_type=F32)
        vt = v_buf[slot].astype(BF16)
        m, l, acc = update(m, l, acc, s, sc_ref[:, c * ck:(c + 1) * ck] >= thr,
                           lambda p: lax.dot_general(p, vt, nt, preferred_element_type=F32))
    v_cur = vcur_ref[0].astype(BF16).astype(F32)
    m, l, acc = update(m, l, acc, bf16_products(qs, kcur_ref[0]), cur_key >= thr,
                       lambda p: p.astype(F32) * v_cur)
    o_ref[0] = acc / l


def _keys_minor(cache):
    pages, page_size = cache.shape[:2]
    return jnp.moveaxis(cache, 1, -1).reshape(pages, -1, page_size)


def _sample_attention(page_table, qi_s, w_s, qs, kicur, kcur, vcur, cache_kidx, cache_k, cache_v):
    dec_batch, n_pages = page_table.shape
    past = n_pages * PAGE_SIZE
    topk = min(TOPK_MAX, (past + 1) // 4)
    kv_w = N_KV_HEADS * HEAD_DIM
    ppc = SAMPLE_PAGES_PER_CHUNK
    per_b = lambda b, pt: (b, 0, 0)
    any_spec = pl.BlockSpec(memory_space=pl.ANY)
    grid_spec = pltpu.PrefetchScalarGridSpec(
        num_scalar_prefetch=1,
        grid=(dec_batch,),
        in_specs=[
            pl.BlockSpec((1, IDX_HEADS, LANES), per_b),
            pl.BlockSpec((1, IDX_HEADS, 1), per_b),
            pl.BlockSpec((1, N_HEADS, kv_w), per_b),
            pl.BlockSpec((1, 1, LANES), per_b),
            pl.BlockSpec((1, 1, kv_w), per_b),
            pl.BlockSpec((1, 1, kv_w), per_b),
            any_spec, any_spec, any_spec,
        ],
        out_specs=pl.BlockSpec((1, N_HEADS, kv_w), per_b),
        scratch_shapes=[
            pltpu.VMEM((IDX_DIM, past), F32),
            pltpu.VMEM((2, kv_w, ppc * PAGE_SIZE), F32),
            pltpu.VMEM((2, kv_w, ppc * PAGE_SIZE), F32),
            pltpu.VMEM((1, past + PAGE_SIZE), I32),
            pltpu.SemaphoreType.DMA((1,)),
            pltpu.SemaphoreType.DMA((2,)),
            pltpu.SemaphoreType.DMA((2,)),
        ],
    )
    body = functools.partial(_sample_attn_body, n_pages=n_pages, topk=topk)
    return pl.pallas_call(
        body,
        grid_spec=grid_spec,
        out_shape=jax.ShapeDtypeStruct((dec_batch, N_HEADS, kv_w), F32),
        compiler_params=pltpu.CompilerParams(dimension_semantics=("arbitrary",), vmem_limit_bytes=VMEM_LIMIT),
        name="sample_attention",
    )(page_table, qi_s, w_s, qs, kicur, kcur, vcur, cache_kidx, cache_k, cache_v)


def _mm(a, b, precise):
    if precise:
        return jnp.dot(a.astype(F32), b, precision=lax.Precision.HIGHEST, preferred_element_type=F32)
    return jnp.dot(a.astype(BF16), b, preferred_element_type=F32)


def _operand_dtype(precise):
    return F32 if precise else BF16


MERGE_COL_CHUNK = 512


def _merge_body(x_ref, a_ref, o_ref, wga_ref, wgb_ref, wba_ref, wbb_ref, out_ref, *, precise):
    x, a, o = x_ref[...], a_ref[...], o_ref[...]
    cc = MERGE_COL_CHUNK
    for c in range(out_ref.shape[1] // cc):
        cols = slice(c * cc, (c + 1) * cc)
        ga = _mm(x, wga_ref[:, cols], precise)
        gb = _mm(x, wgb_ref[:, cols], precise)
        ya = _mm(a, wba_ref[:, cols], precise)
        yo = _mm(o, wbb_ref[:, cols], precise)
        out_ref[:, cols] = (jax.nn.sigmoid(ga) * ya + jax.nn.sigmoid(gb) * yo).astype(out_ref.dtype)


def _merge(x, a, o, wga, wgb, wba, wbb, *, tm, n_tiles, x_tile0, a_tile0, col_block, precise):
    d_model = x.shape[1]
    resident = col_block == d_model

    def wspec(w):
        mode = dict(pipeline_mode=pl.Buffered(1)) if resident else {}
        return pl.BlockSpec((w.shape[0], col_block), lambda i, j: (0, j), **mode)

    body = functools.partial(_merge_body, precise=precise)
    return pl.pallas_call(
        body,
        grid=(n_tiles, d_model // col_block),
        in_specs=[pl.BlockSpec((tm, d_model), lambda i, j: (i + x_tile0, 0)),
                  pl.BlockSpec((tm, a.shape[1]), lambda i, j: (i + a_tile0, 0)),
                  pl.BlockSpec((tm, o.shape[1]), lambda i, j: (i, 0)),
                  wspec(wga), wspec(wgb), wspec(wba), wspec(wbb)],
        out_specs=pl.BlockSpec((tm, col_block), lambda i, j: (i, j)),
        out_shape=jax.ShapeDtypeStruct((n_tiles * tm, d_model), _operand_dtype(precise)),
        compiler_params=pltpu.CompilerParams(dimension_semantics=("arbitrary", "arbitrary"),
                                             vmem_limit_bytes=VMEM_LIMIT),
        name="branch_merge",
    )(x, a, o, wga, wgb, wba, wbb)


def _layer_norm(x, g, b):
    mu = jnp.mean(x, axis=-1, keepdims=True)
    var = jnp.mean(jnp.square(x - mu), axis=-1, keepdims=True)
    return (x - mu) * lax.rsqrt(var + LN_EPS) * g + b


def _route(logits):
    lane = lax.broadcasted_iota(I32, logits.shape, 1)
    lane_f = lane.astype(F32)
    big = float(LANES)

    def masked_softmax(mask):
        x = jnp.where(mask, logits, -jnp.inf)
        e = jnp.exp(x - jnp.max(x, axis=1, keepdims=True))
        return jnp.where(mask, e / jnp.sum(e, axis=1, keepdims=True), -1.0)

    def top1(p):
        best = jnp.max(p, axis=1, keepdims=True)
        idx = jnp.min(jnp.where(p == best, lane_f, big), axis=1, keepdims=True)
        return best, idx

    gp = masked_softmax((lane >= N_EXPERTS) & (lane < N_EXPERTS + N_GROUPS))
    g_p, g_lane = top1(gp)
    g_idx = g_lane.astype(I32) - N_EXPERTS
    ep = masked_softmax((lane < N_EXPERTS) & ((lane >> int(math.log2(EXPERTS_PER_GROUP))) == g_idx))
    p1, i1 = top1(ep)
    p2, i2 = top1(jnp.where(lane_f == i1, -1.0, ep))
    denom = p1 + p2
    return jnp.where(lane_f == i1, g_p * p1 / denom, 0.0) + jnp.where(lane_f == i2, g_p * p2 / denom, 0.0)


def _outproj_body(m_ref, x_ref, wout_ref, g_ref, b_ref, wr_ref, h_ref, hb_ref, dense_ref, *, precise):
    mo = _mm(m_ref[...], wout_ref[...], precise)
    h = _layer_norm(ALPHA * x_ref[...] + mo, g_ref[...], b_ref[...])
    h_ref[...] = h
    hb_ref[...] = h.astype(BF16)
    dense_ref[...] = _route(_mm(h, wr_ref[...], precise))


def _out_projection(merged, x, wout, ln_g, ln_b, w_route, *, tm, n_tiles, x_tile0, precise):
    d_model = x.shape[1]
    n_rows = n_tiles * tm
    body = functools.partial(_outproj_body, precise=precise)
    out_row = lambda width: pl.BlockSpec((tm, width), lambda i: (i, 0))
    return pl.pallas_call(
        body,
        grid=(n_tiles,),
        in_specs=[out_row(d_model), pl.BlockSpec((tm, d_model), lambda i: (i + x_tile0, 0)), _const_spec(wout.shape),
                  _const_spec(ln_g.shape), _const_spec(ln_b.shape), _const_spec(w_route.shape)],
        out_specs=[out_row(d_model), out_row(d_model), out_row(LANES)],
        out_shape=[jax.ShapeDtypeStruct((n_rows, d_model), F32),
                   jax.ShapeDtypeStruct((n_rows, d_model), BF16),
                   jax.ShapeDtypeStruct((n_rows, LANES), F32)],
        compiler_params=pltpu.CompilerParams(dimension_semantics=("arbitrary",), vmem_limit_bytes=VMEM_LIMIT),
        name="out_projection",
    )(merged, x, wout, ln_g, ln_b, w_route)


def _moe_body(hb_ref, dense_ref, h_ref, wg_ref, wu_ref, wd_ref, g_ref, b_ref, y_ref, acc_ref):
    e = pl.program_id(1)

    @pl.when(e == 0)
    def _():
        acc_ref[...] = jnp.zeros(acc_ref.shape, F32)

    hb = hb_ref[...]
    gate = jnp.dot(hb, wg_ref[0], preferred_element_type=F32)
    up = jnp.dot(hb, wu_ref[0], preferred_element_type=F32)
    lane = lax.broadcasted_iota(I32, dense_ref.shape, 1)
    dcol = jnp.sum(jnp.where(lane == e, dense_ref[...], 0.0), axis=1, keepdims=True)
    hh = (gate * jax.nn.sigmoid(gate)) * up * dcol
    acc_ref[...] += jnp.dot(hh.astype(BF16), wd_ref[0], preferred_element_type=F32)

    @pl.when(e == pl.num_programs(1) - 1)
    def _():
        y_ref[...] = _layer_norm(ALPHA * h_ref[...] + acc_ref[...], g_ref[...], b_ref[...])


def _moe(hb, dense, h, wg, wu, wd, ln_g, ln_b, *, tm):
    n_rows, d_model = h.shape
    n_experts, _, d_expert = wg.shape
    row = lambda width: pl.BlockSpec((tm, width), lambda i, e: (i, 0))
    return pl.pallas_call(
        _moe_body,
        grid=(n_rows // tm, n_experts),
        in_specs=[row(d_model), row(LANES), row(d_model),
                  pl.BlockSpec((1, d_model, d_expert), lambda i, e: (e, 0, 0)),
                  pl.BlockSpec((1, d_model, d_expert), lambda i, e: (e, 0, 0)),
                  pl.BlockSpec((1, d_expert, d_model), lambda i, e: (e, 0, 0)),
                  _const_spec(ln_g.shape), _const_spec(ln_b.shape)],
        out_specs=row(d_model),
        out_shape=jax.ShapeDtypeStruct((n_rows, d_model), F32),
        scratch_shapes=[pltpu.VMEM((tm, d_model), F32)],
        compiler_params=pltpu.CompilerParams(dimension_semantics=("arbitrary", "arbitrary"),
                                             vmem_limit_bytes=VMEM_LIMIT),
        name="expert_mlp",
    )(hb, dense, h, wg, wu, wd, ln_g, ln_b)


def _layer(x_prompt, x_sample, cache_k, cache_v, cache_kidx, state_pool, page_table, w_in, w_pool, pool_scale,
           w_branch_a, w_branch_b, w_out, ln1_g, ln1_b, w_group, w_expert_router, w_gate, w_up, w_down,
           ln2_g, ln2_b):
    batch, seq, d_model = x_prompt.shape
    dec_batch = x_sample.shape[0]
    n_pages = page_table.shape[1]
    past = n_pages * PAGE_SIZE
    n_prompt = batch * seq
    kv_w = N_KV_HEADS * HEAD_DIM
    attn_w = N_HEADS * HEAD_DIM
    group = N_HEADS // N_KV_HEADS
    assert x_sample.shape[1] == 1 and seq % KEY_CHUNK == 0 and seq % MOE_ROW_TILE == 0
    assert dec_batch <= Q_TILE and dec_batch % 16 == 0 and ROW_TILE % dec_batch == 0
    assert n_pages % SAMPLE_PAGES_PER_CHUNK == 0

    x_p = x_prompt.reshape(n_prompt, d_model)
    x_s = x_sample.reshape(dec_batch, d_model)
    x_all = jnp.concatenate([x_p, x_s, jnp.zeros((ROW_TILE - dec_batch, d_model), F32)], axis=0)
    utail, a_all, qt, qit, k, v, kb, vt, kiwi, kib, wt = _stage1(x_all, batch, seq, state_pool, w_in, w_pool,
                                                                 pool_scale, past)
    o_prompt = _prompt_attention(qit, wt, qt, kib, kb, vt, batch=batch, seq=seq)

    srows = slice(n_prompt, n_prompt + dec_batch)
    sblk = n_prompt // Q_TILE
    per_seq = lambda t: jnp.transpose(t[sblk].reshape(LANES, N_HEADS, Q_TILE)[:, :, :dec_batch], (2, 1, 0))
    qi_s, q_s = per_seq(qit), per_seq(qt)
    low_pair = (jnp.arange(N_HEADS) // group < 2)[None, :, None]
    qs = jnp.concatenate([jnp.where(low_pair, q_s, 0), jnp.where(low_pair, 0, q_s)], axis=-1).astype(BF16)
    w_s = kiwi[srows, IDX_DIM:IDX_DIM + IDX_HEADS].reshape(dec_batch, IDX_HEADS, 1)
    o_s = _sample_attention(page_table, qi_s, w_s, qs, kiwi[srows].reshape(dec_batch, 1, LANES),
                            k[srows].reshape(dec_batch, 1, kv_w), v[srows].reshape(dec_batch, 1, kv_w),
                            _keys_minor(cache_kidx), _keys_minor(cache_k), _keys_minor(cache_v))
    o_s = o_s.reshape(dec_batch, N_KV_HEADS, group, N_KV_HEADS, HEAD_DIM)
    o_s = jnp.transpose(jnp.diagonal(o_s, axis1=1, axis2=3), (0, 3, 1, 2)).reshape(dec_batch, attn_w)
    o_sample = o_s.astype(BF16)

    g_off = w_in.shape[1] - 2 * d_model
    wga, wgb = w_in[:, g_off:g_off + d_model], w_in[:, g_off + d_model:]
    w_route = jnp.concatenate([w_expert_router, w_group,
                               jnp.zeros((d_model, LANES - N_EXPERTS - N_GROUPS), F32)], axis=1)
    g1, b1 = ln1_g.reshape(1, d_model), ln1_b.reshape(1, d_model)
    g2, b2 = ln2_g.reshape(1, d_model), ln2_b.reshape(1, d_model)
    wg, wu, wd = w_gate.astype(BF16), w_up.astype(BF16), w_down.astype(BF16)

    merged_p = _merge(x_p, a_all, o_prompt, wga.astype(BF16), wgb.astype(BF16), w_branch_a.astype(BF16),
                      w_branch_b.astype(BF16), tm=ROW_TILE, n_tiles=n_prompt // ROW_TILE, x_tile0=0, a_tile0=0,
                      col_block=d_model, precise=False)
    h_p, hb_p, dense_p = _out_projection(merged_p, x_p, w_out.astype(BF16), g1, b1, w_route.astype(BF16),
                                         tm=ROW_TILE, n_tiles=n_prompt // ROW_TILE, x_tile0=0, precise=False)
    y_prompt = _moe(hb_p, dense_p, h_p, wg, wu, wd, g2, b2, tm=MOE_ROW_TILE)

    merged_s = _merge(x_s, a_all, o_sample, wga, wgb, w_branch_a, w_branch_b, tm=dec_batch, n_tiles=1, x_tile0=0,
                      a_tile0=n_prompt // dec_batch, col_block=MERGE_COL_CHUNK, precise=True)
    h_s, hb_s, dense_s = _out_projection(merged_s, x_s, w_out, g1, b1, w_route, tm=dec_batch, n_tiles=1,
                                         x_tile0=0, precise=True)
    y_sample = _moe(hb_s, dense_s, h_s, wg, wu, wd, g2, b2, tm=dec_batch)

    pool_sample = jnp.concatenate([state_pool[:, 1:], utail[batch, :dec_batch][:, None, :]], axis=1)
    tail = utail.shape[1]
    return (y_prompt.reshape(batch, seq, d_model), y_sample.reshape(dec_batch, 1, d_model),
            k[:n_prompt].reshape(batch, seq, N_KV_HEADS, HEAD_DIM), v[:n_prompt].reshape(batch, seq, N_KV_HEADS, HEAD_DIM),
            kiwi[:n_prompt, :IDX_DIM].reshape(batch, seq, IDX_DIM), utail[:batch, tail - POOL_STATE:],
            k[srows].reshape(dec_batch, 1, N_KV_HEADS, HEAD_DIM), v[srows].reshape(dec_batch, 1, N_KV_HEADS, HEAD_DIM),
            kiwi[srows, :IDX_DIM].reshape(dec_batch, 1, IDX_DIM), pool_sample)


def kernel(x_prompt, x_sample, cache_k, cache_v, cache_kidx, state_pool, page_table, w_in, w_pool, pool_scale,
           w_branch_a, w_branch_b, w_out, ln1_g, ln1_b, w_group, w_expert_router, w_gate, w_up, w_down,
           ln2_g, ln2_b):
    assert w_in.shape[0] == DEPTH
    outs = _layer(x_prompt, x_sample, cache_k[0], cache_v[0], cache_kidx[0], state_pool[0], page_table, w_in[0],
                  w_pool[0], pool_scale[0], w_branch_a[0], w_branch_b[0], w_out[0], ln1_g[0], ln1_b[0],
                  w_group[0], w_expert_router[0], w_gate[0], w_up[0], w_down[0], ln2_g[0], ln2_b[0])
    y_p, y_s = outs[0], outs[1]
    return (y_p, y_s) + tuple(o[None] for o in outs[2:])
```

```python
import functools
import math

import jax
import jax.numpy as jnp
import numpy as np
from jax import lax
from jax.experimental import pallas as pl
from jax.experimental.pallas import tpu as pltpu

BF16 = jnp.bfloat16
F32 = jnp.float32
I32 = jnp.int32

PAGE_SIZE = 128
POOL_WINDOWS = (2, 4, 8, 16)
POOL_STATE = 15
N_HEADS = 16
N_KV_HEADS = 4
HEAD_DIM = 64
ROT_DIM = HEAD_DIM // 4
ROPE_THETA = 500000.0
IDX_HEADS = 16
IDX_DIM = 64
IDX_W_SCALE = (IDX_HEADS * IDX_DIM) ** -0.5
TOPK_MAX = 256
N_GROUPS = 4
EXPERTS_PER_GROUP = 8
N_EXPERTS = N_GROUPS * EXPERTS_PER_GROUP
LN_EPS = 1e-5
DEPTH = 1
ALPHA = (2 * DEPTH) ** 0.25

LANES = 128
SUBLANES = 8
Q_TILE = 128
KEY_CHUNK = 512
ROW_TILE = 256
MOE_ROW_TILE = 512
VMEM_LIMIT = 56 * 1024 * 1024

INT_MIN = -2 ** 31
NEG_INF_KEY = int(np.int32(np.uint32(0xFF800000) ^ np.uint32(0x7FFFFFFF)))
NT_DIMS = (((1,), (1,)), ((), ()))


def _sortable_key(x):
    bits = lax.bitcast_convert_type(x, I32)
    return bits ^ ((bits >> 31) & 0x7FFFFFFF)


def _const_spec(shape):
    nd = len(shape)
    return pl.BlockSpec(shape, lambda *_: (0,) * nd, pipeline_mode=pl.Buffered(1))


def _project(xb, wt_ref, cos, nsin, psin, pool_width):
    tm = xb.shape[0]
    lo = lax.broadcasted_iota(I32, (tm, LANES), 1) < HEAD_DIM
    attn_w, kv_w, idx_w = N_HEADS * HEAD_DIM, N_KV_HEADS * HEAD_DIM, IDX_HEADS * IDX_DIM
    off_q = pool_width
    off_k = off_q + attn_w
    off_v = off_k + kv_w
    off_qi = off_v + kv_w
    off_kiwi = off_qi + idx_w

    def proj(c0, width):
        return lax.dot_general(xb, wt_ref[c0:c0 + width, :], NT_DIMS, preferred_element_type=F32)

    def rope(z, c, s1, s2):
        return z * c + pltpu.roll(z, LANES - ROT_DIM // 2, 1) * s1 + pltpu.roll(z, ROT_DIM // 2, 1) * s2

    def rope_tiles(z):
        return [rope(z[:, j * LANES:(j + 1) * LANES], cos, nsin, psin) for j in range(z.shape[1] // LANES)]

    u = proj(0, pool_width)
    q = [t * (HEAD_DIM ** -0.5) for t in rope_tiles(proj(off_q, attn_w))]
    qi = rope_tiles(proj(off_qi, idx_w))
    k = rope_tiles(proj(off_k, kv_w))
    v = proj(off_v, kv_w)
    kw = rope(proj(off_kiwi, LANES), jnp.where(lo, cos, IDX_W_SCALE), jnp.where(lo, nsin, 0.0),
              jnp.where(lo, psin, 0.0))
    return u, q, qi, k, v, kw


def _pool_mix(d, g, wpool_ref, pscale_ref, gw):
    z = jnp.dot(d.astype(BF16), wpool_ref[g], preferred_element_type=F32)
    return z * pscale_ref[:, g * gw:(g + 1) * gw]


def _inproj_body(x_ref, wt_ref, tc_ref, ts1_ref, ts2_ref, wpool_ref, pscale_ref,
                 utail_ref, a_ref, qt_ref, qit_ref, kb_ref, kib_ref, kt_ref, vt_ref, vtb_ref, kit_ref, wt_out_ref,
                 uext_ref, *, tm, tiles_per_batch, pool_width):
    i = pl.program_id(0)
    u, q, qi, k, v, kw = _project(x_ref[...].astype(BF16), wt_ref, tc_ref[...], ts1_ref[...], ts2_ref[...],
                                  pool_width)
    lo = lax.broadcasted_iota(I32, (tm, LANES), 1) < HEAD_DIM
    gw = pool_width // len(POOL_WINDOWS)

    def blocks(tile):
        return [(slice(blk * Q_TILE, (blk + 1) * Q_TILE), tile[blk * Q_TILE:(blk + 1) * Q_TILE].T)
                for blk in range(tm // Q_TILE)]

    def store_head_t(dst_ref, h, tile):
        for blk, (_, t) in enumerate(blocks(tile)):
            dst_ref[blk, :, h * Q_TILE:(h + 1) * Q_TILE] = t.astype(BF16)

    for j, p in enumerate(q):
        pr = pltpu.roll(p, HEAD_DIM, 1)
        if (j // 2) % 2 == 0:
            store_head_t(qt_ref, 2 * j, jnp.where(lo, p, 0.0))
            store_head_t(qt_ref, 2 * j + 1, jnp.where(lo, pr, 0.0))
        else:
            store_head_t(qt_ref, 2 * j, jnp.where(lo, 0.0, pr))
            store_head_t(qt_ref, 2 * j + 1, jnp.where(lo, 0.0, p))
    for j, p in enumerate(qi):
        store_head_t(qit_ref, 2 * j, jnp.where(lo, p, 0.0))
        store_head_t(qit_ref, 2 * j + 1, jnp.where(lo, pltpu.roll(p, IDX_DIM, 1), 0.0))

    for j, kr in enumerate(k):
        lanes = slice(j * LANES, (j + 1) * LANES)
        kb_ref[:, lanes] = kr.astype(BF16)
        for cols, t in blocks(kr):
            kt_ref[0, lanes, cols] = t
        for cols, t in blocks(v[:, lanes]):
            vt_ref[0, lanes, cols] = t
            vtb_ref[lanes, cols] = t.astype(BF16)
    kib_ref[...] = kw.astype(BF16)
    for cols, t in blocks(kw):
        kit_ref[0, :, cols] = t[:IDX_DIM, :]
        wt_out_ref[:, cols] = t[IDX_DIM:IDX_DIM + IDX_HEADS, :]

    first = (i % tiles_per_batch) == 0

    @pl.when(first)
    def _():
        uext_ref[0:16, :] = jnp.zeros((16, pool_width), F32)

    @pl.when(jnp.logical_not(first))
    def _():
        uext_ref[0:16, :] = uext_ref[tm:tm + 16, :]

    uext_ref[16:16 + tm, :] = u
    utail_ref[0] = u[tm - 16:tm, :]
    pos = (i % tiles_per_batch) * tm + lax.broadcasted_iota(I32, (tm, 1), 0)
    for g, w in enumerate(POOL_WINDOWS):
        lanes = slice(g * gw, (g + 1) * gw)
        acc = u[:, lanes]
        for jj in range(1, w):
            acc = acc + uext_ref[16 - jj:16 - jj + tm, lanes]
        cnt = jnp.minimum(w, pos + 1).astype(F32)
        d = acc / cnt - u[:, lanes]
        a_ref[:, lanes] = _pool_mix(d, g, wpool_ref, pscale_ref, gw).astype(BF16)


def _in_projection(x, wt_a, tabs, wpool, pscale, *, batch, seq):
    n_rows, d_model = x.shape
    tm = ROW_TILE
    pool_width = wpool.shape[0] * wpool.shape[1]
    nblk = n_rows // Q_TILE
    kv_w = N_KV_HEADS * HEAD_DIM
    tpb = seq // tm
    row = lambda width: pl.BlockSpec((tm, width), lambda i: (i, 0))
    col = lambda height: pl.BlockSpec((height, tm), lambda i: (0, i))
    per_seq = lambda height: pl.BlockSpec((1, height, tm), lambda i: (i // tpb, 0, i % tpb))
    tab = pl.BlockSpec((tm, LANES), lambda i: (i % tpb, 0))
    hm = pl.BlockSpec((tm // Q_TILE, LANES, N_HEADS * Q_TILE), lambda i: (i, 0, 0))
    body = functools.partial(_inproj_body, tm=tm, tiles_per_batch=tpb, pool_width=pool_width)
    return pl.pallas_call(
        body,
        grid=(n_rows // tm,),
        in_specs=[row(d_model), _const_spec(wt_a.shape), tab, tab, tab,
                  _const_spec(wpool.shape), _const_spec(pscale.shape)],
        out_specs=[pl.BlockSpec((1, 16, pool_width), lambda i: (i // tpb, 0, 0)),
                   row(pool_width), hm, hm, row(kv_w), row(LANES), per_seq(kv_w), per_seq(kv_w), col(kv_w),
                   per_seq(IDX_DIM), col(IDX_HEADS)],
        out_shape=[
            jax.ShapeDtypeStruct((batch, 16, pool_width), F32),
            jax.ShapeDtypeStruct((n_rows, pool_width), BF16),
            jax.ShapeDtypeStruct((nblk, LANES, N_HEADS * Q_TILE), BF16),
            jax.ShapeDtypeStruct((nblk, LANES, IDX_HEADS * Q_TILE), BF16),
            jax.ShapeDtypeStruct((n_rows, kv_w), BF16),
            jax.ShapeDtypeStruct((n_rows, LANES), BF16),
            jax.ShapeDtypeStruct((batch, kv_w, seq), F32),
            jax.ShapeDtypeStruct((batch, kv_w, seq), F32),
            jax.ShapeDtypeStruct((kv_w, n_rows), BF16),
            jax.ShapeDtypeStruct((batch, IDX_DIM, seq), F32),
            jax.ShapeDtypeStruct((IDX_HEADS, n_rows), F32),
        ],
        scratch_shapes=[pltpu.VMEM((tm + 16, pool_width), F32)],
        compiler_params=pltpu.CompilerParams(dimension_semantics=("arbitrary",), vmem_limit_bytes=VMEM_LIMIT),
        name="in_projection",
    )(x, wt_a, *tabs, wpool, pscale)


def _inproj_sample_body(x_ref, wt_ref, tc_ref, ts1_ref, ts2_ref, wpool_ref, pscale_ref, state_ref,
                        u_ref, a_ref, q_ref, qi_ref, k_ref, v_ref, kiwi_ref, *, pool_width, sample_pos):
    u, q, qi, k, v, kw = _project(x_ref[...].astype(BF16), wt_ref, tc_ref[...], ts1_ref[...], ts2_ref[...],
                                  pool_width)
    gw = pool_width // len(POOL_WINDOWS)
    u_ref[...] = u
    for j, t in enumerate(q):
        q_ref[:, j * LANES:(j + 1) * LANES] = t.astype(BF16)
    for j, t in enumerate(qi):
        qi_ref[:, j * LANES:(j + 1) * LANES] = t.astype(BF16)
    for j, t in enumerate(k):
        k_ref[:, j * LANES:(j + 1) * LANES] = t
    v_ref[...] = v
    kiwi_ref[...] = kw
    for g, w in enumerate(POOL_WINDOWS):
        lanes = slice(g * gw, (g + 1) * gw)
        acc = u[:, lanes]
        for jj in range(1, w):
            acc = acc + state_ref[POOL_STATE - jj, :, lanes]
        d = acc / float(min(w, sample_pos + 1)) - u[:, lanes]
        a_ref[:, lanes] = _pool_mix(d, g, wpool_ref, pscale_ref, gw).astype(BF16)


def _in_projection_sample(x, wt_a, tabs, wpool, pscale, state_t, *, sample_pos):
    n_rows, d_model = x.shape
    pool_width = wpool.shape[0] * wpool.shape[1]
    kv_w = N_KV_HEADS * HEAD_DIM
    full = lambda shape: pl.BlockSpec(shape, lambda i: (0,) * len(shape))
    body = functools.partial(_inproj_sample_body, pool_width=pool_width, sample_pos=sample_pos)
    out_shape = [
        jax.ShapeDtypeStruct((n_rows, pool_width), F32),
        jax.ShapeDtypeStruct((n_rows, pool_width), BF16),
        jax.ShapeDtypeStruct((n_rows, N_HEADS * HEAD_DIM), BF16),
        jax.ShapeDtypeStruct((n_rows, IDX_HEADS * IDX_DIM), BF16),
        jax.ShapeDtypeStruct((n_rows, kv_w), F32),
        jax.ShapeDtypeStruct((n_rows, kv_w), F32),
        jax.ShapeDtypeStruct((n_rows, LANES), F32),
    ]
    return pl.pallas_call(
        body,
        grid=(1,),
        in_specs=[full(x.shape), _const_spec(wt_a.shape), full(tabs[0].shape), full(tabs[1].shape),
                  full(tabs[2].shape), _const_spec(wpool.shape), _const_spec(pscale.shape), full(state_t.shape)],
        out_specs=[full(s.shape) for s in out_shape],
        out_shape=out_shape,
        compiler_params=pltpu.CompilerParams(dimension_semantics=("arbitrary",), vmem_limit_bytes=VMEM_LIMIT),
        name="in_projection_sample",
    )(x, wt_a, *tabs, wpool, pscale, state_t)


def _rope_tables(positions):
    pos = np.asarray(positions, np.float32)
    n = pos.shape[0]
    half = ROT_DIM // 2
    inv = np.exp(-np.arange(half, dtype=np.float32) * np.float32(math.log(ROPE_THETA) / half)).astype(np.float32)
    ang = (pos[:, None] * inv[None, :]).astype(np.float32).astype(np.float64)
    cos, sin = np.cos(ang).astype(np.float32), np.sin(ang).astype(np.float32)
    zeros = np.zeros((n, half), np.float32)
    rest = HEAD_DIM - ROT_DIM
    c = np.concatenate([cos, cos, np.ones((n, rest), np.float32)], axis=1)
    s1 = np.concatenate([-sin, zeros, np.zeros((n, rest), np.float32)], axis=1)
    s2 = np.concatenate([zeros, sin, np.zeros((n, rest), np.float32)], axis=1)
    return tuple(jnp.asarray(np.concatenate([t, t], axis=1)) for t in (c, s1, s2))


def _topk_threshold(count_ge, init, topk):
    def step(it, t):
        cand = t + jnp.left_shift(jnp.int32(1), 31 - it)
        return jnp.where(count_ge(cand) >= topk, cand, t)
    return lax.fori_loop(0, 32, step, init)


def _fold_slabs(x, op):
    assert x.shape[0] & (x.shape[0] - 1) == 0
    while x.shape[0] > 1:
        half = x.shape[0] // 2
        x = op(x[:half], x[half:])
    return x[0]


def _prompt_attn_body(qit_ref, wt_ref, qt_ref, kib_ref, kb_ref, vt_ref, o_ref,
                      sc_ref, thr_ref, m_ref, l_ref, acc_ref, *, topk):
    i = pl.program_id(1)
    tq, kc = Q_TILE, KEY_CHUNK
    n_keys = (i + 1) * tq
    n_chunks = (n_keys + kc - 1) // kc
    qpos = i * tq + lax.broadcasted_iota(I32, (1, tq), 1)
    group = N_HEADS // N_KV_HEADS
    gcols = group * tq

    def score_chunk(c, carry):
        k0 = pl.multiple_of(c * kc, kc)
        kchunk = kib_ref[pl.ds(k0, kc), :]
        acc = jnp.zeros((kc, tq), F32)
        for hg in range(IDX_HEADS // group):
            d = jnp.dot(kchunk, qit_ref[0, :, hg * gcols:(hg + 1) * gcols], preferred_element_type=F32)
            for r in range(group):
                h = hg * group + r
                acc = acc + jnp.maximum(d[:, r * tq:(r + 1) * tq], 0.0) * wt_ref[h:h + 1, :]
        kpos = k0 + lax.broadcasted_iota(I32, (kc, 1), 0)
        sc_ref[pl.ds(k0, kc), :] = _sortable_key(jnp.where(kpos <= qpos, acc, -jnp.inf))
        return carry

    lax.fori_loop(0, n_chunks, score_chunk, 0)

    thr_ref[...] = jnp.full((1, tq), NEG_INF_KEY + 1, I32)

    @pl.when(n_keys > topk)
    def _():
        def count_ge(cand):
            candb = jnp.broadcast_to(cand, (SUBLANES, tq))

            def blk(c, acc):
                k0 = pl.multiple_of(c * kc, kc)
                hit = jnp.where(sc_ref[pl.ds(k0, kc), :].reshape(kc // SUBLANES, SUBLANES, tq) >= candb, 1.0, 0.0)
                return acc + _fold_slabs(hit, jnp.add)
            acc = lax.fori_loop(0, n_chunks, blk, jnp.zeros((SUBLANES, tq), F32))
            return jnp.sum(acc, axis=0, keepdims=True)

        t = _topk_threshold(count_ge, jnp.full((1, tq), INT_MIN, I32), float(topk))
        thr_ref[...] = jnp.where(qpos + 1 > topk, t, NEG_INF_KEY + 1)

    m_ref[...] = jnp.full(m_ref.shape, -jnp.inf, F32)
    l_ref[...] = jnp.zeros(l_ref.shape, F32)
    acc_ref[...] = jnp.zeros(acc_ref.shape, F32)

    def attn_chunk(c, carry):
        k0 = pl.multiple_of(c * kc, kc)
        sel = sc_ref[pl.ds(k0, kc), :] >= thr_ref[...]
        for g in range(N_KV_HEADS):
            pair = slice((g // 2) * LANES, (g // 2 + 1) * LANES)
            cols = slice(g * gcols, (g + 1) * gcols)
            s = jnp.dot(kb_ref[pl.ds(k0, kc), pair], qt_ref[0, :, cols], preferred_element_type=F32)
            s = jnp.concatenate([jnp.where(sel, s[:, r * tq:(r + 1) * tq], -jnp.inf) for r in range(group)], axis=1)
            m_old = m_ref[:, cols]
            m_new = jnp.maximum(m_old, jnp.max(s, axis=0, keepdims=True))
            m_safe = jnp.where(m_new == -jnp.inf, 0.0, m_new)
            alpha = jnp.exp(m_old - m_safe)
            p = jnp.exp(s - m_safe)
            l_ref[:, cols] = alpha * l_ref[:, cols] + jnp.sum(p, axis=0, keepdims=True)
            pv = jnp.dot(vt_ref[pair, pl.ds(k0, kc)], p.astype(BF16), preferred_element_type=F32)
            acc_ref[g] = alpha * acc_ref[g] + pv
            m_ref[:, cols] = m_new
        return carry

    lax.fori_loop(0, n_chunks, attn_chunk, 0)

    for j in range(N_HEADS // 2):
        halves = []
        for h in (2 * j, 2 * j + 1):
            g, r = h // group, h % group
            rows = slice((g % 2) * HEAD_DIM, (g % 2 + 1) * HEAD_DIM)
            cols = slice(r * tq, (r + 1) * tq)
            halves.append(acc_ref[g, rows, cols] / l_ref[:, g * gcols + r * tq:g * gcols + (r + 1) * tq])
        o_ref[:, j * LANES:(j + 1) * LANES] = jnp.concatenate(halves, axis=0).T.astype(BF16)


def _prompt_attention(qit, wt, qt, kib, kb, vt, *, batch, seq):
    nq = seq // Q_TILE
    topk = min(TOPK_MAX, seq // 4)
    kv_w = N_KV_HEADS * HEAD_DIM
    per_q = lambda b, i: (b * nq + i, 0, 0)
    body = functools.partial(_prompt_attn_body, topk=topk)
    return pl.pallas_call(
        body,
        grid=(batch, nq),
        in_specs=[
            pl.BlockSpec((1, LANES, IDX_HEADS * Q_TILE), per_q),
            pl.BlockSpec((IDX_HEADS, Q_TILE), lambda b, i: (0, b * nq + i)),
            pl.BlockSpec((1, LANES, N_HEADS * Q_TILE), per_q),
            pl.BlockSpec((seq, LANES), lambda b, i: (b, 0)),
            pl.BlockSpec((seq, kv_w), lambda b, i: (b, 0)),
            pl.BlockSpec((kv_w, seq), lambda b, i: (0, b)),
        ],
        out_specs=pl.BlockSpec((Q_TILE, N_HEADS * HEAD_DIM), lambda b, i: (b * nq + i, 0)),
        out_shape=jax.ShapeDtypeStruct((batch * seq, N_HEADS * HEAD_DIM), BF16),
        scratch_shapes=[
            pltpu.VMEM((seq, Q_TILE), I32),
            pltpu.VMEM((1, Q_TILE), I32),
            pltpu.VMEM((1, N_HEADS * Q_TILE), F32),
            pltpu.VMEM((1, N_HEADS * Q_TILE), F32),
            pltpu.VMEM((N_KV_HEADS, LANES, (N_HEADS // N_KV_HEADS) * Q_TILE), F32),
        ],
        compiler_params=pltpu.CompilerParams(dimension_semantics=("arbitrary", "arbitrary"),
                                             vmem_limit_bytes=VMEM_LIMIT),
        name="prompt_attention",
    )(qit, wt, qt, kib, kb, vt)


SAMPLE_PAGES_PER_CHUNK = 16


def _sample_attn_body(pt_ref, qi_ref, w_ref, qs_ref, kicur_ref, kcur_ref, vcur_ref, ckidx_hbm, ck_hbm, cv_hbm,
                      o_ref, kidx_buf, k_buf, v_buf, sc_ref, sem_idx, sem_k, sem_v, *, n_pages, topk):
    b = pl.program_id(0)
    last = pl.num_programs(0) - 1
    ppc = SAMPLE_PAGES_PER_CHUNK
    n_chunks = n_pages // ppc
    ck = ppc * PAGE_SIZE
    seg = sc_ref.shape[1]
    segs_per_chunk = ck // seg

    def idx_copy(seq, p):
        return pltpu.make_async_copy(ckidx_hbm.at[pt_ref[seq, p]],
                                     kidx_buf.at[seq % 2, :, pl.ds(p * PAGE_SIZE, PAGE_SIZE)], sem_idx.at[seq % 2])

    def kv_copies(seq, c, p):
        slot = c % 2
        page = pt_ref[seq, c * ppc + p]
        window = pl.ds(p * PAGE_SIZE, PAGE_SIZE)
        return (pltpu.make_async_copy(ck_hbm.at[page], k_buf.at[slot, :, window], sem_k.at[slot]),
                pltpu.make_async_copy(cv_hbm.at[page], v_buf.at[slot, :, window], sem_v.at[slot]))

    def start_idx(seq):
        for p in range(n_pages):
            idx_copy(seq, p).start()

    def start_kv(seq, c):
        for p in range(ppc):
            for cp in kv_copies(seq, c, p):
                cp.start()

    def wait_kv(seq, c):
        for p in range(ppc):
            for cp in kv_copies(seq, c, p):
                cp.wait()

    @pl.when(b == 0)
    def _():
        start_idx(b)
        start_kv(b, 0)

    for p in range(n_pages):
        idx_copy(b, p).wait()

    qi = qi_ref[0][:, :IDX_DIM]
    w = w_ref[0]

    def head_mix(d):
        return jnp.sum(jnp.maximum(d, 0.0) * w, axis=0, keepdims=True)

    def bf16_products(a, row):
        return jnp.sum(a.astype(F32) * row.astype(BF16).astype(F32), axis=1, keepdims=True)

    for c in range(n_chunks):
        d = jnp.dot(qi, kidx_buf[b % 2, :, c * ck:(c + 1) * ck].astype(BF16), preferred_element_type=F32)
        keys = _sortable_key(head_mix(d))
        for r in range(segs_per_chunk):
            sc_ref[c * segs_per_chunk + r:c * segs_per_chunk + r + 1, :] = keys[:, r * seg:(r + 1) * seg]
    cur_key = _sortable_key(head_mix(bf16_products(qi, kicur_ref[0][:, :IDX_DIM])))

    if n_pages * PAGE_SIZE + 1 > topk:
        def count_ge(cand):
            hit = jnp.where(sc_ref[...] >= cand, 1.0, 0.0)
            past_hits = jnp.sum(jnp.sum(hit, axis=1, keepdims=True), axis=0, keepdims=True)
            return past_hits + jnp.where(cur_key >= cand, 1.0, 0.0)
        thr = _topk_threshold(count_ge, jnp.full((1, 1), INT_MIN, I32), float(topk))
    else:
        thr = jnp.full((1, 1), NEG_INF_KEY + 1, I32)

    qs = qs_ref[0]
    m = jnp.full((N_HEADS, 1), -jnp.inf, F32)
    l = jnp.zeros((N_HEADS, 1), F32)
    acc = jnp.zeros((N_HEADS, N_KV_HEADS * HEAD_DIM), F32)

    def update(m, l, acc, s, sel, pv):
        s = jnp.where(sel, s, -jnp.inf)
        m_new = jnp.maximum(m, jnp.max(s, axis=1, keepdims=True))
        m_safe = jnp.where(m_new == -jnp.inf, 0.0, m_new)
        alpha = jnp.exp(m - m_safe)
        p = jnp.exp(s - m_safe)
        l = alpha * l + jnp.sum(p, axis=1, keepdims=True)
        acc = alpha * acc + pv(p.astype(BF16))
        return m_new, l, acc

    def prefetch_next_sequence():
        @pl.when(b < last)
        def _():
            start_idx(b + 1)
            start_kv(b + 1, 0)

    slot0_free_early = n_chunks % 2 == 0
    for c in range(n_chunks):
        if c + 1 < n_chunks:
            start_kv(b, c + 1)
        elif slot0_free_early:
            prefetch_next_sequence()
        wait_kv(b, c)
        slot = c % 2
        s = jnp.dot(qs, k_buf[slot].astype(BF16), preferred_element_type=F32)
        vt = v_buf[slot].astype(BF16)
        sel = jnp.concatenate([sc_ref[c * segs_per_chunk + r:c * segs_per_chunk + r + 1, :]
                               for r in range(segs_per_chunk)], axis=1) >= thr
        m, l, acc = update(m, l, acc, s, sel, lambda p: lax.dot_general(p, vt, NT_DIMS, preferred_element_type=F32))
    if not slot0_free_early:
        prefetch_next_sequence()
    v_cur = vcur_ref[0].astype(BF16).astype(F32)
    m, l, acc = update(m, l, acc, bf16_products(qs, kcur_ref[0]), cur_key >= thr,
                       lambda p: p.astype(F32) * v_cur)
    o_ref[0] = acc / l


def _keys_minor(cache):
    pages, page_size = cache.shape[:2]
    return jnp.moveaxis(cache, 1, -1).reshape(pages, -1, page_size)


def _sample_attention(page_table, qi_s, w_s, qs, kicur, kcur, vcur, cache_kidx, cache_k, cache_v):
    dec_batch, n_pages = page_table.shape
    past = n_pages * PAGE_SIZE
    topk = min(TOPK_MAX, (past + 1) // 4)
    kv_w = N_KV_HEADS * HEAD_DIM
    ppc = SAMPLE_PAGES_PER_CHUNK
    per_b = lambda b, pt: (b, 0, 0)
    any_spec = pl.BlockSpec(memory_space=pl.ANY)
    grid_spec = pltpu.PrefetchScalarGridSpec(
        num_scalar_prefetch=1,
        grid=(dec_batch,),
        in_specs=[
            pl.BlockSpec((1, IDX_HEADS, LANES), per_b),
            pl.BlockSpec((1, IDX_HEADS, 1), per_b),
            pl.BlockSpec((1, N_HEADS, kv_w), per_b),
            pl.BlockSpec((1, 1, LANES), per_b),
            pl.BlockSpec((1, 1, kv_w), per_b),
            pl.BlockSpec((1, 1, kv_w), per_b),
            any_spec, any_spec, any_spec,
        ],
        out_specs=pl.BlockSpec((1, N_HEADS, kv_w), per_b),
        scratch_shapes=[
            pltpu.VMEM((2, IDX_DIM, past), F32),
            pltpu.VMEM((2, kv_w, ppc * PAGE_SIZE), F32),
            pltpu.VMEM((2, kv_w, ppc * PAGE_SIZE), F32),
            pltpu.VMEM((SUBLANES, past // SUBLANES), I32),
            pltpu.SemaphoreType.DMA((2,)),
            pltpu.SemaphoreType.DMA((2,)),
            pltpu.SemaphoreType.DMA((2,)),
        ],
    )
    body = functools.partial(_sample_attn_body, n_pages=n_pages, topk=topk)
    return pl.pallas_call(
        body,
        grid_spec=grid_spec,
        out_shape=jax.ShapeDtypeStruct((dec_batch, N_HEADS, kv_w), F32),
        compiler_params=pltpu.CompilerParams(dimension_semantics=("arbitrary",), vmem_limit_bytes=VMEM_LIMIT),
        name="sample_attention",
    )(page_table, qi_s, w_s, qs, kicur, kcur, vcur, cache_kidx, cache_k, cache_v)


def _mm(a, b, precise, *, nt=False):
    dims = NT_DIMS if nt else (((1,), (0,)), ((), ()))
    if precise:
        return lax.dot_general(a.astype(F32), b, dims, precision=lax.Precision.HIGHEST, preferred_element_type=F32)
    return lax.dot_general(a.astype(BF16), b, dims, preferred_element_type=F32)


def _operand_dtype(precise):
    return F32 if precise else BF16


MERGE_COL_CHUNK = 512


def _merge_body(x_ref, a_ref, o_ref, wgat_ref, wgbt_ref, wba_ref, wbb_ref, out_ref, *, precise):
    x, a, o = x_ref[...], a_ref[...], o_ref[...]
    cc = MERGE_COL_CHUNK
    for c in range(out_ref.shape[1] // cc):
        cols = slice(c * cc, (c + 1) * cc)
        ga = _mm(x, wgat_ref[cols, :], precise, nt=True)
        gb = _mm(x, wgbt_ref[cols, :], precise, nt=True)
        ya = _mm(a, wba_ref[:, cols], precise)
        yo = _mm(o, wbb_ref[:, cols], precise)
        out_ref[:, cols] = (jax.nn.sigmoid(ga) * ya + jax.nn.sigmoid(gb) * yo).astype(out_ref.dtype)


def _merge(x, a, o, wgat, wgbt, wba, wbb, *, tm, col_block, precise):
    n_rows, d_model = x.shape
    mode = dict(pipeline_mode=pl.Buffered(1)) if col_block == d_model else {}
    gate_spec = pl.BlockSpec((col_block, d_model), lambda i, j: (j, 0), **mode)
    branch_spec = lambda w: pl.BlockSpec((w.shape[0], col_block), lambda i, j: (0, j), **mode)
    row = lambda width: pl.BlockSpec((tm, width), lambda i, j: (i, 0))
    body = functools.partial(_merge_body, precise=precise)
    return pl.pallas_call(
        body,
        grid=(n_rows // tm, d_model // col_block),
        in_specs=[row(d_model), row(a.shape[1]), row(o.shape[1]), gate_spec, gate_spec,
                  branch_spec(wba), branch_spec(wbb)],
        out_specs=pl.BlockSpec((tm, col_block), lambda i, j: (i, j)),
        out_shape=jax.ShapeDtypeStruct((n_rows, d_model), _operand_dtype(precise)),
        compiler_params=pltpu.CompilerParams(dimension_semantics=("arbitrary", "arbitrary"),
                                             vmem_limit_bytes=VMEM_LIMIT),
        name="branch_merge",
    )(x, a, o, wgat, wgbt, wba, wbb)


def _layer_norm(x, g, b):
    mu = jnp.mean(x, axis=-1, keepdims=True)
    var = jnp.mean(jnp.square(x - mu), axis=-1, keepdims=True)
    return (x - mu) * lax.rsqrt(var + LN_EPS) * g + b


def _route(logits):
    lane = lax.broadcasted_iota(I32, logits.shape, 1)
    lane_f = lane.astype(F32)
    big = float(LANES)

    def masked_softmax(mask):
        x = jnp.where(mask, logits, -jnp.inf)
        e = jnp.exp(x - jnp.max(x, axis=1, keepdims=True))
        return jnp.where(mask, e / jnp.sum(e, axis=1, keepdims=True), -1.0)

    def top1(p):
        best = jnp.max(p, axis=1, keepdims=True)
        idx = jnp.min(jnp.where(p == best, lane_f, big), axis=1, keepdims=True)
        return best, idx

    gp = masked_softmax((lane >= N_EXPERTS) & (lane < N_EXPERTS + N_GROUPS))
    g_p, g_lane = top1(gp)
    g_idx = g_lane.astype(I32) - N_EXPERTS
    ep = masked_softmax((lane < N_EXPERTS) & ((lane >> int(math.log2(EXPERTS_PER_GROUP))) == g_idx))
    p1, i1 = top1(ep)
    p2, i2 = top1(jnp.where(lane_f == i1, -1.0, ep))
    denom = p1 + p2
    return jnp.where(lane_f == i1, g_p * p1 / denom, 0.0) + jnp.where(lane_f == i2, g_p * p2 / denom, 0.0)


def _outproj_body(m_ref, x_ref, wout_ref, g_ref, b_ref, wr_ref, h_ref, hb_ref, dense_ref, *, precise):
    mo = _mm(m_ref[...], wout_ref[...], precise)
    h = _layer_norm(ALPHA * x_ref[...] + mo, g_ref[...], b_ref[...])
    h_ref[...] = h
    hb_ref[...] = h.astype(BF16)
    dense_ref[...] = _route(_mm(h, wr_ref[...], precise))


def _out_projection(merged, x, wout, ln_g, ln_b, w_route, *, tm, precise):
    n_rows, d_model = x.shape
    body = functools.partial(_outproj_body, precise=precise)
    out_row = lambda width: pl.BlockSpec((tm, width), lambda i: (i, 0))
    return pl.pallas_call(
        body,
        grid=(n_rows // tm,),
        in_specs=[out_row(d_model), out_row(d_model), _const_spec(wout.shape),
                  _const_spec(ln_g.shape), _const_spec(ln_b.shape), _const_spec(w_route.shape)],
        out_specs=[out_row(d_model), out_row(d_model), out_row(LANES)],
        out_shape=[jax.ShapeDtypeStruct((n_rows, d_model), F32),
                   jax.ShapeDtypeStruct((n_rows, d_model), BF16),
                   jax.ShapeDtypeStruct((n_rows, LANES), F32)],
        compiler_params=pltpu.CompilerParams(dimension_semantics=("arbitrary",), vmem_limit_bytes=VMEM_LIMIT),
        name="out_projection",
    )(merged, x, wout, ln_g, ln_b, w_route)


def _moe_body(hb_ref, dense_ref, h_ref, wg_ref, wu_ref, wd_ref, g_ref, b_ref, y_ref, acc_ref):
    e = pl.program_id(1)

    @pl.when(e == 0)
    def _():
        acc_ref[...] = jnp.zeros(acc_ref.shape, F32)

    hb = hb_ref[...]
    gate = jnp.dot(hb, wg_ref[0], preferred_element_type=F32)
    up = jnp.dot(hb, wu_ref[0], preferred_element_type=F32)
    lane = lax.broadcasted_iota(I32, dense_ref.shape, 1)
    dcol = jnp.sum(jnp.where(lane == e, dense_ref[...], 0.0), axis=1, keepdims=True)
    hh = (gate * jax.nn.sigmoid(gate)) * up * dcol
    acc_ref[...] += jnp.dot(hh.astype(BF16), wd_ref[0], preferred_element_type=F32)

    @pl.when(e == pl.num_programs(1) - 1)
    def _():
        y_ref[...] = _layer_norm(ALPHA * h_ref[...] + acc_ref[...], g_ref[...], b_ref[...])


def _moe(hb, dense, h, wg, wu, wd, ln_g, ln_b, *, tm):
    n_rows, d_model = h.shape
    n_experts, _, d_expert = wg.shape
    row = lambda width: pl.BlockSpec((tm, width), lambda i, e: (i, 0))
    return pl.pallas_call(
        _moe_body,
        grid=(n_rows // tm, n_experts),
        in_specs=[row(d_model), row(LANES), row(d_model),
                  pl.BlockSpec((1, d_model, d_expert), lambda i, e: (e, 0, 0)),
                  pl.BlockSpec((1, d_model, d_expert), lambda i, e: (e, 0, 0)),
                  pl.BlockSpec((1, d_expert, d_model), lambda i, e: (e, 0, 0)),
                  _const_spec(ln_g.shape), _const_spec(ln_b.shape)],
        out_specs=row(d_model),
        out_shape=jax.ShapeDtypeStruct((n_rows, d_model), F32),
        scratch_shapes=[pltpu.VMEM((tm, d_model), F32)],
        compiler_params=pltpu.CompilerParams(dimension_semantics=("arbitrary", "arbitrary"),
                                             vmem_limit_bytes=VMEM_LIMIT),
        name="expert_mlp",
    )(hb, dense, h, wg, wu, wd, ln_g, ln_b)


def _layer(x_prompt, x_sample, cache_k, cache_v, cache_kidx, state_pool, page_table, w_in, w_pool, pool_scale,
           w_branch_a, w_branch_b, w_out, ln1_g, ln1_b, w_group, w_expert_router, w_gate, w_up, w_down,
           ln2_g, ln2_b):
    batch, seq, d_model = x_prompt.shape
    dec_batch = x_sample.shape[0]
    n_pages = page_table.shape[1]
    past = n_pages * PAGE_SIZE
    n_prompt = batch * seq
    kv_w = N_KV_HEADS * HEAD_DIM
    attn_w = N_HEADS * HEAD_DIM
    pool_width = w_pool.shape[0] * w_pool.shape[1]
    group = N_HEADS // N_KV_HEADS
    assert x_sample.shape[1] == 1 and seq % KEY_CHUNK == 0 and seq % MOE_ROW_TILE == 0
    assert dec_batch % 16 == 0 and n_pages % SAMPLE_PAGES_PER_CHUNK == 0
    assert (SAMPLE_PAGES_PER_CHUNK * PAGE_SIZE) % (past // SUBLANES) == 0

    x_p = x_prompt.reshape(n_prompt, d_model)
    x_s = x_sample.reshape(dec_batch, d_model)

    w_t = jnp.transpose(w_in)
    a_width = w_in.shape[1] - 2 * d_model
    wt_a = jnp.concatenate([w_t[:a_width], jnp.zeros((-a_width % LANES, d_model), F32)], axis=0).astype(BF16)
    wgat, wgbt = w_t[a_width:a_width + d_model], w_t[a_width + d_model:]
    wpool_b, pscale = w_pool.astype(BF16), pool_scale.reshape(1, pool_width)

    utail, a_p, qt, qit, kb, kib, kt, vt, vtb, kit, wt = _in_projection(
        x_p, wt_a, _rope_tables(np.arange(seq)), wpool_b, pscale, batch=batch, seq=seq)
    o_prompt = _prompt_attention(qit, wt, qt, kib, kb, vtb, batch=batch, seq=seq)

    u_s, a_s, q_s, qi_s, k_s, v_s, kiwi_s = _in_projection_sample(
        x_s, wt_a, _rope_tables(np.full((dec_batch,), past)), wpool_b, pscale, jnp.transpose(state_pool, (1, 0, 2)),
        sample_pos=past)
    qi_h = qi_s.reshape(dec_batch, IDX_HEADS, IDX_DIM)
    qi_h = jnp.concatenate([qi_h, jnp.zeros_like(qi_h)], axis=-1)
    in_group = (jnp.arange(N_HEADS)[:, None] // group == jnp.arange(N_KV_HEADS)[None, :])[None, :, :, None]
    qs = jnp.where(in_group, q_s.reshape(dec_batch, N_HEADS, 1, HEAD_DIM), 0).reshape(dec_batch, N_HEADS, kv_w)
    w_s = kiwi_s[:, IDX_DIM:IDX_DIM + IDX_HEADS].reshape(dec_batch, IDX_HEADS, 1)
    o_s = _sample_attention(page_table, qi_h, w_s, qs, kiwi_s.reshape(dec_batch, 1, LANES),
                            k_s.reshape(dec_batch, 1, kv_w), v_s.reshape(dec_batch, 1, kv_w),
                            _keys_minor(cache_kidx), _keys_minor(cache_k), _keys_minor(cache_v))
    o_s = o_s.reshape(dec_batch, N_KV_HEADS, group, N_KV_HEADS, HEAD_DIM)
    o_s = jnp.transpose(jnp.diagonal(o_s, axis1=1, axis2=3), (0, 3, 1, 2)).reshape(dec_batch, attn_w)
    o_sample = o_s.astype(BF16)

    w_route = jnp.concatenate([w_expert_router, w_group,
                               jnp.zeros((d_model, LANES - N_EXPERTS - N_GROUPS), F32)], axis=1)
    g1, b1 = ln1_g.reshape(1, d_model), ln1_b.reshape(1, d_model)
    g2, b2 = ln2_g.reshape(1, d_model), ln2_b.reshape(1, d_model)
    wg, wu, wd = w_gate.astype(BF16), w_up.astype(BF16), w_down.astype(BF16)

    merged_p = _merge(x_p, a_p, o_prompt, wgat.astype(BF16), wgbt.astype(BF16), w_branch_a.astype(BF16),
                      w_branch_b.astype(BF16), tm=ROW_TILE, col_block=d_model, precise=False)
    h_p, hb_p, dense_p = _out_projection(merged_p, x_p, w_out.astype(BF16), g1, b1, w_route.astype(BF16),
                                         tm=ROW_TILE, precise=False)
    y_prompt = _moe(hb_p, dense_p, h_p, wg, wu, wd, g2, b2, tm=MOE_ROW_TILE)

    merged_s = _merge(x_s, a_s, o_sample, wgat, wgbt, w_branch_a, w_branch_b, tm=dec_batch,
                      col_block=MERGE_COL_CHUNK, precise=True)
    h_s, hb_s, dense_s = _out_projection(merged_s, x_s, w_out, g1, b1, w_route, tm=dec_batch, precise=True)
    y_sample = _moe(hb_s, dense_s, h_s, wg, wu, wd, g2, b2, tm=dec_batch)

    heads_t = lambda t: jnp.transpose(t.reshape(batch, N_KV_HEADS, HEAD_DIM, seq), (0, 3, 1, 2))
    pool_sample = jnp.concatenate([state_pool[:, 1:], u_s[:, None, :]], axis=1)
    return (y_prompt.reshape(batch, seq, d_model), y_sample.reshape(dec_batch, 1, d_model),
            heads_t(kt), heads_t(vt), jnp.transpose(kit, (0, 2, 1)), utail[:, 16 - POOL_STATE:],
            k_s.reshape(dec_batch, 1, N_KV_HEADS, HEAD_DIM), v_s.reshape(dec_batch, 1, N_KV_HEADS, HEAD_DIM),
            kiwi_s[:, :IDX_DIM].reshape(dec_batch, 1, IDX_DIM), pool_sample)


def kernel(x_prompt, x_sample, cache_k, cache_v, cache_kidx, state_pool, page_table, w_in, w_pool, pool_scale,
           w_branch_a, w_branch_b, w_out, ln1_g, ln1_b, w_group, w_expert_router, w_gate, w_up, w_down,
           ln2_g, ln2_b):
    assert w_in.shape[0] == DEPTH
    outs = _layer(x_prompt, x_sample, cache_k[0], cache_v[0], cache_kidx[0], state_pool[0], page_table, w_in[0],
                  w_pool[0], pool_scale[0], w_branch_a[0], w_branch_b[0], w_out[0], ln1_g[0], ln1_b[0],
                  w_group[0], w_expert_router[0], w_gate[0], w_up[0], w_down[0], ln2_g[0], ln2_b[0])
    y_p, y_s = outs[0], outs[1]
    return (y_p, y_s) + tuple(o[None] for o in outs[2:])
```

```python
import functools
import math

import jax
import jax.numpy as jnp
import numpy as np
from jax import lax
from jax.experimental import pallas as pl
from jax.experimental.pallas import tpu as pltpu

BF16 = jnp.bfloat16
F32 = jnp.float32
I32 = jnp.int32

PAGE_SIZE = 128
POOL_WINDOWS = (2, 4, 8, 16)
POOL_STATE = 15
N_HEADS = 16
N_KV_HEADS = 4
HEAD_DIM = 64
ROT_DIM = HEAD_DIM // 4
ROPE_THETA = 500000.0
IDX_HEADS = 16
IDX_DIM = 64
IDX_W_SCALE = (IDX_HEADS * IDX_DIM) ** -0.5
TOPK_MAX = 256
N_GROUPS = 4
EXPERTS_PER_GROUP = 8
N_EXPERTS = N_GROUPS * EXPERTS_PER_GROUP
LN_EPS = 1e-5
DEPTH = 1
ALPHA = (2 * DEPTH) ** 0.25

LANES = 128
SUBLANES = 8
Q_TILE = 128
KEY_CHUNK = 512
ROW_TILE = 256
MOE_ROW_TILE = 512
VMEM_LIMIT = 56 * 1024 * 1024

INT_MIN = -2 ** 31
NEG_INF_KEY = int(np.int32(np.uint32(0xFF800000) ^ np.uint32(0x7FFFFFFF)))
NT_DIMS = (((1,), (1,)), ((), ()))


def _sortable_key(x):
    bits = lax.bitcast_convert_type(x, I32)
    return bits ^ ((bits >> 31) & 0x7FFFFFFF)


def _const_spec(shape):
    nd = len(shape)
    return pl.BlockSpec(shape, lambda *_: (0,) * nd, pipeline_mode=pl.Buffered(1))


def _project(xb, wt_ref, cos, nsin, psin, pool_width):
    tm = xb.shape[0]
    lo = lax.broadcasted_iota(I32, (tm, LANES), 1) < HEAD_DIM
    attn_w, kv_w, idx_w = N_HEADS * HEAD_DIM, N_KV_HEADS * HEAD_DIM, IDX_HEADS * IDX_DIM
    off_q = pool_width
    off_k = off_q + attn_w
    off_v = off_k + kv_w
    off_qi = off_v + kv_w
    off_kiwi = off_qi + idx_w

    def proj(c0, width):
        return lax.dot_general(xb, wt_ref[c0:c0 + width, :], NT_DIMS, preferred_element_type=F32)

    def rope(z, c, s1, s2):
        return z * c + pltpu.roll(z, LANES - ROT_DIM // 2, 1) * s1 + pltpu.roll(z, ROT_DIM // 2, 1) * s2

    def rope_tiles(z):
        return [rope(z[:, j * LANES:(j + 1) * LANES], cos, nsin, psin) for j in range(z.shape[1] // LANES)]

    u = proj(0, pool_width)
    q = [t * (HEAD_DIM ** -0.5) for t in rope_tiles(proj(off_q, attn_w))]
    qi = rope_tiles(proj(off_qi, idx_w))
    k = rope_tiles(proj(off_k, kv_w))
    v = proj(off_v, kv_w)
    kw = rope(proj(off_kiwi, LANES), jnp.where(lo, cos, IDX_W_SCALE), jnp.where(lo, nsin, 0.0),
              jnp.where(lo, psin, 0.0))
    return u, q, qi, k, v, kw


def _pool_mix(d, g, wpool_ref, pscale_ref, gw):
    z = jnp.dot(d.astype(BF16), wpool_ref[g], preferred_element_type=F32)
    return z * pscale_ref[:, g * gw:(g + 1) * gw]


def _inproj_body(x_ref, wt_ref, tc_ref, ts1_ref, ts2_ref, wpool_ref, pscale_ref,
                 utail_ref, a_ref, qt_ref, qit_ref, kb_ref, kib_ref, kt_ref, vt_ref, vtb_ref, kit_ref, wt_out_ref,
                 uext_ref, *, tm, tiles_per_batch, pool_width):
    i = pl.program_id(0)
    u, q, qi, k, v, kw = _project(x_ref[...].astype(BF16), wt_ref, tc_ref[...], ts1_ref[...], ts2_ref[...],
                                  pool_width)
    lo = lax.broadcasted_iota(I32, (tm, LANES), 1) < HEAD_DIM
    gw = pool_width // len(POOL_WINDOWS)

    def blocks(tile):
        return [(slice(blk * Q_TILE, (blk + 1) * Q_TILE), tile[blk * Q_TILE:(blk + 1) * Q_TILE].T)
                for blk in range(tm // Q_TILE)]

    def store_head_t(dst_ref, h, tile):
        for blk, (_, t) in enumerate(blocks(tile)):
            dst_ref[blk, :, h * Q_TILE:(h + 1) * Q_TILE] = t.astype(BF16)

    for j, p in enumerate(q):
        pr = pltpu.roll(p, HEAD_DIM, 1)
        if (j // 2) % 2 == 0:
            store_head_t(qt_ref, 2 * j, jnp.where(lo, p, 0.0))
            store_head_t(qt_ref, 2 * j + 1, jnp.where(lo, pr, 0.0))
        else:
            store_head_t(qt_ref, 2 * j, jnp.where(lo, 0.0, pr))
            store_head_t(qt_ref, 2 * j + 1, jnp.where(lo, 0.0, p))
    for j, p in enumerate(qi):
        store_head_t(qit_ref, 2 * j, jnp.where(lo, p, 0.0))
        store_head_t(qit_ref, 2 * j + 1, jnp.where(lo, pltpu.roll(p, IDX_DIM, 1), 0.0))

    for j, kr in enumerate(k):
        lanes = slice(j * LANES, (j + 1) * LANES)
        kb_ref[:, lanes] = kr.astype(BF16)
        for cols, t in blocks(kr):
            kt_ref[0, lanes, cols] = t
        for cols, t in blocks(v[:, lanes]):
            vt_ref[0, lanes, cols] = t
            vtb_ref[lanes, cols] = t.astype(BF16)
    kib_ref[...] = kw.astype(BF16)
    for cols, t in blocks(kw):
        kit_ref[0, :, cols] = t[:IDX_DIM, :]
        wt_out_ref[:, cols] = t[IDX_DIM:IDX_DIM + IDX_HEADS, :]

    first = (i % tiles_per_batch) == 0

    @pl.when(first)
    def _():
        uext_ref[0:16, :] = jnp.zeros((16, pool_width), F32)

    @pl.when(jnp.logical_not(first))
    def _():
        uext_ref[0:16, :] = uext_ref[tm:tm + 16, :]

    uext_ref[16:16 + tm, :] = u
    utail_ref[0] = u[tm - 16:tm, :]
    pos = (i % tiles_per_batch) * tm + lax.broadcasted_iota(I32, (tm, 1), 0)
    for g, w in enumerate(POOL_WINDOWS):
        lanes = slice(g * gw, (g + 1) * gw)
        acc = u[:, lanes]
        for jj in range(1, w):
            acc = acc + uext_ref[16 - jj:16 - jj + tm, lanes]
        cnt = jnp.minimum(w, pos + 1).astype(F32)
        d = acc / cnt - u[:, lanes]
        a_ref[:, lanes] = _pool_mix(d, g, wpool_ref, pscale_ref, gw).astype(BF16)


def _in_projection(x, wt_a, tabs, wpool, pscale, *, batch, seq):
    n_rows, d_model = x.shape
    tm = ROW_TILE
    pool_width = wpool.shape[0] * wpool.shape[1]
    nblk = n_rows // Q_TILE
    kv_w = N_KV_HEADS * HEAD_DIM
    tpb = seq // tm
    row = lambda width: pl.BlockSpec((tm, width), lambda i: (i, 0))
    col = lambda height: pl.BlockSpec((height, tm), lambda i: (0, i))
    per_seq = lambda height: pl.BlockSpec((1, height, tm), lambda i: (i // tpb, 0, i % tpb))
    tab = pl.BlockSpec((tm, LANES), lambda i: (i % tpb, 0))
    hm = pl.BlockSpec((tm // Q_TILE, LANES, N_HEADS * Q_TILE), lambda i: (i, 0, 0))
    body = functools.partial(_inproj_body, tm=tm, tiles_per_batch=tpb, pool_width=pool_width)
    return pl.pallas_call(
        body,
        grid=(n_rows // tm,),
        in_specs=[row(d_model), _const_spec(wt_a.shape), tab, tab, tab,
                  _const_spec(wpool.shape), _const_spec(pscale.shape)],
        out_specs=[pl.BlockSpec((1, 16, pool_width), lambda i: (i // tpb, 0, 0)),
                   row(pool_width), hm, hm, row(kv_w), row(LANES), per_seq(kv_w), per_seq(kv_w), col(kv_w),
                   per_seq(IDX_DIM), col(IDX_HEADS)],
        out_shape=[
            jax.ShapeDtypeStruct((batch, 16, pool_width), F32),
            jax.ShapeDtypeStruct((n_rows, pool_width), BF16),
            jax.ShapeDtypeStruct((nblk, LANES, N_HEADS * Q_TILE), BF16),
            jax.ShapeDtypeStruct((nblk, LANES, IDX_HEADS * Q_TILE), BF16),
            jax.ShapeDtypeStruct((n_rows, kv_w), BF16),
            jax.ShapeDtypeStruct((n_rows, LANES), BF16),
            jax.ShapeDtypeStruct((batch, kv_w, seq), F32),
            jax.ShapeDtypeStruct((batch, kv_w, seq), F32),
            jax.ShapeDtypeStruct((kv_w, n_rows), BF16),
            jax.ShapeDtypeStruct((batch, IDX_DIM, seq), F32),
            jax.ShapeDtypeStruct((IDX_HEADS, n_rows), F32),
        ],
        scratch_shapes=[pltpu.VMEM((tm + 16, pool_width), F32)],
        compiler_params=pltpu.CompilerParams(dimension_semantics=("arbitrary",), vmem_limit_bytes=VMEM_LIMIT),
        name="in_projection",
    )(x, wt_a, *tabs, wpool, pscale)


def _inproj_sample_body(x_ref, wt_ref, tc_ref, ts1_ref, ts2_ref, wpool_ref, pscale_ref, state_ref,
                        u_ref, a_ref, q_ref, qi_ref, k_ref, v_ref, kiwi_ref, *, pool_width, sample_pos):
    u, q, qi, k, v, kw = _project(x_ref[...].astype(BF16), wt_ref, tc_ref[...], ts1_ref[...], ts2_ref[...],
                                  pool_width)
    gw = pool_width // len(POOL_WINDOWS)
    u_ref[...] = u
    for j, t in enumerate(q):
        q_ref[:, j * LANES:(j + 1) * LANES] = t.astype(BF16)
    for j, t in enumerate(qi):
        qi_ref[:, j * LANES:(j + 1) * LANES] = t.astype(BF16)
    for j, t in enumerate(k):
        k_ref[:, j * LANES:(j + 1) * LANES] = t
    v_ref[...] = v
    kiwi_ref[...] = kw
    for g, w in enumerate(POOL_WINDOWS):
        lanes = slice(g * gw, (g + 1) * gw)
        acc = u[:, lanes]
        for jj in range(1, w):
            acc = acc + state_ref[POOL_STATE - jj, :, lanes]
        d = acc / float(min(w, sample_pos + 1)) - u[:, lanes]
        a_ref[:, lanes] = _pool_mix(d, g, wpool_ref, pscale_ref, gw).astype(BF16)


def _in_projection_sample(x, wt_a, tabs, wpool, pscale, state_t, *, sample_pos):
    n_rows, d_model = x.shape
    pool_width = wpool.shape[0] * wpool.shape[1]
    kv_w = N_KV_HEADS * HEAD_DIM
    full = lambda shape: pl.BlockSpec(shape, lambda i: (0,) * len(shape))
    body = functools.partial(_inproj_sample_body, pool_width=pool_width, sample_pos=sample_pos)
    out_shape = [
        jax.ShapeDtypeStruct((n_rows, pool_width), F32),
        jax.ShapeDtypeStruct((n_rows, pool_width), BF16),
        jax.ShapeDtypeStruct((n_rows, N_HEADS * HEAD_DIM), BF16),
        jax.ShapeDtypeStruct((n_rows, IDX_HEADS * IDX_DIM), BF16),
        jax.ShapeDtypeStruct((n_rows, kv_w), F32),
        jax.ShapeDtypeStruct((n_rows, kv_w), F32),
        jax.ShapeDtypeStruct((n_rows, LANES), F32),
    ]
    return pl.pallas_call(
        body,
        grid=(1,),
        in_specs=[full(x.shape), _const_spec(wt_a.shape), full(tabs[0].shape), full(tabs[1].shape),
                  full(tabs[2].shape), _const_spec(wpool.shape), _const_spec(pscale.shape), full(state_t.shape)],
        out_specs=[full(s.shape) for s in out_shape],
        out_shape=out_shape,
        compiler_params=pltpu.CompilerParams(dimension_semantics=("arbitrary",), vmem_limit_bytes=VMEM_LIMIT),
        name="in_projection_sample",
    )(x, wt_a, *tabs, wpool, pscale, state_t)


def _rope_tables(positions):
    pos = np.asarray(positions, np.float32)
    n = pos.shape[0]
    half = ROT_DIM // 2
    inv = np.exp(-np.arange(half, dtype=np.float32) * np.float32(math.log(ROPE_THETA) / half)).astype(np.float32)
    ang = (pos[:, None] * inv[None, :]).astype(np.float32).astype(np.float64)
    cos, sin = np.cos(ang).astype(np.float32), np.sin(ang).astype(np.float32)
    zeros = np.zeros((n, half), np.float32)
    rest = HEAD_DIM - ROT_DIM
    c = np.concatenate([cos, cos, np.ones((n, rest), np.float32)], axis=1)
    s1 = np.concatenate([-sin, zeros, np.zeros((n, rest), np.float32)], axis=1)
    s2 = np.concatenate([zeros, sin, np.zeros((n, rest), np.float32)], axis=1)
    return tuple(jnp.asarray(np.concatenate([t, t], axis=1)) for t in (c, s1, s2))


def _topk_threshold(count_ge, init, topk):
    def step(it, t):
        cand = t + jnp.left_shift(jnp.int32(1), 31 - it)
        return jnp.where(count_ge(cand) >= topk, cand, t)
    return lax.fori_loop(0, 32, step, init)


def _fold_slabs(x, op):
    assert x.shape[0] & (x.shape[0] - 1) == 0
    while x.shape[0] > 1:
        half = x.shape[0] // 2
        x = op(x[:half], x[half:])
    return x[0]


def _prompt_attn_body(qit_ref, wt_ref, qt_ref, kib_ref, kb_ref, vt_ref, o_ref,
                      sc_ref, thr_ref, m_ref, l_ref, acc_ref, *, topk):
    i = pl.program_id(1)
    tq, kc = Q_TILE, KEY_CHUNK
    n_keys = (i + 1) * tq
    n_chunks = (n_keys + kc - 1) // kc
    qpos = i * tq + lax.broadcasted_iota(I32, (1, tq), 1)
    group = N_HEADS // N_KV_HEADS
    gcols = group * tq

    def score_chunk(c, carry):
        k0 = pl.multiple_of(c * kc, kc)
        kchunk = kib_ref[pl.ds(k0, kc), :]
        acc = jnp.zeros((kc, tq), F32)
        for hg in range(IDX_HEADS // group):
            d = jnp.dot(kchunk, qit_ref[0, :, hg * gcols:(hg + 1) * gcols], preferred_element_type=F32)
            for r in range(group):
                h = hg * group + r
                acc = acc + jnp.maximum(d[:, r * tq:(r + 1) * tq], 0.0) * wt_ref[h:h + 1, :]
        kpos = k0 + lax.broadcasted_iota(I32, (kc, 1), 0)
        sc_ref[pl.ds(k0, kc), :] = _sortable_key(jnp.where(kpos <= qpos, acc, -jnp.inf))
        return carry

    lax.fori_loop(0, n_chunks, score_chunk, 0)

    thr_ref[...] = jnp.full((1, tq), NEG_INF_KEY + 1, I32)

    @pl.when(n_keys > topk)
    def _():
        def count_ge(cand):
            candb = jnp.broadcast_to(cand, (SUBLANES, tq))

            def blk(c, acc):
                k0 = pl.multiple_of(c * kc, kc)
                hit = jnp.where(sc_ref[pl.ds(k0, kc), :].reshape(kc // SUBLANES, SUBLANES, tq) >= candb, 1.0, 0.0)
                return acc + _fold_slabs(hit, jnp.add)
            acc = lax.fori_loop(0, n_chunks, blk, jnp.zeros((SUBLANES, tq), F32))
            return jnp.sum(acc, axis=0, keepdims=True)

        t = _topk_threshold(count_ge, jnp.full((1, tq), INT_MIN, I32), float(topk))
        thr_ref[...] = jnp.where(qpos + 1 > topk, t, NEG_INF_KEY + 1)

    m_ref[...] = jnp.full(m_ref.shape, -jnp.inf, F32)
    l_ref[...] = jnp.zeros(l_ref.shape, F32)
    acc_ref[...] = jnp.zeros(acc_ref.shape, F32)

    def attn_chunk(c, carry):
        k0 = pl.multiple_of(c * kc, kc)
        sel = sc_ref[pl.ds(k0, kc), :] >= thr_ref[...]
        for g in range(N_KV_HEADS):
            pair = slice((g // 2) * LANES, (g // 2 + 1) * LANES)
            cols = slice(g * gcols, (g + 1) * gcols)
            s = jnp.dot(kb_ref[pl.ds(k0, kc), pair], qt_ref[0, :, cols], preferred_element_type=F32)
            s = jnp.concatenate([jnp.where(sel, s[:, r * tq:(r + 1) * tq], -jnp.inf) for r in range(group)], axis=1)
            m_old = m_ref[:, cols]
            m_new = jnp.maximum(m_old, jnp.max(s, axis=0, keepdims=True))
            m_safe = jnp.where(m_new == -jnp.inf, 0.0, m_new)
            alpha = jnp.exp(m_old - m_safe)
            p = jnp.exp(s - m_safe)
            l_ref[:, cols] = alpha * l_ref[:, cols] + jnp.sum(p, axis=0, keepdims=True)
            pv = jnp.dot(vt_ref[pair, pl.ds(k0, kc)], p.astype(BF16), preferred_element_type=F32)
            acc_ref[g] = alpha * acc_ref[g] + pv
            m_ref[:, cols] = m_new
        return carry

    lax.fori_loop(0, n_chunks, attn_chunk, 0)
    l_all = l_ref[...]

    for j in range(N_HEADS // 2):
        halves = []
        for h in (2 * j, 2 * j + 1):
            g, r = h // group, h % group
            rows = slice((g % 2) * HEAD_DIM, (g % 2 + 1) * HEAD_DIM)
            cols = slice(r * tq, (r + 1) * tq)
            halves.append(acc_ref[g, rows, cols] / l_all[:, g * gcols + r * tq:g * gcols + (r + 1) * tq])
        o_ref[:, j * LANES:(j + 1) * LANES] = jnp.concatenate(halves, axis=0).T.astype(BF16)


def _prompt_attention(qit, wt, qt, kib, kb, vt, *, batch, seq):
    nq = seq // Q_TILE
    topk = min(TOPK_MAX, seq // 4)
    kv_w = N_KV_HEADS * HEAD_DIM
    per_q = lambda b, i: (b * nq + i, 0, 0)
    body = functools.partial(_prompt_attn_body, topk=topk)
    return pl.pallas_call(
        body,
        grid=(batch, nq),
        in_specs=[
            pl.BlockSpec((1, LANES, IDX_HEADS * Q_TILE), per_q),
            pl.BlockSpec((IDX_HEADS, Q_TILE), lambda b, i: (0, b * nq + i)),
            pl.BlockSpec((1, LANES, N_HEADS * Q_TILE), per_q),
            pl.BlockSpec((seq, LANES), lambda b, i: (b, 0)),
            pl.BlockSpec((seq, kv_w), lambda b, i: (b, 0)),
            pl.BlockSpec((kv_w, seq), lambda b, i: (0, b)),
        ],
        out_specs=pl.BlockSpec((Q_TILE, N_HEADS * HEAD_DIM), lambda b, i: (b * nq + i, 0)),
        out_shape=jax.ShapeDtypeStruct((batch * seq, N_HEADS * HEAD_DIM), BF16),
        scratch_shapes=[
            pltpu.VMEM((seq, Q_TILE), I32),
            pltpu.VMEM((1, Q_TILE), I32),
            pltpu.VMEM((1, N_HEADS * Q_TILE), F32),
            pltpu.VMEM((1, N_HEADS * Q_TILE), F32),
            pltpu.VMEM((N_KV_HEADS, LANES, (N_HEADS // N_KV_HEADS) * Q_TILE), F32),
        ],
        compiler_params=pltpu.CompilerParams(dimension_semantics=("arbitrary", "arbitrary"),
                                             vmem_limit_bytes=VMEM_LIMIT),
        name="prompt_attention",
    )(qit, wt, qt, kib, kb, vt)


SAMPLE_PAGES_PER_CHUNK = 16


def _sample_attn_body(pt_ref, qi_ref, w_ref, qs_ref, kicur_ref, kcur_ref, vcur_ref, ckidx_hbm, ck_hbm, cv_hbm,
                      o_ref, kidx_buf, k_buf, v_buf, sc_ref, sem_idx, sem_k, sem_v, *, n_pages, topk):
    b = pl.program_id(0)
    last = pl.num_programs(0) - 1
    ppc = SAMPLE_PAGES_PER_CHUNK
    n_chunks = n_pages // ppc
    ck = ppc * PAGE_SIZE
    seg = sc_ref.shape[1]
    segs_per_chunk = ck // seg

    def idx_copy(seq, p):
        return pltpu.make_async_copy(ckidx_hbm.at[pt_ref[seq, p]],
                                     kidx_buf.at[seq % 2, :, pl.ds(p * PAGE_SIZE, PAGE_SIZE)], sem_idx.at[seq % 2])

    def kv_copies(seq, c, p):
        slot = c % 2
        page = pt_ref[seq, c * ppc + p]
        window = pl.ds(p * PAGE_SIZE, PAGE_SIZE)
        return (pltpu.make_async_copy(ck_hbm.at[page], k_buf.at[slot, :, window], sem_k.at[slot]),
                pltpu.make_async_copy(cv_hbm.at[page], v_buf.at[slot, :, window], sem_v.at[slot]))

    def start_idx(seq):
        for p in range(n_pages):
            idx_copy(seq, p).start()

    def start_kv(seq, c):
        for p in range(ppc):
            for cp in kv_copies(seq, c, p):
                cp.start()

    def wait_kv(seq, c):
        for p in range(ppc):
            for cp in kv_copies(seq, c, p):
                cp.wait()

    @pl.when(b == 0)
    def _():
        start_idx(b)
        start_kv(b, 0)

    for p in range(n_pages):
        idx_copy(b, p).wait()

    qi = qi_ref[0][:, :IDX_DIM]
    w = w_ref[0]

    def head_mix(d):
        return jnp.sum(jnp.maximum(d, 0.0) * w, axis=0, keepdims=True)

    def bf16_products(a, row):
        return jnp.sum(a.astype(F32) * row.astype(BF16).astype(F32), axis=1, keepdims=True)

    for c in range(n_chunks):
        d = jnp.dot(qi, kidx_buf[b % 2, :, c * ck:(c + 1) * ck].astype(BF16), preferred_element_type=F32)
        keys = _sortable_key(head_mix(d))
        for r in range(segs_per_chunk):
            sc_ref[c * segs_per_chunk + r:c * segs_per_chunk + r + 1, :] = keys[:, r * seg:(r + 1) * seg]
    cur_key = _sortable_key(head_mix(bf16_products(qi, kicur_ref[0][:, :IDX_DIM])))

    vreg = (SUBLANES, LANES)
    if n_pages * PAGE_SIZE + 1 > topk:
        cur_key_v = jnp.broadcast_to(cur_key, vreg)

        def count_ge(cand):
            hits = jnp.where(cur_key_v >= cand, 1.0 / (SUBLANES * LANES), 0.0)
            for j in range(seg // LANES):
                hits = hits + jnp.where(sc_ref[:, j * LANES:(j + 1) * LANES] >= cand, 1.0, 0.0)
            total = jnp.sum(jnp.sum(hits, axis=1, keepdims=True), axis=0, keepdims=True)
            return jnp.broadcast_to(total, vreg)
        thr = _topk_threshold(count_ge, jnp.full(vreg, INT_MIN, I32), float(topk))[0:1, 0:1]
    else:
        thr = jnp.full((1, 1), NEG_INF_KEY + 1, I32)

    qs = qs_ref[0]
    m = jnp.full((N_HEADS, 1), -jnp.inf, F32)
    l = jnp.zeros((N_HEADS, 1), F32)
    acc = jnp.zeros((N_HEADS, N_KV_HEADS * HEAD_DIM), F32)

    def update(m, l, acc, s, sel, pv):
        s = jnp.where(sel, s, -jnp.inf)
        m_new = jnp.maximum(m, jnp.max(s, axis=1, keepdims=True))
        m_safe = jnp.where(m_new == -jnp.inf, 0.0, m_new)
        alpha = jnp.exp(m - m_safe)
        p = jnp.exp(s - m_safe)
        l = alpha * l + jnp.sum(p, axis=1, keepdims=True)
        acc = alpha * acc + pv(p.astype(BF16))
        return m_new, l, acc

    def prefetch_next_sequence():
        @pl.when(b < last)
        def _():
            start_idx(b + 1)
            start_kv(b + 1, 0)

    slot0_free_early = n_chunks % 2 == 0
    for c in range(n_chunks):
        if c + 1 < n_chunks:
            start_kv(b, c + 1)
        elif slot0_free_early:
            prefetch_next_sequence()
        wait_kv(b, c)
        slot = c % 2
        s = jnp.dot(qs, k_buf[slot].astype(BF16), preferred_element_type=F32)
        vt = v_buf[slot].astype(BF16)
        sel = jnp.concatenate([sc_ref[c * segs_per_chunk + r:c * segs_per_chunk + r + 1, :]
                               for r in range(segs_per_chunk)], axis=1) >= thr
        m, l, acc = update(m, l, acc, s, sel, lambda p: lax.dot_general(p, vt, NT_DIMS, preferred_element_type=F32))
    if not slot0_free_early:
        prefetch_next_sequence()
    v_cur = vcur_ref[0].astype(BF16).astype(F32)
    m, l, acc = update(m, l, acc, bf16_products(qs, kcur_ref[0]), cur_key >= thr,
                       lambda p: p.astype(F32) * v_cur)
    o_ref[0] = acc / l


def _keys_minor(cache):
    pages, page_size = cache.shape[:2]
    return jnp.moveaxis(cache, 1, -1).reshape(pages, -1, page_size)


def _sample_attention(page_table, qi_s, w_s, qs, kicur, kcur, vcur, cache_kidx, cache_k, cache_v):
    dec_batch, n_pages = page_table.shape
    past = n_pages * PAGE_SIZE
    topk = min(TOPK_MAX, (past + 1) // 4)
    kv_w = N_KV_HEADS * HEAD_DIM
    ppc = SAMPLE_PAGES_PER_CHUNK
    per_b = lambda b, pt: (b, 0, 0)
    any_spec = pl.BlockSpec(memory_space=pl.ANY)
    grid_spec = pltpu.PrefetchScalarGridSpec(
        num_scalar_prefetch=1,
        grid=(dec_batch,),
        in_specs=[
            pl.BlockSpec((1, IDX_HEADS, LANES), per_b),
            pl.BlockSpec((1, IDX_HEADS, 1), per_b),
            pl.BlockSpec((1, N_HEADS, kv_w), per_b),
            pl.BlockSpec((1, 1, LANES), per_b),
            pl.BlockSpec((1, 1, kv_w), per_b),
            pl.BlockSpec((1, 1, kv_w), per_b),
            any_spec, any_spec, any_spec,
        ],
        out_specs=pl.BlockSpec((1, N_HEADS, kv_w), per_b),
        scratch_shapes=[
            pltpu.VMEM((2, IDX_DIM, past), F32),
            pltpu.VMEM((2, kv_w, ppc * PAGE_SIZE), F32),
            pltpu.VMEM((2, kv_w, ppc * PAGE_SIZE), F32),
            pltpu.VMEM((SUBLANES, past // SUBLANES), I32),
            pltpu.SemaphoreType.DMA((2,)),
            pltpu.SemaphoreType.DMA((2,)),
            pltpu.SemaphoreType.DMA((2,)),
        ],
    )
    body = functools.partial(_sample_attn_body, n_pages=n_pages, topk=topk)
    return pl.pallas_call(
        body,
        grid_spec=grid_spec,
        out_shape=jax.ShapeDtypeStruct((dec_batch, N_HEADS, kv_w), F32),
        compiler_params=pltpu.CompilerParams(dimension_semantics=("arbitrary",), vmem_limit_bytes=VMEM_LIMIT),
        name="sample_attention",
    )(page_table, qi_s, w_s, qs, kicur, kcur, vcur, cache_kidx, cache_k, cache_v)


def _mm(a, b, precise, *, nt=False):
    dims = NT_DIMS if nt else (((1,), (0,)), ((), ()))
    if precise:
        return lax.dot_general(a.astype(F32), b, dims, precision=lax.Precision.HIGHEST, preferred_element_type=F32)
    return lax.dot_general(a.astype(BF16), b, dims, preferred_element_type=F32)


def _operand_dtype(precise):
    return F32 if precise else BF16


MERGE_COL_CHUNK = 512


def _merge_body(x_ref, a_ref, o_ref, wgat_ref, wgbt_ref, wba_ref, wbb_ref, out_ref, *, precise):
    x, a, o = x_ref[...], a_ref[...], o_ref[...]
    cc = MERGE_COL_CHUNK
    for c in range(out_ref.shape[1] // cc):
        cols = slice(c * cc, (c + 1) * cc)
        ga = _mm(x, wgat_ref[cols, :], precise, nt=True)
        gb = _mm(x, wgbt_ref[cols, :], precise, nt=True)
        ya = _mm(a, wba_ref[:, cols], precise)
        yo = _mm(o, wbb_ref[:, cols], precise)
        out_ref[:, cols] = (jax.nn.sigmoid(ga) * ya + jax.nn.sigmoid(gb) * yo).astype(out_ref.dtype)


def _merge(x, a, o, wgat, wgbt, wba, wbb, *, tm, col_block, precise):
    n_rows, d_model = x.shape
    mode = dict(pipeline_mode=pl.Buffered(1)) if col_block == d_model else {}
    gate_spec = pl.BlockSpec((col_block, d_model), lambda i, j: (j, 0), **mode)
    branch_spec = lambda w: pl.BlockSpec((w.shape[0], col_block), lambda i, j: (0, j), **mode)
    row = lambda width: pl.BlockSpec((tm, width), lambda i, j: (i, 0))
    body = functools.partial(_merge_body, precise=precise)
    return pl.pallas_call(
        body,
        grid=(n_rows // tm, d_model // col_block),
        in_specs=[row(d_model), row(a.shape[1]), row(o.shape[1]), gate_spec, gate_spec,
                  branch_spec(wba), branch_spec(wbb)],
        out_specs=pl.BlockSpec((tm, col_block), lambda i, j: (i, j)),
        out_shape=jax.ShapeDtypeStruct((n_rows, d_model), _operand_dtype(precise)),
        compiler_params=pltpu.CompilerParams(dimension_semantics=("arbitrary", "arbitrary"),
                                             vmem_limit_bytes=VMEM_LIMIT),
        name="branch_merge",
    )(x, a, o, wgat, wgbt, wba, wbb)


def _layer_norm(x, g, b):
    mu = jnp.mean(x, axis=-1, keepdims=True)
    var = jnp.mean(jnp.square(x - mu), axis=-1, keepdims=True)
    return (x - mu) * lax.rsqrt(var + LN_EPS) * g + b


def _route(logits):
    lane = lax.broadcasted_iota(I32, logits.shape, 1)
    lane_f = lane.astype(F32)
    big = float(LANES)

    def masked_softmax(mask):
        x = jnp.where(mask, logits, -jnp.inf)
        e = jnp.exp(x - jnp.max(x, axis=1, keepdims=True))
        return jnp.where(mask, e / jnp.sum(e, axis=1, keepdims=True), -1.0)

    def top1(p):
        best = jnp.max(p, axis=1, keepdims=True)
        idx = jnp.min(jnp.where(p == best, lane_f, big), axis=1, keepdims=True)
        return best, idx

    gp = masked_softmax((lane >= N_EXPERTS) & (lane < N_EXPERTS + N_GROUPS))
    g_p, g_lane = top1(gp)
    g_idx = g_lane.astype(I32) - N_EXPERTS
    ep = masked_softmax((lane < N_EXPERTS) & ((lane >> int(math.log2(EXPERTS_PER_GROUP))) == g_idx))
    p1, i1 = top1(ep)
    p2, i2 = top1(jnp.where(lane_f == i1, -1.0, ep))
    denom = p1 + p2
    return jnp.where(lane_f == i1, g_p * p1 / denom, 0.0) + jnp.where(lane_f == i2, g_p * p2 / denom, 0.0)


def _outproj_body(m_ref, x_ref, wout_ref, g_ref, b_ref, wr_ref, h_ref, dense_ref, *, precise, n_real):
    i = pl.program_id(0)

    @pl.when(i < n_real)
    def _():
        mo = _mm(m_ref[...], wout_ref[...], precise)
        h = _layer_norm(ALPHA * x_ref[...] + mo, g_ref[...], b_ref[...])
        h_ref[...] = h
        dense_ref[...] = _route(_mm(h, wr_ref[...], precise))

    @pl.when(i >= n_real)
    def _():
        h_ref[...] = jnp.zeros(h_ref.shape, F32)
        dense_ref[...] = jnp.zeros(dense_ref.shape, F32)


def _out_projection(merged, x, wout, ln_g, ln_b, w_route, *, tm, precise, zero_tiles=0):
    n_rows, d_model = x.shape
    n_real = n_rows // tm
    body = functools.partial(_outproj_body, precise=precise, n_real=n_real)
    in_row = lambda width: pl.BlockSpec((tm, width), lambda i: (jnp.minimum(i, n_real - 1), 0))
    out_row = lambda width: pl.BlockSpec((tm, width), lambda i: (i, 0))
    n_out = n_rows + zero_tiles * tm
    return pl.pallas_call(
        body,
        grid=(n_real + zero_tiles,),
        in_specs=[in_row(d_model), in_row(d_model), _const_spec(wout.shape),
                  _const_spec(ln_g.shape), _const_spec(ln_b.shape), _const_spec(w_route.shape)],
        out_specs=[out_row(d_model), out_row(LANES)],
        out_shape=[jax.ShapeDtypeStruct((n_out, d_model), F32),
                   jax.ShapeDtypeStruct((n_out, LANES), F32)],
        compiler_params=pltpu.CompilerParams(dimension_semantics=("arbitrary",), vmem_limit_bytes=VMEM_LIMIT),
        name="out_projection",
    )(merged, x, wout, ln_g, ln_b, w_route)


EXPERT_ROW_TILE = 256
TOP_K_EXPERTS = 2


def _route_plan(dense, tm):
    n_tok = dense.shape[0]
    n_asg = TOP_K_EXPERTS * n_tok
    n_tiles = (n_asg + N_EXPERTS * (tm - 1)) // tm + 1
    w2, e2 = lax.top_k(dense[:, :N_EXPERTS], TOP_K_EXPERTS)
    eflat = e2.reshape(n_asg).astype(I32)
    order = jnp.argsort(eflat, stable=True).astype(I32)
    e_sorted = eflat[order]
    starts = jnp.searchsorted(e_sorted, jnp.arange(N_EXPERTS + 1, dtype=I32), side="left").astype(I32)
    counts = starts[1:] - starts[:-1]
    tiles_e = (counts + tm - 1) // tm
    tile_end = jnp.cumsum(tiles_e).astype(I32)
    tile_start = tile_end - tiles_e
    tile = jnp.arange(n_tiles, dtype=I32)
    tile_expert = jnp.minimum(jnp.searchsorted(tile_end, tile, side="right"), N_EXPERTS - 1).astype(I32)
    tile_used = (tile < tile_end[-1]).astype(I32)
    row = jnp.arange(n_tiles * tm, dtype=I32)
    te = tile_expert[row // tm]
    q = row - tile_start[te] * tm
    valid = (q < counts[te]) & (row // tm < tile_end[-1])
    a = order[jnp.clip(starts[te] + q, 0, n_asg - 1)]
    src = jnp.where(valid, a // TOP_K_EXPERTS, 0)
    slot = a % TOP_K_EXPERTS
    weight = jnp.where(valid, w2.reshape(n_asg)[a], 0.0)
    return src, slot, valid, weight, tile_expert, tile_used


def _expert_body(te_ref, used_ref, src_ref, dst_ref, cw_ref, wg_ref, wu_ref, wd_ref, h_hbm, y_hbm,
                 x_buf, y_buf, sem_in, sem_out, *, tm):
    j = pl.program_id(0)

    def gather(r):
        return pltpu.make_async_copy(h_hbm.at[pl.ds(src_ref[0, 0, r], 1), :], x_buf.at[pl.ds(r, 1), :], sem_in.at[0])

    def scatter(r):
        return pltpu.make_async_copy(y_buf.at[pl.ds(r, 1), :], y_hbm.at[pl.ds(dst_ref[0, 0, r], 1), :], sem_out.at[0])

    @pl.when(j == 0)
    def _():
        y_buf[...] = jnp.zeros(y_buf.shape, F32)
        pad_rows = pltpu.make_async_copy(y_buf, y_hbm.at[pl.ds(y_hbm.shape[0] - tm, tm), :], sem_out.at[0])
        pad_rows.start()
        pad_rows.wait()

    @pl.when(used_ref[j] > 0)
    def _():
        for r in range(tm):
            gather(r).start()
        for r in range(tm):
            gather(r).wait()
        xb = x_buf[...].astype(BF16)
        gate = jnp.dot(xb, wg_ref[0].astype(BF16), preferred_element_type=F32)
        up = jnp.dot(xb, wu_ref[0].astype(BF16), preferred_element_type=F32)
        hh = (gate * jax.nn.sigmoid(gate)) * up * cw_ref[0]
        y_buf[...] = jnp.dot(hh.astype(BF16), wd_ref[0].astype(BF16), preferred_element_type=F32)
        for r in range(tm):
            scatter(r).start()
        for r in range(tm):
            scatter(r).wait()


def _experts(h_all, plan, wg, wu, wd, *, tm, slot_stride):
    src, slot, valid, weight, tile_expert, tile_used = plan
    d_model = h_all.shape[1]
    n_experts, _, d_expert = wg.shape
    n_tiles = tile_expert.shape[0]
    pad_row = 2 * slot_stride + (jnp.arange(n_tiles * tm, dtype=I32) % tm)
    dst = jnp.where(valid, slot * slot_stride + src, pad_row)
    per_tile = lambda a: a.reshape(n_tiles, 1, tm)
    smem_rows = pl.BlockSpec((1, 1, tm), lambda j, te, used: (j, 0, 0), memory_space=pltpu.SMEM)
    any_spec = pl.BlockSpec(memory_space=pl.ANY)
    grid_spec = pltpu.PrefetchScalarGridSpec(
        num_scalar_prefetch=2,
        grid=(n_tiles,),
        in_specs=[smem_rows, smem_rows,
                  pl.BlockSpec((1, tm, 1), lambda j, te, used: (j, 0, 0)),
                  pl.BlockSpec((1, d_model, d_expert), lambda j, te, used: (te[j], 0, 0)),
                  pl.BlockSpec((1, d_model, d_expert), lambda j, te, used: (te[j], 0, 0)),
                  pl.BlockSpec((1, d_expert, d_model), lambda j, te, used: (te[j], 0, 0)),
                  any_spec],
        out_specs=any_spec,
        scratch_shapes=[pltpu.VMEM((tm, d_model), F32), pltpu.VMEM((tm, d_model), F32),
                        pltpu.SemaphoreType.DMA((1,)), pltpu.SemaphoreType.DMA((1,))],
    )
    return pl.pallas_call(
        functools.partial(_expert_body, tm=tm),
        grid_spec=grid_spec,
        out_shape=jax.ShapeDtypeStruct((2 * slot_stride + tm, d_model), F32),
        compiler_params=pltpu.CompilerParams(dimension_semantics=("arbitrary",), vmem_limit_bytes=VMEM_LIMIT),
        name="expert_mlp",
    )(tile_expert, tile_used, per_tile(src), per_tile(dst), weight.reshape(n_tiles, tm, 1), wg, wu, wd, h_all)


def _combine_body(y0_ref, y1_ref, h_ref, g_ref, b_ref, out_ref):
    out_ref[...] = _layer_norm(ALPHA * h_ref[...] + (y0_ref[...] + y1_ref[...]), g_ref[...], b_ref[...])


def _combine(y, h_all, ln_g, ln_b, *, tm, tile0, n_tiles, slot_stride):
    d_model = h_all.shape[1]
    slot = lambda k: pl.BlockSpec((tm, d_model), lambda i: (i + tile0 + k * (slot_stride // tm), 0))
    return pl.pallas_call(
        _combine_body,
        grid=(n_tiles,),
        in_specs=[slot(0), slot(1), slot(0), _const_spec(ln_g.shape), _const_spec(ln_b.shape)],
        out_specs=pl.BlockSpec((tm, d_model), lambda i: (i, 0)),
        out_shape=jax.ShapeDtypeStruct((n_tiles * tm, d_model), F32),
        compiler_params=pltpu.CompilerParams(dimension_semantics=("arbitrary",), vmem_limit_bytes=VMEM_LIMIT),
        name="expert_combine",
    )(y, y, h_all, ln_g, ln_b)


def _layer(x_prompt, x_sample, cache_k, cache_v, cache_kidx, state_pool, page_table, w_in, w_pool, pool_scale,
           w_branch_a, w_branch_b, w_out, ln1_g, ln1_b, w_group, w_expert_router, w_gate, w_up, w_down,
           ln2_g, ln2_b):
    batch, seq, d_model = x_prompt.shape
    dec_batch = x_sample.shape[0]
    n_pages = page_table.shape[1]
    past = n_pages * PAGE_SIZE
    n_prompt = batch * seq
    kv_w = N_KV_HEADS * HEAD_DIM
    attn_w = N_HEADS * HEAD_DIM
    pool_width = w_pool.shape[0] * w_pool.shape[1]
    group = N_HEADS // N_KV_HEADS
    assert x_sample.shape[1] == 1 and seq % KEY_CHUNK == 0 and seq % MOE_ROW_TILE == 0
    assert dec_batch % 16 == 0 and n_pages % SAMPLE_PAGES_PER_CHUNK == 0
    assert (SAMPLE_PAGES_PER_CHUNK * PAGE_SIZE) % (past // SUBLANES) == 0

    x_p = x_prompt.reshape(n_prompt, d_model)
    x_s = x_sample.reshape(dec_batch, d_model)

    w_t = jnp.transpose(w_in)
    a_width = w_in.shape[1] - 2 * d_model
    wt_a = jnp.concatenate([w_t[:a_width], jnp.zeros((-a_width % LANES, d_model), F32)], axis=0).astype(BF16)
    wgat, wgbt = w_t[a_width:a_width + d_model], w_t[a_width + d_model:]
    wpool_b, pscale = w_pool.astype(BF16), pool_scale.reshape(1, pool_width)

    utail, a_p, qt, qit, kb, kib, kt, vt, vtb, kit, wt = _in_projection(
        x_p, wt_a, _rope_tables(np.arange(seq)), wpool_b, pscale, batch=batch, seq=seq)
    o_prompt = _prompt_attention(qit, wt, qt, kib, kb, vtb, batch=batch, seq=seq)

    u_s, a_s, q_s, qi_s, k_s, v_s, kiwi_s = _in_projection_sample(
        x_s, wt_a, _rope_tables(np.full((dec_batch,), past)), wpool_b, pscale, jnp.transpose(state_pool, (1, 0, 2)),
        sample_pos=past)
    qi_h = qi_s.reshape(dec_batch, IDX_HEADS, IDX_DIM)
    qi_h = jnp.concatenate([qi_h, jnp.zeros_like(qi_h)], axis=-1)
    in_group = (jnp.arange(N_HEADS)[:, None] // group == jnp.arange(N_KV_HEADS)[None, :])[None, :, :, None]
    qs = jnp.where(in_group, q_s.reshape(dec_batch, N_HEADS, 1, HEAD_DIM), 0).reshape(dec_batch, N_HEADS, kv_w)
    w_s = kiwi_s[:, IDX_DIM:IDX_DIM + IDX_HEADS].reshape(dec_batch, IDX_HEADS, 1)
    o_s = _sample_attention(page_table, qi_h, w_s, qs, kiwi_s.reshape(dec_batch, 1, LANES),
                            k_s.reshape(dec_batch, 1, kv_w), v_s.reshape(dec_batch, 1, kv_w),
                            _keys_minor(cache_kidx), _keys_minor(cache_k), _keys_minor(cache_v))
    o_s = o_s.reshape(dec_batch, N_KV_HEADS, group, N_KV_HEADS, HEAD_DIM)
    o_s = jnp.transpose(jnp.diagonal(o_s, axis1=1, axis2=3), (0, 3, 1, 2)).reshape(dec_batch, attn_w)
    o_sample = o_s.astype(BF16)

    w_route = jnp.concatenate([w_expert_router, w_group,
                               jnp.zeros((d_model, LANES - N_EXPERTS - N_GROUPS), F32)], axis=1)
    g1, b1 = ln1_g.reshape(1, d_model), ln1_b.reshape(1, d_model)
    g2, b2 = ln2_g.reshape(1, d_model), ln2_b.reshape(1, d_model)

    merged_p = _merge(x_p, a_p, o_prompt, wgat.astype(BF16), wgbt.astype(BF16), w_branch_a.astype(BF16),
                      w_branch_b.astype(BF16), tm=ROW_TILE, col_block=d_model, precise=False)
    h_all, dense_all = _out_projection(merged_p, x_p, w_out.astype(BF16), g1, b1, w_route.astype(BF16),
                                       tm=ROW_TILE, precise=False, zero_tiles=1)
    merged_s = _merge(x_s, a_s, o_sample, wgat, wgbt, w_branch_a, w_branch_b, tm=dec_batch,
                      col_block=MERGE_COL_CHUNK, precise=True)
    h_s, dense_s = _out_projection(merged_s, x_s, w_out, g1, b1, w_route, tm=dec_batch, precise=True)

    h_all = lax.dynamic_update_slice(h_all, h_s, (n_prompt, 0))
    dense_all = lax.dynamic_update_slice(dense_all, dense_s, (n_prompt, 0))
    slot_stride = h_all.shape[0]
    plan = _route_plan(dense_all, EXPERT_ROW_TILE)
    y2 = _experts(h_all, plan, w_gate, w_up, w_down, tm=EXPERT_ROW_TILE, slot_stride=slot_stride)
    y_prompt = _combine(y2, h_all, g2, b2, tm=ROW_TILE, tile0=0, n_tiles=n_prompt // ROW_TILE,
                        slot_stride=slot_stride)
    y_sample = _combine(y2, h_all, g2, b2, tm=dec_batch, tile0=n_prompt // dec_batch, n_tiles=1,
                        slot_stride=slot_stride)

    heads_t = lambda t: jnp.transpose(t.reshape(batch, N_KV_HEADS, HEAD_DIM, seq), (0, 3, 1, 2))
    pool_sample = jnp.concatenate([state_pool[:, 1:], u_s[:, None, :]], axis=1)
    return (y_prompt.reshape(batch, seq, d_model), y_sample.reshape(dec_batch, 1, d_model),
            heads_t(kt), heads_t(vt), jnp.transpose(kit, (0, 2, 1)), utail[:, 16 - POOL_STATE:],
            k_s.reshape(dec_batch, 1, N_KV_HEADS, HEAD_DIM), v_s.reshape(dec_batch, 1, N_KV_HEADS, HEAD_DIM),
            kiwi_s[:, :IDX_DIM].reshape(dec_batch, 1, IDX_DIM), pool_sample)


def kernel(x_prompt, x_sample, cache_k, cache_v, cache_kidx, state_pool, page_table, w_in, w_pool, pool_scale,
           w_branch_a, w_branch_b, w_out, ln1_g, ln1_b, w_group, w_expert_router, w_gate, w_up, w_down,
           ln2_g, ln2_b):
    assert w_in.shape[0] == DEPTH
    outs = _layer(x_prompt, x_sample, cache_k[0], cache_v[0], cache_kidx[0], state_pool[0], page_table, w_in[0],
                  w_pool[0], pool_scale[0], w_branch_a[0], w_branch_b[0], w_out[0], ln1_g[0], ln1_b[0],
                  w_group[0], w_expert_router[0], w_gate[0], w_up[0], w_down[0], ln2_g[0], ln2_b[0])
    y_p, y_s = outs[0], outs[1]
    return (y_p, y_s) + tuple(o[None] for o in outs[2:])
```

```python
import functools
import math

import jax
import jax.numpy as jnp
import numpy as np
from jax import lax
from jax.experimental import pallas as pl
from jax.experimental.pallas import tpu as pltpu

BF16 = jnp.bfloat16
F32 = jnp.float32
I32 = jnp.int32

PAGE_SIZE = 128
POOL_WINDOWS = (2, 4, 8, 16)
POOL_STATE = 15
N_HEADS = 16
N_KV_HEADS = 4
HEAD_DIM = 64
ROT_DIM = HEAD_DIM // 4
ROPE_THETA = 500000.0
IDX_HEADS = 16
IDX_DIM = 64
IDX_W_SCALE = (IDX_HEADS * IDX_DIM) ** -0.5
TOPK_MAX = 256
N_GROUPS = 4
EXPERTS_PER_GROUP = 8
N_EXPERTS = N_GROUPS * EXPERTS_PER_GROUP
LN_EPS = 1e-5
DEPTH = 1
ALPHA = (2 * DEPTH) ** 0.25

LANES = 128
SUBLANES = 8
Q_TILE = 128
KEY_CHUNK = 512
ROW_TILE = 256
MOE_ROW_TILE = 512
VMEM_LIMIT = 56 * 1024 * 1024

INT_MIN = -2 ** 31
NEG_INF_KEY = int(np.int32(np.uint32(0xFF800000) ^ np.uint32(0x7FFFFFFF)))
NT_DIMS = (((1,), (1,)), ((), ()))


def _sortable_key(x):
    bits = lax.bitcast_convert_type(x, I32)
    return bits ^ ((bits >> 31) & 0x7FFFFFFF)


def _const_spec(shape):
    nd = len(shape)
    return pl.BlockSpec(shape, lambda *_: (0,) * nd, pipeline_mode=pl.Buffered(1))


def _project(xb, wt_ref, cos, nsin, psin, pool_width):
    tm = xb.shape[0]
    lo = lax.broadcasted_iota(I32, (tm, LANES), 1) < HEAD_DIM
    attn_w, kv_w, idx_w = N_HEADS * HEAD_DIM, N_KV_HEADS * HEAD_DIM, IDX_HEADS * IDX_DIM
    off_q = pool_width
    off_k = off_q + attn_w
    off_v = off_k + kv_w
    off_qi = off_v + kv_w
    off_kiwi = off_qi + idx_w

    def proj(c0, width):
        return lax.dot_general(xb, wt_ref[c0:c0 + width, :], NT_DIMS, preferred_element_type=F32)

    def rope(z, c, s1, s2):
        return z * c + pltpu.roll(z, LANES - ROT_DIM // 2, 1) * s1 + pltpu.roll(z, ROT_DIM // 2, 1) * s2

    def rope_tiles(z):
        return [rope(z[:, j * LANES:(j + 1) * LANES], cos, nsin, psin) for j in range(z.shape[1] // LANES)]

    u = proj(0, pool_width)
    q = [t * (HEAD_DIM ** -0.5) for t in rope_tiles(proj(off_q, attn_w))]
    qi = rope_tiles(proj(off_qi, idx_w))
    k = rope_tiles(proj(off_k, kv_w))
    v = proj(off_v, kv_w)
    kw = rope(proj(off_kiwi, LANES), jnp.where(lo, cos, IDX_W_SCALE), jnp.where(lo, nsin, 0.0),
              jnp.where(lo, psin, 0.0))
    return u, q, qi, k, v, kw


def _pool_mix(d, g, wpool_ref, pscale_ref, gw):
    z = jnp.dot(d.astype(BF16), wpool_ref[g], preferred_element_type=F32)
    return z * pscale_ref[:, g * gw:(g + 1) * gw]


def _inproj_body(x_ref, wt_ref, tc_ref, ts1_ref, ts2_ref, wpool_ref, pscale_ref,
                 utail_ref, a_ref, qt_ref, qit_ref, kb_ref, kib_ref, kt_ref, vt_ref, vtb_ref, kit_ref, wt_out_ref,
                 uext_ref, *, tm, tiles_per_batch, pool_width):
    i = pl.program_id(0)
    u, q, qi, k, v, kw = _project(x_ref[...].astype(BF16), wt_ref, tc_ref[...], ts1_ref[...], ts2_ref[...],
                                  pool_width)
    lo = lax.broadcasted_iota(I32, (tm, LANES), 1) < HEAD_DIM
    gw = pool_width // len(POOL_WINDOWS)

    def blocks(tile):
        return [(slice(blk * Q_TILE, (blk + 1) * Q_TILE), tile[blk * Q_TILE:(blk + 1) * Q_TILE].T)
                for blk in range(tm // Q_TILE)]

    def store_head_t(dst_ref, h, tile):
        for blk, (_, t) in enumerate(blocks(tile)):
            dst_ref[blk, :, h * Q_TILE:(h + 1) * Q_TILE] = t.astype(BF16)

    for j, p in enumerate(q):
        pr = pltpu.roll(p, HEAD_DIM, 1)
        if (j // 2) % 2 == 0:
            store_head_t(qt_ref, 2 * j, jnp.where(lo, p, 0.0))
            store_head_t(qt_ref, 2 * j + 1, jnp.where(lo, pr, 0.0))
        else:
            store_head_t(qt_ref, 2 * j, jnp.where(lo, 0.0, pr))
            store_head_t(qt_ref, 2 * j + 1, jnp.where(lo, 0.0, p))
    for j, p in enumerate(qi):
        store_head_t(qit_ref, 2 * j, jnp.where(lo, p, 0.0))
        store_head_t(qit_ref, 2 * j + 1, jnp.where(lo, pltpu.roll(p, IDX_DIM, 1), 0.0))

    for j, kr in enumerate(k):
        lanes = slice(j * LANES, (j + 1) * LANES)
        kb_ref[:, lanes] = kr.astype(BF16)
        for cols, t in blocks(kr):
            kt_ref[0, lanes, cols] = t
        for cols, t in blocks(v[:, lanes]):
            vt_ref[0, lanes, cols] = t
            vtb_ref[lanes, cols] = t.astype(BF16)
    kib_ref[...] = kw.astype(BF16)
    for cols, t in blocks(kw):
        kit_ref[0, :, cols] = t[:IDX_DIM, :]
        wt_out_ref[:, cols] = t[IDX_DIM:IDX_DIM + IDX_HEADS, :]

    first = (i % tiles_per_batch) == 0

    @pl.when(first)
    def _():
        uext_ref[0:16, :] = jnp.zeros((16, pool_width), F32)

    @pl.when(jnp.logical_not(first))
    def _():
        uext_ref[0:16, :] = uext_ref[tm:tm + 16, :]

    uext_ref[16:16 + tm, :] = u
    utail_ref[0] = u[tm - 16:tm, :]
    pos = (i % tiles_per_batch) * tm + lax.broadcasted_iota(I32, (tm, 1), 0)
    for g, w in enumerate(POOL_WINDOWS):
        lanes = slice(g * gw, (g + 1) * gw)
        acc = u[:, lanes]
        for jj in range(1, w):
            acc = acc + uext_ref[16 - jj:16 - jj + tm, lanes]
        cnt = jnp.minimum(w, pos + 1).astype(F32)
        d = acc / cnt - u[:, lanes]
        a_ref[:, lanes] = _pool_mix(d, g, wpool_ref, pscale_ref, gw).astype(BF16)


def _in_projection(x, wt_a, tabs, wpool, pscale, *, batch, seq):
    n_rows, d_model = x.shape
    tm = ROW_TILE
    pool_width = wpool.shape[0] * wpool.shape[1]
    nblk = n_rows // Q_TILE
    kv_w = N_KV_HEADS * HEAD_DIM
    tpb = seq // tm
    row = lambda width: pl.BlockSpec((tm, width), lambda i: (i, 0))
    col = lambda height: pl.BlockSpec((height, tm), lambda i: (0, i))
    per_seq = lambda height: pl.BlockSpec((1, height, tm), lambda i: (i // tpb, 0, i % tpb))
    tab = pl.BlockSpec((tm, LANES), lambda i: (i % tpb, 0))
    hm = pl.BlockSpec((tm // Q_TILE, LANES, N_HEADS * Q_TILE), lambda i: (i, 0, 0))
    body = functools.partial(_inproj_body, tm=tm, tiles_per_batch=tpb, pool_width=pool_width)
    return pl.pallas_call(
        body,
        grid=(n_rows // tm,),
        in_specs=[row(d_model), _const_spec(wt_a.shape), tab, tab, tab,
                  _const_spec(wpool.shape), _const_spec(pscale.shape)],
        out_specs=[pl.BlockSpec((1, 16, pool_width), lambda i: (i // tpb, 0, 0)),
                   row(pool_width), hm, hm, row(kv_w), row(LANES), per_seq(kv_w), per_seq(kv_w), col(kv_w),
                   per_seq(IDX_DIM), col(IDX_HEADS)],
        out_shape=[
            jax.ShapeDtypeStruct((batch, 16, pool_width), F32),
            jax.ShapeDtypeStruct((n_rows, pool_width), BF16),
            jax.ShapeDtypeStruct((nblk, LANES, N_HEADS * Q_TILE), BF16),
            jax.ShapeDtypeStruct((nblk, LANES, IDX_HEADS * Q_TILE), BF16),
            jax.ShapeDtypeStruct((n_rows, kv_w), BF16),
            jax.ShapeDtypeStruct((n_rows, LANES), BF16),
            jax.ShapeDtypeStruct((batch, kv_w, seq), F32),
            jax.ShapeDtypeStruct((batch, kv_w, seq), F32),
            jax.ShapeDtypeStruct((kv_w, n_rows), BF16),
            jax.ShapeDtypeStruct((batch, IDX_DIM, seq), F32),
            jax.ShapeDtypeStruct((IDX_HEADS, n_rows), F32),
        ],
        scratch_shapes=[pltpu.VMEM((tm + 16, pool_width), F32)],
        compiler_params=pltpu.CompilerParams(dimension_semantics=("arbitrary",), vmem_limit_bytes=VMEM_LIMIT),
        name="in_projection",
    )(x, wt_a, *tabs, wpool, pscale)


def _inproj_sample_body(x_ref, wt_ref, tc_ref, ts1_ref, ts2_ref, wpool_ref, pscale_ref, state_ref,
                        u_ref, a_ref, q_ref, qi_ref, k_ref, v_ref, kiwi_ref, *, pool_width, sample_pos):
    u, q, qi, k, v, kw = _project(x_ref[...].astype(BF16), wt_ref, tc_ref[...], ts1_ref[...], ts2_ref[...],
                                  pool_width)
    gw = pool_width // len(POOL_WINDOWS)
    u_ref[...] = u
    for j, t in enumerate(q):
        q_ref[:, j * LANES:(j + 1) * LANES] = t.astype(BF16)
    for j, t in enumerate(qi):
        qi_ref[:, j * LANES:(j + 1) * LANES] = t.astype(BF16)
    for j, t in enumerate(k):
        k_ref[:, j * LANES:(j + 1) * LANES] = t
    v_ref[...] = v
    kiwi_ref[...] = kw
    for g, w in enumerate(POOL_WINDOWS):
        lanes = slice(g * gw, (g + 1) * gw)
        acc = u[:, lanes]
        for jj in range(1, w):
            acc = acc + state_ref[POOL_STATE - jj, :, lanes]
        d = acc / float(min(w, sample_pos + 1)) - u[:, lanes]
        a_ref[:, lanes] = _pool_mix(d, g, wpool_ref, pscale_ref, gw).astype(BF16)


def _in_projection_sample(x, wt_a, tabs, wpool, pscale, state_t, *, sample_pos):
    n_rows, d_model = x.shape
    pool_width = wpool.shape[0] * wpool.shape[1]
    kv_w = N_KV_HEADS * HEAD_DIM
    full = lambda shape: pl.BlockSpec(shape, lambda i: (0,) * len(shape))
    body = functools.partial(_inproj_sample_body, pool_width=pool_width, sample_pos=sample_pos)
    out_shape = [
        jax.ShapeDtypeStruct((n_rows, pool_width), F32),
        jax.ShapeDtypeStruct((n_rows, pool_width), BF16),
        jax.ShapeDtypeStruct((n_rows, N_HEADS * HEAD_DIM), BF16),
        jax.ShapeDtypeStruct((n_rows, IDX_HEADS * IDX_DIM), BF16),
        jax.ShapeDtypeStruct((n_rows, kv_w), F32),
        jax.ShapeDtypeStruct((n_rows, kv_w), F32),
        jax.ShapeDtypeStruct((n_rows, LANES), F32),
    ]
    return pl.pallas_call(
        body,
        grid=(1,),
        in_specs=[full(x.shape), _const_spec(wt_a.shape), full(tabs[0].shape), full(tabs[1].shape),
                  full(tabs[2].shape), _const_spec(wpool.shape), _const_spec(pscale.shape), full(state_t.shape)],
        out_specs=[full(s.shape) for s in out_shape],
        out_shape=out_shape,
        compiler_params=pltpu.CompilerParams(dimension_semantics=("arbitrary",), vmem_limit_bytes=VMEM_LIMIT),
        name="in_projection_sample",
    )(x, wt_a, *tabs, wpool, pscale, state_t)


def _rope_tables(positions):
    pos = np.asarray(positions, np.float32)
    n = pos.shape[0]
    half = ROT_DIM // 2
    inv = np.exp(-np.arange(half, dtype=np.float32) * np.float32(math.log(ROPE_THETA) / half)).astype(np.float32)
    ang = (pos[:, None] * inv[None, :]).astype(np.float32).astype(np.float64)
    cos, sin = np.cos(ang).astype(np.float32), np.sin(ang).astype(np.float32)
    zeros = np.zeros((n, half), np.float32)
    rest = HEAD_DIM - ROT_DIM
    c = np.concatenate([cos, cos, np.ones((n, rest), np.float32)], axis=1)
    s1 = np.concatenate([-sin, zeros, np.zeros((n, rest), np.float32)], axis=1)
    s2 = np.concatenate([zeros, sin, np.zeros((n, rest), np.float32)], axis=1)
    return tuple(jnp.asarray(np.concatenate([t, t], axis=1)) for t in (c, s1, s2))


def _topk_threshold(count_ge, init, topk):
    def step(it, t):
        cand = t + jnp.left_shift(jnp.int32(1), 31 - it)
        return jnp.where(count_ge(cand) >= topk, cand, t)
    return lax.fori_loop(0, 32, step, init)


def _fold_slabs(x, op):
    assert x.shape[0] & (x.shape[0] - 1) == 0
    while x.shape[0] > 1:
        half = x.shape[0] // 2
        x = op(x[:half], x[half:])
    return x[0]


def _prompt_attn_body(qit_ref, wt_ref, qt_ref, kib_ref, kb_ref, vt_ref, o_ref,
                      sc_ref, thr_ref, m_ref, l_ref, acc_ref, *, topk):
    i = pl.program_id(1)
    tq, kc = Q_TILE, KEY_CHUNK
    n_keys = (i + 1) * tq
    n_chunks = (n_keys + kc - 1) // kc
    qpos = i * tq + lax.broadcasted_iota(I32, (1, tq), 1)
    group = N_HEADS // N_KV_HEADS
    gcols = group * tq

    def score_chunk(c, carry):
        k0 = pl.multiple_of(c * kc, kc)
        kchunk = kib_ref[pl.ds(k0, kc), :]
        acc = jnp.zeros((kc, tq), F32)
        for hg in range(IDX_HEADS // group):
            d = jnp.dot(kchunk, qit_ref[0, :, hg * gcols:(hg + 1) * gcols], preferred_element_type=F32)
            for r in range(group):
                h = hg * group + r
                acc = acc + jnp.maximum(d[:, r * tq:(r + 1) * tq], 0.0) * wt_ref[h:h + 1, :]
        kpos = k0 + lax.broadcasted_iota(I32, (kc, 1), 0)
        sc_ref[pl.ds(k0, kc), :] = _sortable_key(jnp.where(kpos <= qpos, acc, -jnp.inf))
        return carry

    lax.fori_loop(0, n_chunks, score_chunk, 0)

    thr_ref[...] = jnp.full((1, tq), NEG_INF_KEY + 1, I32)

    @pl.when(n_keys > topk)
    def _():
        def count_ge(cand):
            candb = jnp.broadcast_to(cand, (SUBLANES, tq))

            def blk(c, acc):
                k0 = pl.multiple_of(c * kc, kc)
                hit = jnp.where(sc_ref[pl.ds(k0, kc), :].reshape(kc // SUBLANES, SUBLANES, tq) >= candb, 1.0, 0.0)
                return acc + _fold_slabs(hit, jnp.add)
            acc = lax.fori_loop(0, n_chunks, blk, jnp.zeros((SUBLANES, tq), F32))
            return jnp.sum(acc, axis=0, keepdims=True)

        t = _topk_threshold(count_ge, jnp.full((1, tq), INT_MIN, I32), float(topk))
        thr_ref[...] = jnp.where(qpos + 1 > topk, t, NEG_INF_KEY + 1)

    m_ref[...] = jnp.full(m_ref.shape, -jnp.inf, F32)
    l_ref[...] = jnp.zeros(l_ref.shape, F32)
    acc_ref[...] = jnp.zeros(acc_ref.shape, F32)

    def attn_chunk(c, carry):
        k0 = pl.multiple_of(c * kc, kc)
        sel = sc_ref[pl.ds(k0, kc), :] >= thr_ref[...]
        for g in range(N_KV_HEADS):
            pair = slice((g // 2) * LANES, (g // 2 + 1) * LANES)
            cols = slice(g * gcols, (g + 1) * gcols)
            s = jnp.dot(kb_ref[pl.ds(k0, kc), pair], qt_ref[0, :, cols], preferred_element_type=F32)
            s = jnp.concatenate([jnp.where(sel, s[:, r * tq:(r + 1) * tq], -jnp.inf) for r in range(group)], axis=1)
            m_old = m_ref[:, cols]
            m_new = jnp.maximum(m_old, jnp.max(s, axis=0, keepdims=True))
            m_safe = jnp.where(m_new == -jnp.inf, 0.0, m_new)
            alpha = jnp.exp(m_old - m_safe)
            p = jnp.exp(s - m_safe)
            l_ref[:, cols] = alpha * l_ref[:, cols] + jnp.sum(p, axis=0, keepdims=True)
            pv = jnp.dot(vt_ref[pair, pl.ds(k0, kc)], p.astype(BF16), preferred_element_type=F32)
            acc_ref[g] = alpha * acc_ref[g] + pv
            m_ref[:, cols] = m_new
        return carry

    lax.fori_loop(0, n_chunks, attn_chunk, 0)
    l_all = l_ref[...]

    for j in range(N_HEADS // 2):
        halves = []
        for h in (2 * j, 2 * j + 1):
            g, r = h // group, h % group
            rows = slice((g % 2) * HEAD_DIM, (g % 2 + 1) * HEAD_DIM)
            cols = slice(r * tq, (r + 1) * tq)
            halves.append(acc_ref[g, rows, cols] / l_all[:, g * gcols + r * tq:g * gcols + (r + 1) * tq])
        o_ref[:, j * LANES:(j + 1) * LANES] = jnp.concatenate(halves, axis=0).T.astype(BF16)


def _prompt_attention(qit, wt, qt, kib, kb, vt, *, batch, seq):
    nq = seq // Q_TILE
    topk = min(TOPK_MAX, seq // 4)
    kv_w = N_KV_HEADS * HEAD_DIM
    per_q = lambda b, i: (b * nq + i, 0, 0)
    body = functools.partial(_prompt_attn_body, topk=topk)
    return pl.pallas_call(
        body,
        grid=(batch, nq),
        in_specs=[
            pl.BlockSpec((1, LANES, IDX_HEADS * Q_TILE), per_q),
            pl.BlockSpec((IDX_HEADS, Q_TILE), lambda b, i: (0, b * nq + i)),
            pl.BlockSpec((1, LANES, N_HEADS * Q_TILE), per_q),
            pl.BlockSpec((seq, LANES), lambda b, i: (b, 0)),
            pl.BlockSpec((seq, kv_w), lambda b, i: (b, 0)),
            pl.BlockSpec((kv_w, seq), lambda b, i: (0, b)),
        ],
        out_specs=pl.BlockSpec((Q_TILE, N_HEADS * HEAD_DIM), lambda b, i: (b * nq + i, 0)),
        out_shape=jax.ShapeDtypeStruct((batch * seq, N_HEADS * HEAD_DIM), BF16),
        scratch_shapes=[
            pltpu.VMEM((seq, Q_TILE), I32),
            pltpu.VMEM((1, Q_TILE), I32),
            pltpu.VMEM((1, N_HEADS * Q_TILE), F32),
            pltpu.VMEM((1, N_HEADS * Q_TILE), F32),
            pltpu.VMEM((N_KV_HEADS, LANES, (N_HEADS // N_KV_HEADS) * Q_TILE), F32),
        ],
        compiler_params=pltpu.CompilerParams(dimension_semantics=("arbitrary", "arbitrary"),
                                             vmem_limit_bytes=VMEM_LIMIT),
        name="prompt_attention",
    )(qit, wt, qt, kib, kb, vt)


SAMPLE_PAGES_PER_CHUNK = 16


def _sample_attn_body(pt_ref, qi_ref, w_ref, qs_ref, kicur_ref, kcur_ref, vcur_ref, ckidx_hbm, ck_hbm, cv_hbm,
                      o_ref, kidx_buf, k_buf, v_buf, sc_ref, sem_idx, sem_k, sem_v, *, n_pages, topk):
    b = pl.program_id(0)
    last = pl.num_programs(0) - 1
    ppc = SAMPLE_PAGES_PER_CHUNK
    n_chunks = n_pages // ppc
    ck = ppc * PAGE_SIZE
    seg = sc_ref.shape[1]
    segs_per_chunk = ck // seg

    def idx_copy(seq, p):
        return pltpu.make_async_copy(ckidx_hbm.at[pt_ref[seq, p]],
                                     kidx_buf.at[seq % 2, :, pl.ds(p * PAGE_SIZE, PAGE_SIZE)], sem_idx.at[seq % 2])

    def kv_copies(seq, c, p):
        slot = c % 2
        page = pt_ref[seq, c * ppc + p]
        window = pl.ds(p * PAGE_SIZE, PAGE_SIZE)
        return (pltpu.make_async_copy(ck_hbm.at[page], k_buf.at[slot, :, window], sem_k.at[slot]),
                pltpu.make_async_copy(cv_hbm.at[page], v_buf.at[slot, :, window], sem_v.at[slot]))

    def start_idx(seq):
        for p in range(n_pages):
            idx_copy(seq, p).start()

    def start_kv(seq, c):
        for p in range(ppc):
            for cp in kv_copies(seq, c, p):
                cp.start()

    def wait_kv(seq, c):
        for p in range(ppc):
            for cp in kv_copies(seq, c, p):
                cp.wait()

    @pl.when(b == 0)
    def _():
        start_idx(b)
        start_kv(b, 0)

    for p in range(n_pages):
        idx_copy(b, p).wait()

    qi = qi_ref[0][:, :IDX_DIM]
    w = w_ref[0]

    def head_mix(d):
        return jnp.sum(jnp.maximum(d, 0.0) * w, axis=0, keepdims=True)

    def bf16_products(a, row):
        return jnp.sum(a.astype(F32) * row.astype(BF16).astype(F32), axis=1, keepdims=True)

    for c in range(n_chunks):
        d = jnp.dot(qi, kidx_buf[b % 2, :, c * ck:(c + 1) * ck].astype(BF16), preferred_element_type=F32)
        keys = _sortable_key(head_mix(d))
        for r in range(segs_per_chunk):
            sc_ref[c * segs_per_chunk + r:c * segs_per_chunk + r + 1, :] = keys[:, r * seg:(r + 1) * seg]
    cur_key = _sortable_key(head_mix(bf16_products(qi, kicur_ref[0][:, :IDX_DIM])))

    vreg = (SUBLANES, LANES)
    if n_pages * PAGE_SIZE + 1 > topk:
        cur_key_v = jnp.broadcast_to(cur_key, vreg)

        def count_ge(cand):
            hits = jnp.where(cur_key_v >= cand, 1.0 / (SUBLANES * LANES), 0.0)
            for j in range(seg // LANES):
                hits = hits + jnp.where(sc_ref[:, j * LANES:(j + 1) * LANES] >= cand, 1.0, 0.0)
            total = jnp.sum(jnp.sum(hits, axis=1, keepdims=True), axis=0, keepdims=True)
            return jnp.broadcast_to(total, vreg)
        thr = _topk_threshold(count_ge, jnp.full(vreg, INT_MIN, I32), float(topk))[0:1, 0:1]
    else:
        thr = jnp.full((1, 1), NEG_INF_KEY + 1, I32)

    qs = qs_ref[0]
    m = jnp.full((N_HEADS, 1), -jnp.inf, F32)
    l = jnp.zeros((N_HEADS, 1), F32)
    acc = jnp.zeros((N_HEADS, N_KV_HEADS * HEAD_DIM), F32)

    def update(m, l, acc, s, sel, pv):
        s = jnp.where(sel, s, -jnp.inf)
        m_new = jnp.maximum(m, jnp.max(s, axis=1, keepdims=True))
        m_safe = jnp.where(m_new == -jnp.inf, 0.0, m_new)
        alpha = jnp.exp(m - m_safe)
        p = jnp.exp(s - m_safe)
        l = alpha * l + jnp.sum(p, axis=1, keepdims=True)
        acc = alpha * acc + pv(p.astype(BF16))
        return m_new, l, acc

    def prefetch_next_sequence():
        @pl.when(b < last)
        def _():
            start_idx(b + 1)
            start_kv(b + 1, 0)

    slot0_free_early = n_chunks % 2 == 0
    for c in range(n_chunks):
        if c + 1 < n_chunks:
            start_kv(b, c + 1)
        elif slot0_free_early:
            prefetch_next_sequence()
        wait_kv(b, c)
        slot = c % 2
        s = jnp.dot(qs, k_buf[slot].astype(BF16), preferred_element_type=F32)
        vt = v_buf[slot].astype(BF16)
        sel = jnp.concatenate([sc_ref[c * segs_per_chunk + r:c * segs_per_chunk + r + 1, :]
                               for r in range(segs_per_chunk)], axis=1) >= thr
        m, l, acc = update(m, l, acc, s, sel, lambda p: lax.dot_general(p, vt, NT_DIMS, preferred_element_type=F32))
    if not slot0_free_early:
        prefetch_next_sequence()
    v_cur = vcur_ref[0].astype(BF16).astype(F32)
    m, l, acc = update(m, l, acc, bf16_products(qs, kcur_ref[0]), cur_key >= thr,
                       lambda p: p.astype(F32) * v_cur)
    o_ref[0] = acc / l


def _keys_minor(cache):
    pages, page_size = cache.shape[:2]
    return jnp.moveaxis(cache, 1, -1).reshape(pages, -1, page_size)


def _sample_attention(page_table, qi_s, w_s, qs, kicur, kcur, vcur, cache_kidx, cache_k, cache_v):
    dec_batch, n_pages = page_table.shape
    past = n_pages * PAGE_SIZE
    topk = min(TOPK_MAX, (past + 1) // 4)
    kv_w = N_KV_HEADS * HEAD_DIM
    ppc = SAMPLE_PAGES_PER_CHUNK
    per_b = lambda b, pt: (b, 0, 0)
    any_spec = pl.BlockSpec(memory_space=pl.ANY)
    grid_spec = pltpu.PrefetchScalarGridSpec(
        num_scalar_prefetch=1,
        grid=(dec_batch,),
        in_specs=[
            pl.BlockSpec((1, IDX_HEADS, LANES), per_b),
            pl.BlockSpec((1, IDX_HEADS, 1), per_b),
            pl.BlockSpec((1, N_HEADS, kv_w), per_b),
            pl.BlockSpec((1, 1, LANES), per_b),
            pl.BlockSpec((1, 1, kv_w), per_b),
            pl.BlockSpec((1, 1, kv_w), per_b),
            any_spec, any_spec, any_spec,
        ],
        out_specs=pl.BlockSpec((1, N_HEADS, kv_w), per_b),
        scratch_shapes=[
            pltpu.VMEM((2, IDX_DIM, past), F32),
            pltpu.VMEM((2, kv_w, ppc * PAGE_SIZE), F32),
            pltpu.VMEM((2, kv_w, ppc * PAGE_SIZE), F32),
            pltpu.VMEM((SUBLANES, past // SUBLANES), I32),
            pltpu.SemaphoreType.DMA((2,)),
            pltpu.SemaphoreType.DMA((2,)),
            pltpu.SemaphoreType.DMA((2,)),
        ],
    )
    body = functools.partial(_sample_attn_body, n_pages=n_pages, topk=topk)
    return pl.pallas_call(
        body,
        grid_spec=grid_spec,
        out_shape=jax.ShapeDtypeStruct((dec_batch, N_HEADS, kv_w), F32),
        compiler_params=pltpu.CompilerParams(dimension_semantics=("arbitrary",), vmem_limit_bytes=VMEM_LIMIT),
        name="sample_attention",
    )(page_table, qi_s, w_s, qs, kicur, kcur, vcur, cache_kidx, cache_k, cache_v)


def _mm(a, b, precise, *, nt=False):
    dims = NT_DIMS if nt else (((1,), (0,)), ((), ()))
    if precise:
        return lax.dot_general(a.astype(F32), b, dims, precision=lax.Precision.HIGHEST, preferred_element_type=F32)
    return lax.dot_general(a.astype(BF16), b, dims, preferred_element_type=F32)


def _operand_dtype(precise):
    return F32 if precise else BF16


MERGE_COL_CHUNK = 512


def _merge_body(x_ref, a_ref, o_ref, wgat_ref, wgbt_ref, wba_ref, wbb_ref, out_ref, *, precise):
    x, a, o = x_ref[...], a_ref[...], o_ref[...]
    cc = MERGE_COL_CHUNK
    for c in range(out_ref.shape[1] // cc):
        cols = slice(c * cc, (c + 1) * cc)
        ga = _mm(x, wgat_ref[cols, :], precise, nt=True)
        gb = _mm(x, wgbt_ref[cols, :], precise, nt=True)
        ya = _mm(a, wba_ref[:, cols], precise)
        yo = _mm(o, wbb_ref[:, cols], precise)
        out_ref[:, cols] = (jax.nn.sigmoid(ga) * ya + jax.nn.sigmoid(gb) * yo).astype(out_ref.dtype)


def _merge(x, a, o, wgat, wgbt, wba, wbb, *, tm, col_block, precise):
    n_rows, d_model = x.shape
    mode = dict(pipeline_mode=pl.Buffered(1)) if col_block == d_model else {}
    gate_spec = pl.BlockSpec((col_block, d_model), lambda i, j: (j, 0), **mode)
    branch_spec = lambda w: pl.BlockSpec((w.shape[0], col_block), lambda i, j: (0, j), **mode)
    row = lambda width: pl.BlockSpec((tm, width), lambda i, j: (i, 0))
    body = functools.partial(_merge_body, precise=precise)
    return pl.pallas_call(
        body,
        grid=(n_rows // tm, d_model // col_block),
        in_specs=[row(d_model), row(a.shape[1]), row(o.shape[1]), gate_spec, gate_spec,
                  branch_spec(wba), branch_spec(wbb)],
        out_specs=pl.BlockSpec((tm, col_block), lambda i, j: (i, j)),
        out_shape=jax.ShapeDtypeStruct((n_rows, d_model), _operand_dtype(precise)),
        compiler_params=pltpu.CompilerParams(dimension_semantics=("arbitrary", "arbitrary"),
                                             vmem_limit_bytes=VMEM_LIMIT),
        name="branch_merge",
    )(x, a, o, wgat, wgbt, wba, wbb)


def _layer_norm(x, g, b):
    mu = jnp.mean(x, axis=-1, keepdims=True)
    var = jnp.mean(jnp.square(x - mu), axis=-1, keepdims=True)
    return (x - mu) * lax.rsqrt(var + LN_EPS) * g + b


def _route(logits):
    lane = lax.broadcasted_iota(I32, logits.shape, 1)
    lane_f = lane.astype(F32)
    big = float(LANES)

    def masked_softmax(mask):
        x = jnp.where(mask, logits, -jnp.inf)
        e = jnp.exp(x - jnp.max(x, axis=1, keepdims=True))
        return jnp.where(mask, e / jnp.sum(e, axis=1, keepdims=True), -1.0)

    def top1(p):
        best = jnp.max(p, axis=1, keepdims=True)
        idx = jnp.min(jnp.where(p == best, lane_f, big), axis=1, keepdims=True)
        return best, idx

    gp = masked_softmax((lane >= N_EXPERTS) & (lane < N_EXPERTS + N_GROUPS))
    g_p, g_lane = top1(gp)
    g_idx = g_lane.astype(I32) - N_EXPERTS
    ep = masked_softmax((lane < N_EXPERTS) & ((lane >> int(math.log2(EXPERTS_PER_GROUP))) == g_idx))
    p1, i1 = top1(ep)
    p2, i2 = top1(jnp.where(lane_f == i1, -1.0, ep))
    denom = p1 + p2
    return jnp.where(lane_f == i1, g_p * p1 / denom, 0.0) + jnp.where(lane_f == i2, g_p * p2 / denom, 0.0)


def _store_token_major(ref, x):
    chunks = x.shape[1] // LANES
    for c in range(chunks):
        ref[pl.ds(c, x.shape[0], stride=chunks), :] = x[:, c * LANES:(c + 1) * LANES]


def _load_token_major(ref, n_tokens):
    chunks = ref.shape[0] // n_tokens
    return jnp.concatenate([ref[pl.ds(c, n_tokens, stride=chunks), :] for c in range(chunks)], axis=1)


def _outproj_body(m_ref, x_ref, wout_ref, g_ref, b_ref, wr_ref, h_ref, dense_ref, *, precise, n_real):
    i = pl.program_id(0)

    @pl.when(i < n_real)
    def _():
        mo = _mm(m_ref[...], wout_ref[...], precise)
        h = _layer_norm(ALPHA * x_ref[...] + mo, g_ref[...], b_ref[...])
        _store_token_major(h_ref, h)
        dense_ref[...] = _route(_mm(h, wr_ref[...], precise))

    @pl.when(i >= n_real)
    def _():
        h_ref[...] = jnp.zeros(h_ref.shape, F32)
        dense_ref[...] = jnp.zeros(dense_ref.shape, F32)


def _out_projection(merged, x, wout, ln_g, ln_b, w_route, *, tm, precise, zero_tiles=0):
    n_rows, d_model = x.shape
    n_real = n_rows // tm
    chunks = d_model // LANES
    body = functools.partial(_outproj_body, precise=precise, n_real=n_real)
    in_row = lambda width: pl.BlockSpec((tm, width), lambda i: (jnp.minimum(i, n_real - 1), 0))
    n_out = n_rows + zero_tiles * tm
    return pl.pallas_call(
        body,
        grid=(n_real + zero_tiles,),
        in_specs=[in_row(d_model), in_row(d_model), _const_spec(wout.shape),
                  _const_spec(ln_g.shape), _const_spec(ln_b.shape), _const_spec(w_route.shape)],
        out_specs=[pl.BlockSpec((tm * chunks, LANES), lambda i: (i, 0)), pl.BlockSpec((tm, LANES), lambda i: (i, 0))],
        out_shape=[jax.ShapeDtypeStruct((n_out * chunks, LANES), F32),
                   jax.ShapeDtypeStruct((n_out, LANES), F32)],
        compiler_params=pltpu.CompilerParams(dimension_semantics=("arbitrary",), vmem_limit_bytes=VMEM_LIMIT),
        name="out_projection",
    )(merged, x, wout, ln_g, ln_b, w_route)


EXPERT_ROW_TILE = 256
TOP_K_EXPERTS = 2


def _route_plan(dense, tm):
    n_tok = dense.shape[0]
    n_asg = TOP_K_EXPERTS * n_tok
    n_tiles = (n_asg + N_EXPERTS * (tm - 1)) // tm + 1
    w2, e2 = lax.top_k(dense[:, :N_EXPERTS], TOP_K_EXPERTS)
    eflat = e2.reshape(n_asg).astype(I32)
    w_bits = lax.bitcast_convert_type(w2.reshape(n_asg), I32)
    e_sorted, a_sorted, w_sorted = lax.sort((eflat, jnp.arange(n_asg, dtype=I32), w_bits), num_keys=1, is_stable=True)
    starts = jnp.searchsorted(e_sorted, jnp.arange(N_EXPERTS + 1, dtype=I32), side="left").astype(I32)
    counts = starts[1:] - starts[:-1]
    tiles_e = (counts + tm - 1) // tm
    tile_end = jnp.cumsum(tiles_e).astype(I32)
    tile_start = tile_end - tiles_e
    tile = jnp.arange(n_tiles, dtype=I32)
    tile_expert = jnp.minimum(jnp.searchsorted(tile_end, tile, side="right"), N_EXPERTS - 1).astype(I32)
    tile_used = (tile < tile_end[-1]).astype(I32)
    q = ((tile - tile_start[tile_expert]) * tm)[:, None] + jnp.arange(tm, dtype=I32)[None, :]
    valid = ((q < counts[tile_expert][:, None]) & (tile_used[:, None] > 0)).reshape(n_tiles * tm)
    pos = jnp.clip(starts[tile_expert][:, None] + q, 0, n_asg - 1).reshape(n_tiles * tm)
    picked = jnp.stack([a_sorted, w_sorted], axis=1)[pos]
    a = picked[:, 0]
    src = jnp.where(valid, a // TOP_K_EXPERTS, 0)
    slot = a % TOP_K_EXPERTS
    weight = jnp.where(valid, lax.bitcast_convert_type(picked[:, 1], F32), 0.0)
    return src, slot, valid, weight, tile_expert, tile_used


def _expert_body(te_ref, used_ref, src_ref, dst_ref, cw_ref, wg_ref, wu_ref, wd_ref, h_hbm, y_hbm,
                 x_buf, y_buf, sem_in, sem_out, *, tm):
    j = pl.program_id(0)
    chunks = x_buf.shape[0] // tm

    def token_rows(t):
        return pl.ds(pl.multiple_of(t * chunks, chunks), chunks)

    def gather(r):
        return pltpu.make_async_copy(h_hbm.at[token_rows(src_ref[0, 0, r]), :], x_buf.at[token_rows(r), :],
                                     sem_in.at[0])

    def scatter(r):
        return pltpu.make_async_copy(y_buf.at[token_rows(r), :], y_hbm.at[token_rows(dst_ref[0, 0, r]), :],
                                     sem_out.at[0])

    @pl.when(j == 0)
    def _():
        y_buf[...] = jnp.zeros(y_buf.shape, F32)
        pad_rows = pltpu.make_async_copy(y_buf, y_hbm.at[pl.ds(y_hbm.shape[0] - tm * chunks, tm * chunks), :],
                                         sem_out.at[0])
        pad_rows.start()
        pad_rows.wait()

    @pl.when(used_ref[j] > 0)
    def _():
        for r in range(tm):
            gather(r).start()
        for r in range(tm):
            gather(r).wait()
        xb = _load_token_major(x_buf, tm).astype(BF16)
        gate = jnp.dot(xb, wg_ref[0].astype(BF16), preferred_element_type=F32)
        up = jnp.dot(xb, wu_ref[0].astype(BF16), preferred_element_type=F32)
        hh = (gate * jax.nn.sigmoid(gate)) * up * cw_ref[0]
        _store_token_major(y_buf, jnp.dot(hh.astype(BF16), wd_ref[0].astype(BF16), preferred_element_type=F32))
        for r in range(tm):
            scatter(r).start()
        for r in range(tm):
            scatter(r).wait()


def _experts(h_all, plan, wg, wu, wd, *, tm, slot_stride):
    src, slot, valid, weight, tile_expert, tile_used = plan
    n_experts, d_model, d_expert = wg.shape
    chunks = d_model // LANES
    n_tiles = tile_expert.shape[0]
    pad_row = 2 * slot_stride + (jnp.arange(n_tiles * tm, dtype=I32) % tm)
    dst = jnp.where(valid, slot * slot_stride + src, pad_row)
    per_tile = lambda a: a.reshape(n_tiles, 1, tm)
    smem_rows = pl.BlockSpec((1, 1, tm), lambda j, te, used: (j, 0, 0), memory_space=pltpu.SMEM)
    any_spec = pl.BlockSpec(memory_space=pl.ANY)
    grid_spec = pltpu.PrefetchScalarGridSpec(
        num_scalar_prefetch=2,
        grid=(n_tiles,),
        in_specs=[smem_rows, smem_rows,
                  pl.BlockSpec((1, tm, 1), lambda j, te, used: (j, 0, 0)),
                  pl.BlockSpec((1, d_model, d_expert), lambda j, te, used: (te[j], 0, 0)),
                  pl.BlockSpec((1, d_model, d_expert), lambda j, te, used: (te[j], 0, 0)),
                  pl.BlockSpec((1, d_expert, d_model), lambda j, te, used: (te[j], 0, 0)),
                  any_spec],
        out_specs=any_spec,
        scratch_shapes=[pltpu.VMEM((tm * chunks, LANES), F32), pltpu.VMEM((tm * chunks, LANES), F32),
                        pltpu.SemaphoreType.DMA((1,)), pltpu.SemaphoreType.DMA((1,))],
    )
    return pl.pallas_call(
        functools.partial(_expert_body, tm=tm),
        grid_spec=grid_spec,
        out_shape=jax.ShapeDtypeStruct(((2 * slot_stride + tm) * chunks, LANES), F32),
        compiler_params=pltpu.CompilerParams(dimension_semantics=("arbitrary",), vmem_limit_bytes=VMEM_LIMIT),
        name="expert_mlp",
    )(tile_expert, tile_used, per_tile(src), per_tile(dst), weight.reshape(n_tiles, tm, 1), wg, wu, wd, h_all)


def _combine_body(y0_ref, y1_ref, h_ref, g_ref, b_ref, out_ref):
    tm = out_ref.shape[0]
    f = _load_token_major(y0_ref, tm) + _load_token_major(y1_ref, tm)
    out_ref[...] = _layer_norm(ALPHA * _load_token_major(h_ref, tm) + f, g_ref[...], b_ref[...])


def _combine(y, h_all, ln_g, ln_b, *, tm, tile0, n_tiles, slot_stride):
    d_model = ln_g.shape[1]
    chunks = d_model // LANES
    slot = lambda k: pl.BlockSpec((tm * chunks, LANES), lambda i: (i + tile0 + k * (slot_stride // tm), 0))
    return pl.pallas_call(
        _combine_body,
        grid=(n_tiles,),
        in_specs=[slot(0), slot(1), slot(0), _const_spec(ln_g.shape), _const_spec(ln_b.shape)],
        out_specs=pl.BlockSpec((tm, d_model), lambda i: (i, 0)),
        out_shape=jax.ShapeDtypeStruct((n_tiles * tm, d_model), F32),
        compiler_params=pltpu.CompilerParams(dimension_semantics=("arbitrary",), vmem_limit_bytes=VMEM_LIMIT),
        name="expert_combine",
    )(y, y, h_all, ln_g, ln_b)


def _layer(x_prompt, x_sample, cache_k, cache_v, cache_kidx, state_pool, page_table, w_in, w_pool, pool_scale,
           w_branch_a, w_branch_b, w_out, ln1_g, ln1_b, w_group, w_expert_router, w_gate, w_up, w_down,
           ln2_g, ln2_b):
    batch, seq, d_model = x_prompt.shape
    dec_batch = x_sample.shape[0]
    n_pages = page_table.shape[1]
    past = n_pages * PAGE_SIZE
    n_prompt = batch * seq
    kv_w = N_KV_HEADS * HEAD_DIM
    attn_w = N_HEADS * HEAD_DIM
    pool_width = w_pool.shape[0] * w_pool.shape[1]
    group = N_HEADS // N_KV_HEADS
    assert x_sample.shape[1] == 1 and seq % KEY_CHUNK == 0 and seq % MOE_ROW_TILE == 0
    assert dec_batch % 16 == 0 and n_pages % SAMPLE_PAGES_PER_CHUNK == 0
    assert (SAMPLE_PAGES_PER_CHUNK * PAGE_SIZE) % (past // SUBLANES) == 0

    x_p = x_prompt.reshape(n_prompt, d_model)
    x_s = x_sample.reshape(dec_batch, d_model)

    w_t = jnp.transpose(w_in)
    a_width = w_in.shape[1] - 2 * d_model
    wt_a = jnp.concatenate([w_t[:a_width], jnp.zeros((-a_width % LANES, d_model), F32)], axis=0).astype(BF16)
    wgat, wgbt = w_t[a_width:a_width + d_model], w_t[a_width + d_model:]
    wpool_b, pscale = w_pool.astype(BF16), pool_scale.reshape(1, pool_width)

    utail, a_p, qt, qit, kb, kib, kt, vt, vtb, kit, wt = _in_projection(
        x_p, wt_a, _rope_tables(np.arange(seq)), wpool_b, pscale, batch=batch, seq=seq)
    o_prompt = _prompt_attention(qit, wt, qt, kib, kb, vtb, batch=batch, seq=seq)

    u_s, a_s, q_s, qi_s, k_s, v_s, kiwi_s = _in_projection_sample(
        x_s, wt_a, _rope_tables(np.full((dec_batch,), past)), wpool_b, pscale, jnp.transpose(state_pool, (1, 0, 2)),
        sample_pos=past)
    qi_h = qi_s.reshape(dec_batch, IDX_HEADS, IDX_DIM)
    qi_h = jnp.concatenate([qi_h, jnp.zeros_like(qi_h)], axis=-1)
    in_group = (jnp.arange(N_HEADS)[:, None] // group == jnp.arange(N_KV_HEADS)[None, :])[None, :, :, None]
    qs = jnp.where(in_group, q_s.reshape(dec_batch, N_HEADS, 1, HEAD_DIM), 0).reshape(dec_batch, N_HEADS, kv_w)
    w_s = kiwi_s[:, IDX_DIM:IDX_DIM + IDX_HEADS].reshape(dec_batch, IDX_HEADS, 1)
    o_s = _sample_attention(page_table, qi_h, w_s, qs, kiwi_s.reshape(dec_batch, 1, LANES),
                            k_s.reshape(dec_batch, 1, kv_w), v_s.reshape(dec_batch, 1, kv_w),
                            _keys_minor(cache_kidx), _keys_minor(cache_k), _keys_minor(cache_v))
    o_s = o_s.reshape(dec_batch, N_KV_HEADS, group, N_KV_HEADS, HEAD_DIM)
    o_s = jnp.transpose(jnp.diagonal(o_s, axis1=1, axis2=3), (0, 3, 1, 2)).reshape(dec_batch, attn_w)
    o_sample = o_s.astype(BF16)

    w_route = jnp.concatenate([w_expert_router, w_group,
                               jnp.zeros((d_model, LANES - N_EXPERTS - N_GROUPS), F32)], axis=1)
    g1, b1 = ln1_g.reshape(1, d_model), ln1_b.reshape(1, d_model)
    g2, b2 = ln2_g.reshape(1, d_model), ln2_b.reshape(1, d_model)

    merged_p = _merge(x_p, a_p, o_prompt, wgat.astype(BF16), wgbt.astype(BF16), w_branch_a.astype(BF16),
                      w_branch_b.astype(BF16), tm=ROW_TILE, col_block=d_model, precise=False)
    h_all, dense_all = _out_projection(merged_p, x_p, w_out.astype(BF16), g1, b1, w_route.astype(BF16),
                                       tm=ROW_TILE, precise=False, zero_tiles=1)
    merged_s = _merge(x_s, a_s, o_sample, wgat, wgbt, w_branch_a, w_branch_b, tm=dec_batch,
                      col_block=MERGE_COL_CHUNK, precise=True)
    h_s, dense_s = _out_projection(merged_s, x_s, w_out, g1, b1, w_route, tm=dec_batch, precise=True)

    h_all = lax.dynamic_update_slice(h_all, h_s, (n_prompt * (d_model // LANES), 0))
    dense_all = lax.dynamic_update_slice(dense_all, dense_s, (n_prompt, 0))
    slot_stride = dense_all.shape[0]
    plan = _route_plan(dense_all, EXPERT_ROW_TILE)
    y2 = _experts(h_all, plan, w_gate, w_up, w_down, tm=EXPERT_ROW_TILE, slot_stride=slot_stride)
    y_prompt = _combine(y2, h_all, g2, b2, tm=ROW_TILE, tile0=0, n_tiles=n_prompt // ROW_TILE,
                        slot_stride=slot_stride)
    y_sample = _combine(y2, h_all, g2, b2, tm=dec_batch, tile0=n_prompt // dec_batch, n_tiles=1,
                        slot_stride=slot_stride)

    heads_t = lambda t: jnp.transpose(t.reshape(batch, N_KV_HEADS, HEAD_DIM, seq), (0, 3, 1, 2))
    pool_sample = jnp.concatenate([state_pool[:, 1:], u_s[:, None, :]], axis=1)
    return (y_prompt.reshape(batch, seq, d_model), y_sample.reshape(dec_batch, 1, d_model),
            heads_t(kt), heads_t(vt), jnp.transpose(kit, (0, 2, 1)), utail[:, 16 - POOL_STATE:],
            k_s.reshape(dec_batch, 1, N_KV_HEADS, HEAD_DIM), v_s.reshape(dec_batch, 1, N_KV_HEADS, HEAD_DIM),
            kiwi_s[:, :IDX_DIM].reshape(dec_batch, 1, IDX_DIM), pool_sample)


def kernel(x_prompt, x_sample, cache_k, cache_v, cache_kidx, state_pool, page_table, w_in, w_pool, pool_scale,
           w_branch_a, w_branch_b, w_out, ln1_g, ln1_b, w_group, w_expert_router, w_gate, w_up, w_down,
           ln2_g, ln2_b):
    assert w_in.shape[0] == DEPTH
    outs = _layer(x_prompt, x_sample, cache_k[0], cache_v[0], cache_kidx[0], state_pool[0], page_table, w_in[0],
                  w_pool[0], pool_scale[0], w_branch_a[0], w_branch_b[0], w_out[0], ln1_g[0], ln1_b[0],
                  w_group[0], w_expert_router[0], w_gate[0], w_up[0], w_down[0], ln2_g[0], ln2_b[0])
    y_p, y_s = outs[0], outs[1]
    return (y_p, y_s) + tuple(o[None] for o in outs[2:])
```

```python
import functools
import math

import jax
import jax.numpy as jnp
import numpy as np
from jax import lax
from jax.experimental import pallas as pl
from jax.experimental.pallas import tpu as pltpu

BF16 = jnp.bfloat16
F32 = jnp.float32
I32 = jnp.int32

PAGE_SIZE = 128
POOL_WINDOWS = (2, 4, 8, 16)
POOL_STATE = 15
N_HEADS = 16
N_KV_HEADS = 4
HEAD_DIM = 64
ROT_DIM = HEAD_DIM // 4
ROPE_THETA = 500000.0
IDX_HEADS = 16
IDX_DIM = 64
IDX_W_SCALE = (IDX_HEADS * IDX_DIM) ** -0.5
TOPK_MAX = 256
N_GROUPS = 4
EXPERTS_PER_GROUP = 8
N_EXPERTS = N_GROUPS * EXPERTS_PER_GROUP
LN_EPS = 1e-5
DEPTH = 1
ALPHA = (2 * DEPTH) ** 0.25

LANES = 128
SUBLANES = 8
Q_TILE = 128
KEY_CHUNK = 512
ROW_TILE = 256
MOE_ROW_TILE = 512
VMEM_LIMIT = 56 * 1024 * 1024

INT_MIN = -2 ** 31
NEG_INF_KEY = int(np.int32(np.uint32(0xFF800000) ^ np.uint32(0x7FFFFFFF)))
NT_DIMS = (((1,), (1,)), ((), ()))


def _sortable_key(x):
    bits = lax.bitcast_convert_type(x, I32)
    return bits ^ ((bits >> 31) & 0x7FFFFFFF)


def _const_spec(shape):
    nd = len(shape)
    return pl.BlockSpec(shape, lambda *_: (0,) * nd, pipeline_mode=pl.Buffered(1))


def _project(xb, wt_ref, cos, nsin, psin, pool_width):
    tm = xb.shape[0]
    lo = lax.broadcasted_iota(I32, (tm, LANES), 1) < HEAD_DIM
    attn_w, kv_w, idx_w = N_HEADS * HEAD_DIM, N_KV_HEADS * HEAD_DIM, IDX_HEADS * IDX_DIM
    off_q = pool_width
    off_k = off_q + attn_w
    off_v = off_k + kv_w
    off_qi = off_v + kv_w
    off_kiwi = off_qi + idx_w

    def proj(c0, width):
        return lax.dot_general(xb, wt_ref[c0:c0 + width, :], NT_DIMS, preferred_element_type=F32)

    def rope(z, c, s1, s2):
        return z * c + pltpu.roll(z, LANES - ROT_DIM // 2, 1) * s1 + pltpu.roll(z, ROT_DIM // 2, 1) * s2

    def rope_tiles(z):
        return [rope(z[:, j * LANES:(j + 1) * LANES], cos, nsin, psin) for j in range(z.shape[1] // LANES)]

    u = proj(0, pool_width)
    q = [t * (HEAD_DIM ** -0.5) for t in rope_tiles(proj(off_q, attn_w))]
    qi = rope_tiles(proj(off_qi, idx_w))
    k = rope_tiles(proj(off_k, kv_w))
    v = proj(off_v, kv_w)
    kw = rope(proj(off_kiwi, LANES), jnp.where(lo, cos, IDX_W_SCALE), jnp.where(lo, nsin, 0.0),
              jnp.where(lo, psin, 0.0))
    return u, q, qi, k, v, kw


def _pool_mix(d, g, wpool_ref, pscale_ref, gw):
    z = jnp.dot(d.astype(BF16), wpool_ref[g], preferred_element_type=F32)
    return z * pscale_ref[:, g * gw:(g + 1) * gw]


def _inproj_body(x_ref, wt_ref, tc_ref, ts1_ref, ts2_ref, wpool_ref, pscale_ref,
                 utail_ref, a_ref, qt_ref, qit_ref, kb_ref, kib_ref, kt_ref, vt_ref, vtb_ref, kit_ref, wt_out_ref,
                 uext_ref, *, tm, tiles_per_batch, pool_width):
    i = pl.program_id(0)
    u, q, qi, k, v, kw = _project(x_ref[...].astype(BF16), wt_ref, tc_ref[...], ts1_ref[...], ts2_ref[...],
                                  pool_width)
    lo = lax.broadcasted_iota(I32, (tm, LANES), 1) < HEAD_DIM
    gw = pool_width // len(POOL_WINDOWS)

    def blocks(tile):
        return [(slice(blk * Q_TILE, (blk + 1) * Q_TILE), tile[blk * Q_TILE:(blk + 1) * Q_TILE].T)
                for blk in range(tm // Q_TILE)]

    def store_head_t(dst_ref, h, tile):
        for blk, (_, t) in enumerate(blocks(tile)):
            dst_ref[blk, :, h * Q_TILE:(h + 1) * Q_TILE] = t.astype(BF16)

    for j, p in enumerate(q):
        pr = pltpu.roll(p, HEAD_DIM, 1)
        if (j // 2) % 2 == 0:
            store_head_t(qt_ref, 2 * j, jnp.where(lo, p, 0.0))
            store_head_t(qt_ref, 2 * j + 1, jnp.where(lo, pr, 0.0))
        else:
            store_head_t(qt_ref, 2 * j, jnp.where(lo, 0.0, pr))
            store_head_t(qt_ref, 2 * j + 1, jnp.where(lo, 0.0, p))
    for j, p in enumerate(qi):
        store_head_t(qit_ref, 2 * j, jnp.where(lo, p, 0.0))
        store_head_t(qit_ref, 2 * j + 1, jnp.where(lo, pltpu.roll(p, IDX_DIM, 1), 0.0))

    for j, kr in enumerate(k):
        lanes = slice(j * LANES, (j + 1) * LANES)
        kb_ref[:, lanes] = kr.astype(BF16)
        for cols, t in blocks(kr):
            kt_ref[0, lanes, cols] = t
        for cols, t in blocks(v[:, lanes]):
            vt_ref[0, lanes, cols] = t
            vtb_ref[lanes, cols] = t.astype(BF16)
    kib_ref[...] = kw.astype(BF16)
    for cols, t in blocks(kw):
        kit_ref[0, :, cols] = t[:IDX_DIM, :]
        wt_out_ref[:, cols] = t[IDX_DIM:IDX_DIM + IDX_HEADS, :]

    first = (i % tiles_per_batch) == 0

    @pl.when(first)
    def _():
        uext_ref[0:16, :] = jnp.zeros((16, pool_width), F32)

    @pl.when(jnp.logical_not(first))
    def _():
        uext_ref[0:16, :] = uext_ref[tm:tm + 16, :]

    uext_ref[16:16 + tm, :] = u
    utail_ref[0] = u[tm - 16:tm, :]
    pos = (i % tiles_per_batch) * tm + lax.broadcasted_iota(I32, (tm, 1), 0)
    for g, w in enumerate(POOL_WINDOWS):
        lanes = slice(g * gw, (g + 1) * gw)
        acc = u[:, lanes]
        for jj in range(1, w):
            acc = acc + uext_ref[16 - jj:16 - jj + tm, lanes]
        cnt = jnp.minimum(w, pos + 1).astype(F32)
        d = acc / cnt - u[:, lanes]
        a_ref[:, lanes] = _pool_mix(d, g, wpool_ref, pscale_ref, gw).astype(BF16)


def _in_projection(x, wt_a, tabs, wpool, pscale, *, batch, seq):
    n_rows, d_model = x.shape
    tm = ROW_TILE
    pool_width = wpool.shape[0] * wpool.shape[1]
    nblk = n_rows // Q_TILE
    kv_w = N_KV_HEADS * HEAD_DIM
    tpb = seq // tm
    row = lambda width: pl.BlockSpec((tm, width), lambda i: (i, 0))
    col = lambda height: pl.BlockSpec((height, tm), lambda i: (0, i))
    per_seq = lambda height: pl.BlockSpec((1, height, tm), lambda i: (i // tpb, 0, i % tpb))
    tab = pl.BlockSpec((tm, LANES), lambda i: (i % tpb, 0))
    hm = pl.BlockSpec((tm // Q_TILE, LANES, N_HEADS * Q_TILE), lambda i: (i, 0, 0))
    body = functools.partial(_inproj_body, tm=tm, tiles_per_batch=tpb, pool_width=pool_width)
    return pl.pallas_call(
        body,
        grid=(n_rows // tm,),
        in_specs=[row(d_model), _const_spec(wt_a.shape), tab, tab, tab,
                  _const_spec(wpool.shape), _const_spec(pscale.shape)],
        out_specs=[pl.BlockSpec((1, 16, pool_width), lambda i: (i // tpb, 0, 0)),
                   row(pool_width), hm, hm, row(kv_w), row(LANES), per_seq(kv_w), per_seq(kv_w), col(kv_w),
                   per_seq(IDX_DIM), col(IDX_HEADS)],
        out_shape=[
            jax.ShapeDtypeStruct((batch, 16, pool_width), F32),
            jax.ShapeDtypeStruct((n_rows, pool_width), BF16),
            jax.ShapeDtypeStruct((nblk, LANES, N_HEADS * Q_TILE), BF16),
            jax.ShapeDtypeStruct((nblk, LANES, IDX_HEADS * Q_TILE), BF16),
            jax.ShapeDtypeStruct((n_rows, kv_w), BF16),
            jax.ShapeDtypeStruct((n_rows, LANES), BF16),
            jax.ShapeDtypeStruct((batch, kv_w, seq), F32),
            jax.ShapeDtypeStruct((batch, kv_w, seq), F32),
            jax.ShapeDtypeStruct((kv_w, n_rows), BF16),
            jax.ShapeDtypeStruct((batch, IDX_DIM, seq), F32),
            jax.ShapeDtypeStruct((IDX_HEADS, n_rows), F32),
        ],
        scratch_shapes=[pltpu.VMEM((tm + 16, pool_width), F32)],
        compiler_params=pltpu.CompilerParams(dimension_semantics=("arbitrary",), vmem_limit_bytes=VMEM_LIMIT),
        name="in_projection",
    )(x, wt_a, *tabs, wpool, pscale)


def _inproj_sample_body(x_ref, wt_ref, tc_ref, ts1_ref, ts2_ref, wpool_ref, pscale_ref, state_ref,
                        u_ref, a_ref, q_ref, qi_ref, k_ref, v_ref, kiwi_ref, *, pool_width, sample_pos):
    u, q, qi, k, v, kw = _project(x_ref[...].astype(BF16), wt_ref, tc_ref[...], ts1_ref[...], ts2_ref[...],
                                  pool_width)
    gw = pool_width // len(POOL_WINDOWS)
    u_ref[...] = u
    for j, t in enumerate(q):
        q_ref[:, j * LANES:(j + 1) * LANES] = t.astype(BF16)
    for j, t in enumerate(qi):
        qi_ref[:, j * LANES:(j + 1) * LANES] = t.astype(BF16)
    for j, t in enumerate(k):
        k_ref[:, j * LANES:(j + 1) * LANES] = t
    v_ref[...] = v
    kiwi_ref[...] = kw
    for g, w in enumerate(POOL_WINDOWS):
        lanes = slice(g * gw, (g + 1) * gw)
        acc = u[:, lanes]
        for jj in range(1, w):
            acc = acc + state_ref[POOL_STATE - jj, :, lanes]
        d = acc / float(min(w, sample_pos + 1)) - u[:, lanes]
        a_ref[:, lanes] = _pool_mix(d, g, wpool_ref, pscale_ref, gw).astype(BF16)


def _in_projection_sample(x, wt_a, tabs, wpool, pscale, state_t, *, sample_pos):
    n_rows, d_model = x.shape
    pool_width = wpool.shape[0] * wpool.shape[1]
    kv_w = N_KV_HEADS * HEAD_DIM
    full = lambda shape: pl.BlockSpec(shape, lambda i: (0,) * len(shape))
    body = functools.partial(_inproj_sample_body, pool_width=pool_width, sample_pos=sample_pos)
    out_shape = [
        jax.ShapeDtypeStruct((n_rows, pool_width), F32),
        jax.ShapeDtypeStruct((n_rows, pool_width), BF16),
        jax.ShapeDtypeStruct((n_rows, N_HEADS * HEAD_DIM), BF16),
        jax.ShapeDtypeStruct((n_rows, IDX_HEADS * IDX_DIM), BF16),
        jax.ShapeDtypeStruct((n_rows, kv_w), F32),
        jax.ShapeDtypeStruct((n_rows, kv_w), F32),
        jax.ShapeDtypeStruct((n_rows, LANES), F32),
    ]
    return pl.pallas_call(
        body,
        grid=(1,),
        in_specs=[full(x.shape), _const_spec(wt_a.shape), full(tabs[0].shape), full(tabs[1].shape),
                  full(tabs[2].shape), _const_spec(wpool.shape), _const_spec(pscale.shape), full(state_t.shape)],
        out_specs=[full(s.shape) for s in out_shape],
        out_shape=out_shape,
        compiler_params=pltpu.CompilerParams(dimension_semantics=("arbitrary",), vmem_limit_bytes=VMEM_LIMIT),
        name="in_projection_sample",
    )(x, wt_a, *tabs, wpool, pscale, state_t)


def _rope_tables(positions):
    pos = np.asarray(positions, np.float32)
    n = pos.shape[0]
    half = ROT_DIM // 2
    inv = np.exp(-np.arange(half, dtype=np.float32) * np.float32(math.log(ROPE_THETA) / half)).astype(np.float32)
    ang = (pos[:, None] * inv[None, :]).astype(np.float32).astype(np.float64)
    cos, sin = np.cos(ang).astype(np.float32), np.sin(ang).astype(np.float32)
    zeros = np.zeros((n, half), np.float32)
    rest = HEAD_DIM - ROT_DIM
    c = np.concatenate([cos, cos, np.ones((n, rest), np.float32)], axis=1)
    s1 = np.concatenate([-sin, zeros, np.zeros((n, rest), np.float32)], axis=1)
    s2 = np.concatenate([zeros, sin, np.zeros((n, rest), np.float32)], axis=1)
    return tuple(jnp.asarray(np.concatenate([t, t], axis=1)) for t in (c, s1, s2))


def _topk_threshold(count_ge, init, topk):
    def step(it, t):
        cand = t + jnp.left_shift(jnp.int32(1), 31 - it)
        return jnp.where(count_ge(cand) >= topk, cand, t)
    return lax.fori_loop(0, 32, step, init)


def _fold_slabs(x, op):
    assert x.shape[0] & (x.shape[0] - 1) == 0
    while x.shape[0] > 1:
        half = x.shape[0] // 2
        x = op(x[:half], x[half:])
    return x[0]


def _prompt_attn_body(qit_ref, wt_ref, qt_ref, kib_ref, kb_ref, vt_ref, o_ref,
                      sc_ref, thr_ref, tie_ref, m_ref, l_ref, acc_ref, *, topk):
    i = pl.program_id(1)
    tq, kc = Q_TILE, KEY_CHUNK
    seq = kib_ref.shape[0]
    index_bits = seq.bit_length()
    n_keys = (i + 1) * tq
    n_chunks = (n_keys + kc - 1) // kc
    qpos = i * tq + lax.broadcasted_iota(I32, (1, tq), 1)
    group = N_HEADS // N_KV_HEADS
    gcols = group * tq

    def score_chunk(c, carry):
        k0 = pl.multiple_of(c * kc, kc)
        kchunk = kib_ref[pl.ds(k0, kc), :]
        acc = jnp.zeros((kc, tq), F32)
        for hg in range(IDX_HEADS // group):
            d = jnp.dot(kchunk, qit_ref[0, :, hg * gcols:(hg + 1) * gcols], preferred_element_type=F32)
            for r in range(group):
                h = hg * group + r
                acc = acc + jnp.maximum(d[:, r * tq:(r + 1) * tq], 0.0) * wt_ref[h:h + 1, :]
        kpos = k0 + lax.broadcasted_iota(I32, (kc, 1), 0)
        sc_ref[pl.ds(k0, kc), :] = _sortable_key(jnp.where(kpos <= qpos, acc, -jnp.inf))
        return carry

    lax.fori_loop(0, n_chunks, score_chunk, 0)

    thr_ref[...] = jnp.full((1, tq), NEG_INF_KEY + 1, I32)
    tie_ref[...] = jnp.full((1, tq), seq, I32)

    @pl.when(n_keys > topk)
    def _():
        def count(pred):
            def blk(c, acc):
                k0 = pl.multiple_of(c * kc, kc)
                keys = sc_ref[pl.ds(k0, kc), :].reshape(kc // SUBLANES, SUBLANES, tq)
                kpos = k0 + lax.broadcasted_iota(I32, (kc, 1), 0).reshape(kc // SUBLANES, SUBLANES, 1)
                return acc + _fold_slabs(jnp.where(pred(keys, kpos), 1.0, 0.0), jnp.add)
            acc = lax.fori_loop(0, n_chunks, blk, jnp.zeros((SUBLANES, tq), F32))
            return jnp.sum(acc, axis=0, keepdims=True)

        def count_ge(cand):
            candb = jnp.broadcast_to(cand, (SUBLANES, tq))
            return count(lambda keys, kpos: keys >= candb)

        t = _topk_threshold(count_ge, jnp.full((1, tq), INT_MIN, I32), float(topk))
        t = jnp.where(qpos + 1 > topk, t, NEG_INF_KEY + 1)
        thr_ref[...] = t
        n_above = count_ge(t + 1)
        need = float(topk) - n_above
        tied = (count_ge(t) - n_above > need) & (qpos + 1 > topk)

        @pl.when(jnp.max(jnp.where(tied, 1.0, 0.0)) > 0.0)
        def _():
            tb = jnp.broadcast_to(t, (SUBLANES, tq))

            def step(it, d):
                cand = d + jnp.left_shift(jnp.int32(1), index_bits - 1 - it)
                candb = jnp.broadcast_to(cand, (SUBLANES, tq))
                before = count(lambda keys, kpos: (keys == tb) & (kpos < candb))
                return jnp.where(before < need, cand, d)
            last = lax.fori_loop(0, index_bits, step, jnp.zeros((1, tq), I32))
            tie_ref[...] = jnp.where(tied, last, seq)

    m_ref[...] = jnp.full(m_ref.shape, -jnp.inf, F32)
    l_ref[...] = jnp.zeros(l_ref.shape, F32)
    acc_ref[...] = jnp.zeros(acc_ref.shape, F32)

    def attn_chunk(c, carry):
        k0 = pl.multiple_of(c * kc, kc)
        keys = sc_ref[pl.ds(k0, kc), :]
        kpos = k0 + lax.broadcasted_iota(I32, (kc, 1), 0)
        sel = (keys > thr_ref[...]) | ((keys == thr_ref[...]) & (kpos <= tie_ref[...]))
        for g in range(N_KV_HEADS):
            pair = slice((g // 2) * LANES, (g // 2 + 1) * LANES)
            cols = slice(g * gcols, (g + 1) * gcols)
            s = jnp.dot(kb_ref[pl.ds(k0, kc), pair], qt_ref[0, :, cols], preferred_element_type=F32)
            s = jnp.concatenate([jnp.where(sel, s[:, r * tq:(r + 1) * tq], -jnp.inf) for r in range(group)], axis=1)
            m_old = m_ref[:, cols]
            m_new = jnp.maximum(m_old, jnp.max(s, axis=0, keepdims=True))
            m_safe = jnp.where(m_new == -jnp.inf, 0.0, m_new)
            alpha = jnp.exp(m_old - m_safe)
            p = jnp.exp(s - m_safe)
            l_ref[:, cols] = alpha * l_ref[:, cols] + jnp.sum(p, axis=0, keepdims=True)
            pv = jnp.dot(vt_ref[pair, pl.ds(k0, kc)], p.astype(BF16), preferred_element_type=F32)
            acc_ref[g] = alpha * acc_ref[g] + pv
            m_ref[:, cols] = m_new
        return carry

    lax.fori_loop(0, n_chunks, attn_chunk, 0)
    l_all = l_ref[...]

    for j in range(N_HEADS // 2):
        halves = []
        for h in (2 * j, 2 * j + 1):
            g, r = h // group, h % group
            rows = slice((g % 2) * HEAD_DIM, (g % 2 + 1) * HEAD_DIM)
            cols = slice(r * tq, (r + 1) * tq)
            halves.append(acc_ref[g, rows, cols] / l_all[:, g * gcols + r * tq:g * gcols + (r + 1) * tq])
        o_ref[:, j * LANES:(j + 1) * LANES] = jnp.concatenate(halves, axis=0).T.astype(BF16)


def _prompt_attention(qit, wt, qt, kib, kb, vt, *, batch, seq):
    nq = seq // Q_TILE
    topk = min(TOPK_MAX, seq // 4)
    kv_w = N_KV_HEADS * HEAD_DIM
    per_q = lambda b, i: (b * nq + i, 0, 0)
    body = functools.partial(_prompt_attn_body, topk=topk)
    return pl.pallas_call(
        body,
        grid=(batch, nq),
        in_specs=[
            pl.BlockSpec((1, LANES, IDX_HEADS * Q_TILE), per_q),
            pl.BlockSpec((IDX_HEADS, Q_TILE), lambda b, i: (0, b * nq + i)),
            pl.BlockSpec((1, LANES, N_HEADS * Q_TILE), per_q),
            pl.BlockSpec((seq, LANES), lambda b, i: (b, 0)),
            pl.BlockSpec((seq, kv_w), lambda b, i: (b, 0)),
            pl.BlockSpec((kv_w, seq), lambda b, i: (0, b)),
        ],
        out_specs=pl.BlockSpec((Q_TILE, N_HEADS * HEAD_DIM), lambda b, i: (b * nq + i, 0)),
        out_shape=jax.ShapeDtypeStruct((batch * seq, N_HEADS * HEAD_DIM), BF16),
        scratch_shapes=[
            pltpu.VMEM((seq, Q_TILE), I32),
            pltpu.VMEM((1, Q_TILE), I32),
            pltpu.VMEM((1, Q_TILE), I32),
            pltpu.VMEM((1, N_HEADS * Q_TILE), F32),
            pltpu.VMEM((1, N_HEADS * Q_TILE), F32),
            pltpu.VMEM((N_KV_HEADS, LANES, (N_HEADS // N_KV_HEADS) * Q_TILE), F32),
        ],
        compiler_params=pltpu.CompilerParams(dimension_semantics=("arbitrary", "arbitrary"),
                                             vmem_limit_bytes=VMEM_LIMIT),
        name="prompt_attention",
    )(qit, wt, qt, kib, kb, vt)


SAMPLE_PAGES_PER_CHUNK = 16


def _sample_attn_body(pt_ref, qi_ref, w_ref, qs_ref, kicur_ref, kcur_ref, vcur_ref, ckidx_hbm, ck_hbm, cv_hbm,
                      o_ref, kidx_buf, k_buf, v_buf, sc_ref, sem_idx, sem_k, sem_v, *, n_pages, topk):
    b = pl.program_id(0)
    final_step = pl.num_programs(0) - 1
    ppc = SAMPLE_PAGES_PER_CHUNK
    n_chunks = n_pages // ppc
    ck = ppc * PAGE_SIZE
    seg = sc_ref.shape[1]
    segs_per_chunk = ck // seg

    def idx_copy(seq, p):
        return pltpu.make_async_copy(ckidx_hbm.at[pt_ref[seq, p]],
                                     kidx_buf.at[seq % 2, :, pl.ds(p * PAGE_SIZE, PAGE_SIZE)], sem_idx.at[seq % 2])

    def kv_copies(seq, c, p):
        slot = c % 2
        page = pt_ref[seq, c * ppc + p]
        window = pl.ds(p * PAGE_SIZE, PAGE_SIZE)
        return (pltpu.make_async_copy(ck_hbm.at[page], k_buf.at[slot, :, window], sem_k.at[slot]),
                pltpu.make_async_copy(cv_hbm.at[page], v_buf.at[slot, :, window], sem_v.at[slot]))

    def start_idx(seq):
        for p in range(n_pages):
            idx_copy(seq, p).start()

    def start_kv(seq, c):
        for p in range(ppc):
            for cp in kv_copies(seq, c, p):
                cp.start()

    def wait_kv(seq, c):
        for p in range(ppc):
            for cp in kv_copies(seq, c, p):
                cp.wait()

    @pl.when(b == 0)
    def _():
        start_idx(b)
        start_kv(b, 0)

    for p in range(n_pages):
        idx_copy(b, p).wait()

    qi = qi_ref[0][:, :IDX_DIM]
    w = w_ref[0]

    def head_mix(d):
        return jnp.sum(jnp.maximum(d, 0.0) * w, axis=0, keepdims=True)

    def bf16_products(a, row):
        return jnp.sum(a.astype(F32) * row.astype(BF16).astype(F32), axis=1, keepdims=True)

    for c in range(n_chunks):
        d = jnp.dot(qi, kidx_buf[b % 2, :, c * ck:(c + 1) * ck].astype(BF16), preferred_element_type=F32)
        keys = _sortable_key(head_mix(d))
        for r in range(segs_per_chunk):
            sc_ref[c * segs_per_chunk + r:c * segs_per_chunk + r + 1, :] = keys[:, r * seg:(r + 1) * seg]
    cur_key = _sortable_key(head_mix(bf16_products(qi, kicur_ref[0][:, :IDX_DIM])))

    vreg = (SUBLANES, LANES)
    past = n_pages * PAGE_SIZE
    if past + 1 > topk:
        cur_key_v = jnp.broadcast_to(cur_key, vreg)

        def total(hits):
            return jnp.broadcast_to(jnp.sum(jnp.sum(hits, axis=1, keepdims=True), axis=0, keepdims=True), vreg)

        def count_ge(cand):
            hits = jnp.where(cur_key_v >= cand, 1.0 / (SUBLANES * LANES), 0.0)
            for j in range(seg // LANES):
                hits = hits + jnp.where(sc_ref[:, j * LANES:(j + 1) * LANES] >= cand, 1.0, 0.0)
            return total(hits)

        thr_v = _topk_threshold(count_ge, jnp.full(vreg, INT_MIN, I32), float(topk))
        need = float(topk) - count_ge(thr_v + 1)
        pos0 = lax.broadcasted_iota(I32, vreg, 0) * seg + lax.broadcasted_iota(I32, vreg, 1)
        index_bits = (past + 1).bit_length()

        def step(it, d):
            cand = d + jnp.left_shift(jnp.int32(1), index_bits - 1 - it)
            hits = jnp.zeros(vreg, F32)
            for j in range(seg // LANES):
                tied = (sc_ref[:, j * LANES:(j + 1) * LANES] == thr_v) & (pos0 + j * LANES < cand)
                hits = hits + jnp.where(tied, 1.0, 0.0)
            return jnp.where(total(hits) < need, cand, d)
        thr = thr_v[0:1, 0:1]
        last = lax.fori_loop(0, index_bits, step, jnp.zeros(vreg, I32))[0:1, 0:1]
    else:
        thr = jnp.full((1, 1), NEG_INF_KEY + 1, I32)
        last = jnp.full((1, 1), past, I32)

    def selected(keys, kpos):
        return (keys > thr) | ((keys == thr) & (kpos <= last))

    qs = qs_ref[0]
    m = jnp.full((N_HEADS, 1), -jnp.inf, F32)
    l = jnp.zeros((N_HEADS, 1), F32)
    acc = jnp.zeros((N_HEADS, N_KV_HEADS * HEAD_DIM), F32)

    def update(m, l, acc, s, sel, pv):
        s = jnp.where(sel, s, -jnp.inf)
        m_new = jnp.maximum(m, jnp.max(s, axis=1, keepdims=True))
        m_safe = jnp.where(m_new == -jnp.inf, 0.0, m_new)
        alpha = jnp.exp(m - m_safe)
        p = jnp.exp(s - m_safe)
        l = alpha * l + jnp.sum(p, axis=1, keepdims=True)
        acc = alpha * acc + pv(p.astype(BF16))
        return m_new, l, acc

    def prefetch_next_sequence():
        @pl.when(b < final_step)
        def _():
            start_idx(b + 1)
            start_kv(b + 1, 0)

    slot0_free_early = n_chunks % 2 == 0
    for c in range(n_chunks):
        if c + 1 < n_chunks:
            start_kv(b, c + 1)
        elif slot0_free_early:
            prefetch_next_sequence()
        wait_kv(b, c)
        slot = c % 2
        s = jnp.dot(qs, k_buf[slot].astype(BF16), preferred_element_type=F32)
        vt = v_buf[slot].astype(BF16)
        keys = jnp.concatenate([sc_ref[c * segs_per_chunk + r:c * segs_per_chunk + r + 1, :]
                                for r in range(segs_per_chunk)], axis=1)
        sel = selected(keys, c * ck + lax.broadcasted_iota(I32, (1, ck), 1))
        m, l, acc = update(m, l, acc, s, sel, lambda p: lax.dot_general(p, vt, NT_DIMS, preferred_element_type=F32))
    if not slot0_free_early:
        prefetch_next_sequence()
    v_cur = vcur_ref[0].astype(BF16).astype(F32)
    m, l, acc = update(m, l, acc, bf16_products(qs, kcur_ref[0]), selected(cur_key, past),
                       lambda p: p.astype(F32) * v_cur)
    o_ref[0] = acc / l


def _keys_minor(cache):
    pages, page_size = cache.shape[:2]
    return jnp.moveaxis(cache, 1, -1).reshape(pages, -1, page_size)


def _sample_attention(page_table, qi_s, w_s, qs, kicur, kcur, vcur, cache_kidx, cache_k, cache_v):
    dec_batch, n_pages = page_table.shape
    past = n_pages * PAGE_SIZE
    topk = min(TOPK_MAX, (past + 1) // 4)
    kv_w = N_KV_HEADS * HEAD_DIM
    ppc = SAMPLE_PAGES_PER_CHUNK
    per_b = lambda b, pt: (b, 0, 0)
    any_spec = pl.BlockSpec(memory_space=pl.ANY)
    grid_spec = pltpu.PrefetchScalarGridSpec(
        num_scalar_prefetch=1,
        grid=(dec_batch,),
        in_specs=[
            pl.BlockSpec((1, IDX_HEADS, LANES), per_b),
            pl.BlockSpec((1, IDX_HEADS, 1), per_b),
            pl.BlockSpec((1, N_HEADS, kv_w), per_b),
            pl.BlockSpec((1, 1, LANES), per_b),
            pl.BlockSpec((1, 1, kv_w), per_b),
            pl.BlockSpec((1, 1, kv_w), per_b),
            any_spec, any_spec, any_spec,
        ],
        out_specs=pl.BlockSpec((1, N_HEADS, kv_w), per_b),
        scratch_shapes=[
            pltpu.VMEM((2, IDX_DIM, past), F32),
            pltpu.VMEM((2, kv_w, ppc * PAGE_SIZE), F32),
            pltpu.VMEM((2, kv_w, ppc * PAGE_SIZE), F32),
            pltpu.VMEM((SUBLANES, past // SUBLANES), I32),
            pltpu.SemaphoreType.DMA((2,)),
            pltpu.SemaphoreType.DMA((2,)),
            pltpu.SemaphoreType.DMA((2,)),
        ],
    )
    body = functools.partial(_sample_attn_body, n_pages=n_pages, topk=topk)
    return pl.pallas_call(
        body,
        grid_spec=grid_spec,
        out_shape=jax.ShapeDtypeStruct((dec_batch, N_HEADS, kv_w), F32),
        compiler_params=pltpu.CompilerParams(dimension_semantics=("arbitrary",), vmem_limit_bytes=VMEM_LIMIT),
        name="sample_attention",
    )(page_table, qi_s, w_s, qs, kicur, kcur, vcur, cache_kidx, cache_k, cache_v)


def _mm(a, b, precise, *, nt=False):
    dims = NT_DIMS if nt else (((1,), (0,)), ((), ()))
    if precise:
        return lax.dot_general(a.astype(F32), b, dims, precision=lax.Precision.HIGHEST, preferred_element_type=F32)
    return lax.dot_general(a.astype(BF16), b, dims, preferred_element_type=F32)


def _operand_dtype(precise):
    return F32 if precise else BF16


MERGE_COL_CHUNK = 512


def _merge_body(x_ref, a_ref, o_ref, wgat_ref, wgbt_ref, wba_ref, wbb_ref, out_ref, *, precise):
    x, a, o = x_ref[...], a_ref[...], o_ref[...]
    cc = MERGE_COL_CHUNK
    for c in range(out_ref.shape[1] // cc):
        cols = slice(c * cc, (c + 1) * cc)
        ga = _mm(x, wgat_ref[cols, :], precise, nt=True)
        gb = _mm(x, wgbt_ref[cols, :], precise, nt=True)
        ya = _mm(a, wba_ref[:, cols], precise)
        yo = _mm(o, wbb_ref[:, cols], precise)
        out_ref[:, cols] = (jax.nn.sigmoid(ga) * ya + jax.nn.sigmoid(gb) * yo).astype(out_ref.dtype)


def _merge(x, a, o, wgat, wgbt, wba, wbb, *, tm, col_block, precise):
    n_rows, d_model = x.shape
    mode = dict(pipeline_mode=pl.Buffered(1)) if col_block == d_model else {}
    gate_spec = pl.BlockSpec((col_block, d_model), lambda i, j: (j, 0), **mode)
    branch_spec = lambda w: pl.BlockSpec((w.shape[0], col_block), lambda i, j: (0, j), **mode)
    row = lambda width: pl.BlockSpec((tm, width), lambda i, j: (i, 0))
    body = functools.partial(_merge_body, precise=precise)
    return pl.pallas_call(
        body,
        grid=(n_rows // tm, d_model // col_block),
        in_specs=[row(d_model), row(a.shape[1]), row(o.shape[1]), gate_spec, gate_spec,
                  branch_spec(wba), branch_spec(wbb)],
        out_specs=pl.BlockSpec((tm, col_block), lambda i, j: (i, j)),
        out_shape=jax.ShapeDtypeStruct((n_rows, d_model), _operand_dtype(precise)),
        compiler_params=pltpu.CompilerParams(dimension_semantics=("arbitrary", "arbitrary"),
                                             vmem_limit_bytes=VMEM_LIMIT),
        name="branch_merge",
    )(x, a, o, wgat, wgbt, wba, wbb)


def _layer_norm(x, g, b):
    mu = jnp.mean(x, axis=-1, keepdims=True)
    var = jnp.mean(jnp.square(x - mu), axis=-1, keepdims=True)
    return (x - mu) * lax.rsqrt(var + LN_EPS) * g + b


def _route(logits):
    lane = lax.broadcasted_iota(I32, logits.shape, 1)
    lane_f = lane.astype(F32)
    big = float(LANES)

    def masked_softmax(mask):
        x = jnp.where(mask, logits, -jnp.inf)
        e = jnp.exp(x - jnp.max(x, axis=1, keepdims=True))
        return jnp.where(mask, e / jnp.sum(e, axis=1, keepdims=True), -1.0)

    def top1(p):
        best = jnp.max(p, axis=1, keepdims=True)
        idx = jnp.min(jnp.where(p == best, lane_f, big), axis=1, keepdims=True)
        return best, idx

    gp = masked_softmax((lane >= N_EXPERTS) & (lane < N_EXPERTS + N_GROUPS))
    g_p, g_lane = top1(gp)
    g_idx = g_lane.astype(I32) - N_EXPERTS
    ep = masked_softmax((lane < N_EXPERTS) & ((lane >> int(math.log2(EXPERTS_PER_GROUP))) == g_idx))
    p1, i1 = top1(ep)
    p2, i2 = top1(jnp.where(lane_f == i1, -1.0, ep))
    denom = p1 + p2
    return jnp.where(lane_f == i1, g_p * p1 / denom, 0.0) + jnp.where(lane_f == i2, g_p * p2 / denom, 0.0)


def _store_token_major(ref, x):
    chunks = x.shape[1] // LANES
    for c in range(chunks):
        ref[pl.ds(c, x.shape[0], stride=chunks), :] = x[:, c * LANES:(c + 1) * LANES]


def _load_token_major(ref, n_tokens):
    chunks = ref.shape[0] // n_tokens
    return jnp.concatenate([ref[pl.ds(c, n_tokens, stride=chunks), :] for c in range(chunks)], axis=1)


def _outproj_body(m_ref, x_ref, wout_ref, g_ref, b_ref, wr_ref, h_ref, dense_ref, *, precise, n_real):
    i = pl.program_id(0)

    @pl.when(i < n_real)
    def _():
        mo = _mm(m_ref[...], wout_ref[...], precise)
        h = _layer_norm(ALPHA * x_ref[...] + mo, g_ref[...], b_ref[...])
        _store_token_major(h_ref, h)
        dense_ref[...] = _route(_mm(h, wr_ref[...], precise))

    @pl.when(i >= n_real)
    def _():
        h_ref[...] = jnp.zeros(h_ref.shape, F32)
        dense_ref[...] = jnp.zeros(dense_ref.shape, F32)


def _out_projection(merged, x, wout, ln_g, ln_b, w_route, *, tm, precise, zero_tiles=0):
    n_rows, d_model = x.shape
    n_real = n_rows // tm
    chunks = d_model // LANES
    body = functools.partial(_outproj_body, precise=precise, n_real=n_real)
    in_row = lambda width: pl.BlockSpec((tm, width), lambda i: (jnp.minimum(i, n_real - 1), 0))
    n_out = n_rows + zero_tiles * tm
    return pl.pallas_call(
        body,
        grid=(n_real + zero_tiles,),
        in_specs=[in_row(d_model), in_row(d_model), _const_spec(wout.shape),
                  _const_spec(ln_g.shape), _const_spec(ln_b.shape), _const_spec(w_route.shape)],
        out_specs=[pl.BlockSpec((tm * chunks, LANES), lambda i: (i, 0)), pl.BlockSpec((tm, LANES), lambda i: (i, 0))],
        out_shape=[jax.ShapeDtypeStruct((n_out * chunks, LANES), F32),
                   jax.ShapeDtypeStruct((n_out, LANES), F32)],
        compiler_params=pltpu.CompilerParams(dimension_semantics=("arbitrary",), vmem_limit_bytes=VMEM_LIMIT),
        name="out_projection",
    )(merged, x, wout, ln_g, ln_b, w_route)


EXPERT_ROW_TILE = 256
TOP_K_EXPERTS = 2


def _route_plan(dense, tm):
    n_tok = dense.shape[0]
    n_asg = TOP_K_EXPERTS * n_tok
    n_tiles = (n_asg + N_EXPERTS * (tm - 1)) // tm + 1
    w2, e2 = lax.top_k(dense[:, :N_EXPERTS], TOP_K_EXPERTS)
    eflat = e2.reshape(n_asg).astype(I32)
    w_bits = lax.bitcast_convert_type(w2.reshape(n_asg), I32)
    e_sorted, a_sorted, w_sorted = lax.sort((eflat, jnp.arange(n_asg, dtype=I32), w_bits), num_keys=1, is_stable=True)
    starts = jnp.searchsorted(e_sorted, jnp.arange(N_EXPERTS + 1, dtype=I32), side="left").astype(I32)
    counts = starts[1:] - starts[:-1]
    tiles_e = (counts + tm - 1) // tm
    tile_end = jnp.cumsum(tiles_e).astype(I32)
    tile_start = tile_end - tiles_e
    tile = jnp.arange(n_tiles, dtype=I32)
    tile_expert = jnp.minimum(jnp.searchsorted(tile_end, tile, side="right"), N_EXPERTS - 1).astype(I32)
    tile_used = (tile < tile_end[-1]).astype(I32)
    q = ((tile - tile_start[tile_expert]) * tm)[:, None] + jnp.arange(tm, dtype=I32)[None, :]
    valid = ((q < counts[tile_expert][:, None]) & (tile_used[:, None] > 0)).reshape(n_tiles * tm)
    pos = jnp.clip(starts[tile_expert][:, None] + q, 0, n_asg - 1).reshape(n_tiles * tm)
    picked = jnp.stack([a_sorted, w_sorted], axis=1)[pos]
    a = picked[:, 0]
    src = jnp.where(valid, a // TOP_K_EXPERTS, 0)
    slot = a % TOP_K_EXPERTS
    weight = jnp.where(valid, lax.bitcast_convert_type(picked[:, 1], F32), 0.0)
    return src, slot, valid, weight, tile_expert, tile_used


def _expert_body(te_ref, used_ref, src_ref, dst_ref, cw_ref, wg_ref, wu_ref, wd_ref, h_hbm, y_hbm,
                 x_buf, y_buf, sem_in, sem_out, *, tm):
    j = pl.program_id(0)
    chunks = x_buf.shape[0] // tm

    def token_rows(t):
        return pl.ds(pl.multiple_of(t * chunks, chunks), chunks)

    def gather(r):
        return pltpu.make_async_copy(h_hbm.at[token_rows(src_ref[0, 0, r]), :], x_buf.at[token_rows(r), :],
                                     sem_in.at[0])

    def scatter(r):
        return pltpu.make_async_copy(y_buf.at[token_rows(r), :], y_hbm.at[token_rows(dst_ref[0, 0, r]), :],
                                     sem_out.at[0])

    @pl.when(j == 0)
    def _():
        y_buf[...] = jnp.zeros(y_buf.shape, F32)
        pad_rows = pltpu.make_async_copy(y_buf, y_hbm.at[pl.ds(y_hbm.shape[0] - tm * chunks, tm * chunks), :],
                                         sem_out.at[0])
        pad_rows.start()
        pad_rows.wait()

    @pl.when(used_ref[j] > 0)
    def _():
        for r in range(tm):
            gather(r).start()
        for r in range(tm):
            gather(r).wait()
        xb = _load_token_major(x_buf, tm).astype(BF16)
        gate = jnp.dot(xb, wg_ref[0].astype(BF16), preferred_element_type=F32)
        up = jnp.dot(xb, wu_ref[0].astype(BF16), preferred_element_type=F32)
        hh = (gate * jax.nn.sigmoid(gate)) * up * cw_ref[0]
        _store_token_major(y_buf, jnp.dot(hh.astype(BF16), wd_ref[0].astype(BF16), preferred_element_type=F32))
        for r in range(tm):
            scatter(r).start()
        for r in range(tm):
            scatter(r).wait()


def _experts(h_all, plan, wg, wu, wd, *, tm, slot_stride):
    src, slot, valid, weight, tile_expert, tile_used = plan
    n_experts, d_model, d_expert = wg.shape
    chunks = d_model // LANES
    n_tiles = tile_expert.shape[0]
    pad_row = 2 * slot_stride + (jnp.arange(n_tiles * tm, dtype=I32) % tm)
    dst = jnp.where(valid, slot * slot_stride + src, pad_row)
    per_tile = lambda a: a.reshape(n_tiles, 1, tm)
    smem_rows = pl.BlockSpec((1, 1, tm), lambda j, te, used: (j, 0, 0), memory_space=pltpu.SMEM)
    any_spec = pl.BlockSpec(memory_space=pl.ANY)
    grid_spec = pltpu.PrefetchScalarGridSpec(
        num_scalar_prefetch=2,
        grid=(n_tiles,),
        in_specs=[smem_rows, smem_rows,
                  pl.BlockSpec((1, tm, 1), lambda j, te, used: (j, 0, 0)),
                  pl.BlockSpec((1, d_model, d_expert), lambda j, te, used: (te[j], 0, 0)),
                  pl.BlockSpec((1, d_model, d_expert), lambda j, te, used: (te[j], 0, 0)),
                  pl.BlockSpec((1, d_expert, d_model), lambda j, te, used: (te[j], 0, 0)),
                  any_spec],
        out_specs=any_spec,
        scratch_shapes=[pltpu.VMEM((tm * chunks, LANES), F32), pltpu.VMEM((tm * chunks, LANES), F32),
                        pltpu.SemaphoreType.DMA((1,)), pltpu.SemaphoreType.DMA((1,))],
    )
    return pl.pallas_call(
        functools.partial(_expert_body, tm=tm),
        grid_spec=grid_spec,
        out_shape=jax.ShapeDtypeStruct(((2 * slot_stride + tm) * chunks, LANES), F32),
        compiler_params=pltpu.CompilerParams(dimension_semantics=("arbitrary",), vmem_limit_bytes=VMEM_LIMIT),
        name="expert_mlp",
    )(tile_expert, tile_used, per_tile(src), per_tile(dst), weight.reshape(n_tiles, tm, 1), wg, wu, wd, h_all)


def _combine_body(y0_ref, y1_ref, h_ref, g_ref, b_ref, out_ref):
    tm = out_ref.shape[0]
    f = _load_token_major(y0_ref, tm) + _load_token_major(y1_ref, tm)
    out_ref[...] = _layer_norm(ALPHA * _load_token_major(h_ref, tm) + f, g_ref[...], b_ref[...])


def _combine(y, h_all, ln_g, ln_b, *, tm, tile0, n_tiles, slot_stride):
    d_model = ln_g.shape[1]
    chunks = d_model // LANES
    slot = lambda k: pl.BlockSpec((tm * chunks, LANES), lambda i: (i + tile0 + k * (slot_stride // tm), 0))
    return pl.pallas_call(
        _combine_body,
        grid=(n_tiles,),
        in_specs=[slot(0), slot(1), slot(0), _const_spec(ln_g.shape), _const_spec(ln_b.shape)],
        out_specs=pl.BlockSpec((tm, d_model), lambda i: (i, 0)),
        out_shape=jax.ShapeDtypeStruct((n_tiles * tm, d_model), F32),
        compiler_params=pltpu.CompilerParams(dimension_semantics=("arbitrary",), vmem_limit_bytes=VMEM_LIMIT),
        name="expert_combine",
    )(y, y, h_all, ln_g, ln_b)


def _layer(x_prompt, x_sample, cache_k, cache_v, cache_kidx, state_pool, page_table, w_in, w_pool, pool_scale,
           w_branch_a, w_branch_b, w_out, ln1_g, ln1_b, w_group, w_expert_router, w_gate, w_up, w_down,
           ln2_g, ln2_b):
    batch, seq, d_model = x_prompt.shape
    dec_batch = x_sample.shape[0]
    n_pages = page_table.shape[1]
    past = n_pages * PAGE_SIZE
    n_prompt = batch * seq
    kv_w = N_KV_HEADS * HEAD_DIM
    attn_w = N_HEADS * HEAD_DIM
    pool_width = w_pool.shape[0] * w_pool.shape[1]
    group = N_HEADS // N_KV_HEADS
    assert x_sample.shape[1] == 1 and seq % KEY_CHUNK == 0 and seq % MOE_ROW_TILE == 0
    assert dec_batch % 16 == 0 and n_pages % SAMPLE_PAGES_PER_CHUNK == 0
    assert (SAMPLE_PAGES_PER_CHUNK * PAGE_SIZE) % (past // SUBLANES) == 0

    x_p = x_prompt.reshape(n_prompt, d_model)
    x_s = x_sample.reshape(dec_batch, d_model)

    w_t = jnp.transpose(w_in)
    a_width = w_in.shape[1] - 2 * d_model
    wt_a = jnp.concatenate([w_t[:a_width], jnp.zeros((-a_width % LANES, d_model), F32)], axis=0).astype(BF16)
    wgat, wgbt = w_t[a_width:a_width + d_model], w_t[a_width + d_model:]
    wpool_b, pscale = w_pool.astype(BF16), pool_scale.reshape(1, pool_width)

    utail, a_p, qt, qit, kb, kib, kt, vt, vtb, kit, wt = _in_projection(
        x_p, wt_a, _rope_tables(np.arange(seq)), wpool_b, pscale, batch=batch, seq=seq)
    o_prompt = _prompt_attention(qit, wt, qt, kib, kb, vtb, batch=batch, seq=seq)

    u_s, a_s, q_s, qi_s, k_s, v_s, kiwi_s = _in_projection_sample(
        x_s, wt_a, _rope_tables(np.full((dec_batch,), past)), wpool_b, pscale, jnp.transpose(state_pool, (1, 0, 2)),
        sample_pos=past)
    qi_h = qi_s.reshape(dec_batch, IDX_HEADS, IDX_DIM)
    qi_h = jnp.concatenate([qi_h, jnp.zeros_like(qi_h)], axis=-1)
    in_group = (jnp.arange(N_HEADS)[:, None] // group == jnp.arange(N_KV_HEADS)[None, :])[None, :, :, None]
    qs = jnp.where(in_group, q_s.reshape(dec_batch, N_HEADS, 1, HEAD_DIM), 0).reshape(dec_batch, N_HEADS, kv_w)
    w_s = kiwi_s[:, IDX_DIM:IDX_DIM + IDX_HEADS].reshape(dec_batch, IDX_HEADS, 1)
    o_s = _sample_attention(page_table, qi_h, w_s, qs, kiwi_s.reshape(dec_batch, 1, LANES),
                            k_s.reshape(dec_batch, 1, kv_w), v_s.reshape(dec_batch, 1, kv_w),
                            _keys_minor(cache_kidx), _keys_minor(cache_k), _keys_minor(cache_v))
    o_s = o_s.reshape(dec_batch, N_KV_HEADS, group, N_KV_HEADS, HEAD_DIM)
    o_s = jnp.transpose(jnp.diagonal(o_s, axis1=1, axis2=3), (0, 3, 1, 2)).reshape(dec_batch, attn_w)
    o_sample = o_s.astype(BF16)

    w_route = jnp.concatenate([w_expert_router, w_group,
                               jnp.zeros((d_model, LANES - N_EXPERTS - N_GROUPS), F32)], axis=1)
    g1, b1 = ln1_g.reshape(1, d_model), ln1_b.reshape(1, d_model)
    g2, b2 = ln2_g.reshape(1, d_model), ln2_b.reshape(1, d_model)

    merged_p = _merge(x_p, a_p, o_prompt, wgat.astype(BF16), wgbt.astype(BF16), w_branch_a.astype(BF16),
                      w_branch_b.astype(BF16), tm=ROW_TILE, col_block=d_model, precise=False)
    h_all, dense_all = _out_projection(merged_p, x_p, w_out.astype(BF16), g1, b1, w_route.astype(BF16),
                                       tm=ROW_TILE, precise=False, zero_tiles=1)
    merged_s = _merge(x_s, a_s, o_sample, wgat, wgbt, w_branch_a, w_branch_b, tm=dec_batch,
                      col_block=MERGE_COL_CHUNK, precise=True)
    h_s, dense_s = _out_projection(merged_s, x_s, w_out, g1, b1, w_route, tm=dec_batch, precise=True)

    h_all = lax.dynamic_update_slice(h_all, h_s, (n_prompt * (d_model // LANES), 0))
    dense_all = lax.dynamic_update_slice(dense_all, dense_s, (n_prompt, 0))
    slot_stride = dense_all.shape[0]
    plan = _route_plan(dense_all, EXPERT_ROW_TILE)
    y2 = _experts(h_all, plan, w_gate, w_up, w_down, tm=EXPERT_ROW_TILE, slot_stride=slot_stride)
    y_prompt = _combine(y2, h_all, g2, b2, tm=ROW_TILE, tile0=0, n_tiles=n_prompt // ROW_TILE,
                        slot_stride=slot_stride)
    y_sample = _combine(y2, h_all, g2, b2, tm=dec_batch, tile0=n_prompt // dec_batch, n_tiles=1,
                        slot_stride=slot_stride)

    heads_t = lambda t: jnp.transpose(t.reshape(batch, N_KV_HEADS, HEAD_DIM, seq), (0, 3, 1, 2))
    pool_sample = jnp.concatenate([state_pool[:, 1:], u_s[:, None, :]], axis=1)
    return (y_prompt.reshape(batch, seq, d_model), y_sample.reshape(dec_batch, 1, d_model),
            heads_t(kt), heads_t(vt), jnp.transpose(kit, (0, 2, 1)), utail[:, 16 - POOL_STATE:],
            k_s.reshape(dec_batch, 1, N_KV_HEADS, HEAD_DIM), v_s.reshape(dec_batch, 1, N_KV_HEADS, HEAD_DIM),
            kiwi_s[:, :IDX_DIM].reshape(dec_batch, 1, IDX_DIM), pool_sample)


def kernel(x_prompt, x_sample, cache_k, cache_v, cache_kidx, state_pool, page_table, w_in, w_pool, pool_scale,
           w_branch_a, w_branch_b, w_out, ln1_g, ln1_b, w_group, w_expert_router, w_gate, w_up, w_down,
           ln2_g, ln2_b):
    assert w_in.shape[0] == DEPTH
    outs = _layer(x_prompt, x_sample, cache_k[0], cache_v[0], cache_kidx[0], state_pool[0], page_table, w_in[0],
                  w_pool[0], pool_scale[0], w_branch_a[0], w_branch_b[0], w_out[0], ln1_g[0], ln1_b[0],
                  w_group[0], w_expert_router[0], w_gate[0], w_up[0], w_down[0], ln2_g[0], ln2_b[0])
    y_p, y_s = outs[0], outs[1]
    return (y_p, y_s) + tuple(o[None] for o in outs[2:])
```

```python
import functools
import math

import jax
import jax.numpy as jnp
import numpy as np
from jax import lax
from jax.experimental import pallas as pl
from jax.experimental.pallas import tpu as pltpu

BF16 = jnp.bfloat16
F32 = jnp.float32
I32 = jnp.int32

PAGE_SIZE = 128
POOL_WINDOWS = (2, 4, 8, 16)
POOL_STATE = 15
N_HEADS = 16
N_KV_HEADS = 4
HEAD_DIM = 64
ROT_DIM = HEAD_DIM // 4
ROPE_THETA = 500000.0
IDX_HEADS = 16
IDX_DIM = 64
IDX_W_SCALE = (IDX_HEADS * IDX_DIM) ** -0.5
TOPK_MAX = 256
N_GROUPS = 4
EXPERTS_PER_GROUP = 8
N_EXPERTS = N_GROUPS * EXPERTS_PER_GROUP
LN_EPS = 1e-5
DEPTH = 1
ALPHA = (2 * DEPTH) ** 0.25

LANES = 128
SUBLANES = 8
Q_TILE = 128
KEY_CHUNK = 512
ROW_TILE = 256
MOE_ROW_TILE = 512
VMEM_LIMIT = 56 * 1024 * 1024

INT_MIN = -2 ** 31
NEG_INF_KEY = int(np.int32(np.uint32(0xFF800000) ^ np.uint32(0x7FFFFFFF)))
NT_DIMS = (((1,), (1,)), ((), ()))


def _sortable_key(x):
    bits = lax.bitcast_convert_type(x, I32)
    return bits ^ ((bits >> 31) & 0x7FFFFFFF)


def _const_spec(shape):
    nd = len(shape)
    return pl.BlockSpec(shape, lambda *_: (0,) * nd, pipeline_mode=pl.Buffered(1))


def _project(xb, wt_ref, cos, nsin, psin, pool_width):
    tm = xb.shape[0]
    lo = lax.broadcasted_iota(I32, (tm, LANES), 1) < HEAD_DIM
    attn_w, kv_w, idx_w = N_HEADS * HEAD_DIM, N_KV_HEADS * HEAD_DIM, IDX_HEADS * IDX_DIM
    off_q = pool_width
    off_k = off_q + attn_w
    off_v = off_k + kv_w
    off_qi = off_v + kv_w
    off_kiwi = off_qi + idx_w

    def proj(c0, width):
        return lax.dot_general(xb, wt_ref[c0:c0 + width, :], NT_DIMS, preferred_element_type=F32)

    def rope(z, c, s1, s2):
        return z * c + pltpu.roll(z, LANES - ROT_DIM // 2, 1) * s1 + pltpu.roll(z, ROT_DIM // 2, 1) * s2

    def rope_tiles(z):
        return [rope(z[:, j * LANES:(j + 1) * LANES], cos, nsin, psin) for j in range(z.shape[1] // LANES)]

    u = proj(0, pool_width)
    q = [t * (HEAD_DIM ** -0.5) for t in rope_tiles(proj(off_q, attn_w))]
    qi = rope_tiles(proj(off_qi, idx_w))
    k = rope_tiles(proj(off_k, kv_w))
    v = proj(off_v, kv_w)
    kw = rope(proj(off_kiwi, LANES), jnp.where(lo, cos, IDX_W_SCALE), jnp.where(lo, nsin, 0.0),
              jnp.where(lo, psin, 0.0))
    return u, q, qi, k, v, kw


def _pool_mix(d, g, wpool_ref, pscale_ref, gw):
    z = jnp.dot(d.astype(BF16), wpool_ref[g], preferred_element_type=F32)
    return z * pscale_ref[:, g * gw:(g + 1) * gw]


def _inproj_body(x_ref, wt_ref, tc_ref, ts1_ref, ts2_ref, wpool_ref, pscale_ref,
                 utail_ref, a_ref, qt_ref, qit_ref, kb_ref, kib_ref, kt_ref, vt_ref, vtb_ref, kit_ref, wt_out_ref,
                 uext_ref, *, tm, tiles_per_batch, pool_width):
    i = pl.program_id(0)
    u, q, qi, k, v, kw = _project(x_ref[...].astype(BF16), wt_ref, tc_ref[...], ts1_ref[...], ts2_ref[...],
                                  pool_width)
    lo = lax.broadcasted_iota(I32, (tm, LANES), 1) < HEAD_DIM
    gw = pool_width // len(POOL_WINDOWS)

    def blocks(tile):
        return [(slice(blk * Q_TILE, (blk + 1) * Q_TILE), tile[blk * Q_TILE:(blk + 1) * Q_TILE].T)
                for blk in range(tm // Q_TILE)]

    def store_head_t(dst_ref, h, tile):
        for blk, (_, t) in enumerate(blocks(tile)):
            dst_ref[blk, :, h * Q_TILE:(h + 1) * Q_TILE] = t.astype(BF16)

    for j, p in enumerate(q):
        pr = pltpu.roll(p, HEAD_DIM, 1)
        if (j // 2) % 2 == 0:
            store_head_t(qt_ref, 2 * j, jnp.where(lo, p, 0.0))
            store_head_t(qt_ref, 2 * j + 1, jnp.where(lo, pr, 0.0))
        else:
            store_head_t(qt_ref, 2 * j, jnp.where(lo, 0.0, pr))
            store_head_t(qt_ref, 2 * j + 1, jnp.where(lo, 0.0, p))
    for j, p in enumerate(qi):
        store_head_t(qit_ref, 2 * j, jnp.where(lo, p, 0.0))
        store_head_t(qit_ref, 2 * j + 1, jnp.where(lo, pltpu.roll(p, IDX_DIM, 1), 0.0))

    for j, kr in enumerate(k):
        lanes = slice(j * LANES, (j + 1) * LANES)
        kb_ref[:, lanes] = kr.astype(BF16)
        for cols, t in blocks(kr):
            kt_ref[0, lanes, cols] = t
        for cols, t in blocks(v[:, lanes]):
            vt_ref[0, lanes, cols] = t
            vtb_ref[lanes, cols] = t.astype(BF16)
    kib_ref[...] = kw.astype(BF16)
    for cols, t in blocks(kw):
        kit_ref[0, :, cols] = t[:IDX_DIM, :]
        wt_out_ref[:, cols] = t[IDX_DIM:IDX_DIM + IDX_HEADS, :]

    first = (i % tiles_per_batch) == 0

    @pl.when(first)
    def _():
        uext_ref[0:16, :] = jnp.zeros((16, pool_width), F32)

    @pl.when(jnp.logical_not(first))
    def _():
        uext_ref[0:16, :] = uext_ref[tm:tm + 16, :]

    uext_ref[16:16 + tm, :] = u
    utail_ref[0] = u[tm - 16:tm, :]
    pos = (i % tiles_per_batch) * tm + lax.broadcasted_iota(I32, (tm, 1), 0)
    for g, w in enumerate(POOL_WINDOWS):
        lanes = slice(g * gw, (g + 1) * gw)
        acc = u[:, lanes]
        for jj in range(1, w):
            acc = acc + uext_ref[16 - jj:16 - jj + tm, lanes]
        cnt = jnp.minimum(w, pos + 1).astype(F32)
        d = acc / cnt - u[:, lanes]
        a_ref[:, lanes] = _pool_mix(d, g, wpool_ref, pscale_ref, gw).astype(BF16)


def _in_projection(x, wt_a, tabs, wpool, pscale, *, batch, seq):
    n_rows, d_model = x.shape
    tm = ROW_TILE
    pool_width = wpool.shape[0] * wpool.shape[1]
    nblk = n_rows // Q_TILE
    kv_w = N_KV_HEADS * HEAD_DIM
    tpb = seq // tm
    row = lambda width: pl.BlockSpec((tm, width), lambda i: (i, 0))
    col = lambda height: pl.BlockSpec((height, tm), lambda i: (0, i))
    per_seq = lambda height: pl.BlockSpec((1, height, tm), lambda i: (i // tpb, 0, i % tpb))
    tab = pl.BlockSpec((tm, LANES), lambda i: (i % tpb, 0))
    hm = pl.BlockSpec((tm // Q_TILE, LANES, N_HEADS * Q_TILE), lambda i: (i, 0, 0))
    body = functools.partial(_inproj_body, tm=tm, tiles_per_batch=tpb, pool_width=pool_width)
    return pl.pallas_call(
        body,
        grid=(n_rows // tm,),
        in_specs=[row(d_model), _const_spec(wt_a.shape), tab, tab, tab,
                  _const_spec(wpool.shape), _const_spec(pscale.shape)],
        out_specs=[pl.BlockSpec((1, 16, pool_width), lambda i: (i // tpb, 0, 0)),
                   row(pool_width), hm, hm, row(kv_w), row(LANES), per_seq(kv_w), per_seq(kv_w), col(kv_w),
                   per_seq(IDX_DIM), col(IDX_HEADS)],
        out_shape=[
            jax.ShapeDtypeStruct((batch, 16, pool_width), F32),
            jax.ShapeDtypeStruct((n_rows, pool_width), BF16),
            jax.ShapeDtypeStruct((nblk, LANES, N_HEADS * Q_TILE), BF16),
            jax.ShapeDtypeStruct((nblk, LANES, IDX_HEADS * Q_TILE), BF16),
            jax.ShapeDtypeStruct((n_rows, kv_w), BF16),
            jax.ShapeDtypeStruct((n_rows, LANES), BF16),
            jax.ShapeDtypeStruct((batch, kv_w, seq), F32),
            jax.ShapeDtypeStruct((batch, kv_w, seq), F32),
            jax.ShapeDtypeStruct((kv_w, n_rows), BF16),
            jax.ShapeDtypeStruct((batch, IDX_DIM, seq), F32),
            jax.ShapeDtypeStruct((IDX_HEADS, n_rows), F32),
        ],
        scratch_shapes=[pltpu.VMEM((tm + 16, pool_width), F32)],
        compiler_params=pltpu.CompilerParams(dimension_semantics=("arbitrary",), vmem_limit_bytes=VMEM_LIMIT),
        name="in_projection",
    )(x, wt_a, *tabs, wpool, pscale)


def _inproj_sample_body(x_ref, wt_ref, tc_ref, ts1_ref, ts2_ref, wpool_ref, pscale_ref, state_ref,
                        u_ref, a_ref, q_ref, qi_ref, k_ref, v_ref, kiwi_ref, *, pool_width, sample_pos):
    u, q, qi, k, v, kw = _project(x_ref[...].astype(BF16), wt_ref, tc_ref[...], ts1_ref[...], ts2_ref[...],
                                  pool_width)
    gw = pool_width // len(POOL_WINDOWS)
    u_ref[...] = u
    for j, t in enumerate(q):
        q_ref[:, j * LANES:(j + 1) * LANES] = t.astype(BF16)
    for j, t in enumerate(qi):
        qi_ref[:, j * LANES:(j + 1) * LANES] = t.astype(BF16)
    for j, t in enumerate(k):
        k_ref[:, j * LANES:(j + 1) * LANES] = t
    v_ref[...] = v
    kiwi_ref[...] = kw
    for g, w in enumerate(POOL_WINDOWS):
        lanes = slice(g * gw, (g + 1) * gw)
        acc = u[:, lanes]
        for jj in range(1, w):
            acc = acc + state_ref[POOL_STATE - jj, :, lanes]
        d = acc / float(min(w, sample_pos + 1)) - u[:, lanes]
        a_ref[:, lanes] = _pool_mix(d, g, wpool_ref, pscale_ref, gw).astype(BF16)


def _in_projection_sample(x, wt_a, tabs, wpool, pscale, state_t, *, sample_pos):
    n_rows, d_model = x.shape
    pool_width = wpool.shape[0] * wpool.shape[1]
    kv_w = N_KV_HEADS * HEAD_DIM
    full = lambda shape: pl.BlockSpec(shape, lambda i: (0,) * len(shape))
    body = functools.partial(_inproj_sample_body, pool_width=pool_width, sample_pos=sample_pos)
    out_shape = [
        jax.ShapeDtypeStruct((n_rows, pool_width), F32),
        jax.ShapeDtypeStruct((n_rows, pool_width), BF16),
        jax.ShapeDtypeStruct((n_rows, N_HEADS * HEAD_DIM), BF16),
        jax.ShapeDtypeStruct((n_rows, IDX_HEADS * IDX_DIM), BF16),
        jax.ShapeDtypeStruct((n_rows, kv_w), F32),
        jax.ShapeDtypeStruct((n_rows, kv_w), F32),
        jax.ShapeDtypeStruct((n_rows, LANES), F32),
    ]
    return pl.pallas_call(
        body,
        grid=(1,),
        in_specs=[full(x.shape), _const_spec(wt_a.shape), full(tabs[0].shape), full(tabs[1].shape),
                  full(tabs[2].shape), _const_spec(wpool.shape), _const_spec(pscale.shape), full(state_t.shape)],
        out_specs=[full(s.shape) for s in out_shape],
        out_shape=out_shape,
        compiler_params=pltpu.CompilerParams(dimension_semantics=("arbitrary",), vmem_limit_bytes=VMEM_LIMIT),
        name="in_projection_sample",
    )(x, wt_a, *tabs, wpool, pscale, state_t)


def _rope_tables(positions):
    pos = np.asarray(positions, np.float32)
    n = pos.shape[0]
    half = ROT_DIM // 2
    inv = np.exp(-np.arange(half, dtype=np.float32) * np.float32(math.log(ROPE_THETA) / half)).astype(np.float32)
    ang = (pos[:, None] * inv[None, :]).astype(np.float32).astype(np.float64)
    cos, sin = np.cos(ang).astype(np.float32), np.sin(ang).astype(np.float32)
    zeros = np.zeros((n, half), np.float32)
    rest = HEAD_DIM - ROT_DIM
    c = np.concatenate([cos, cos, np.ones((n, rest), np.float32)], axis=1)
    s1 = np.concatenate([-sin, zeros, np.zeros((n, rest), np.float32)], axis=1)
    s2 = np.concatenate([zeros, sin, np.zeros((n, rest), np.float32)], axis=1)
    return tuple(jnp.asarray(np.concatenate([t, t], axis=1)) for t in (c, s1, s2))


def _topk_threshold(count_ge, init, topk):
    def step(it, t):
        cand = t + jnp.left_shift(jnp.int32(1), 31 - it)
        return jnp.where(count_ge(cand) >= topk, cand, t)
    return lax.fori_loop(0, 32, step, init)


def _fold_slabs(x, op):
    assert x.shape[0] & (x.shape[0] - 1) == 0
    while x.shape[0] > 1:
        half = x.shape[0] // 2
        x = op(x[:half], x[half:])
    return x[0]


def _prompt_attn_body(qit_ref, wt_ref, qt_ref, kib_ref, kb_ref, vt_ref, o_ref,
                      sc_ref, thr_ref, tie_ref, m_ref, l_ref, acc_ref, *, topk):
    i = pl.program_id(1)
    tq, kc = Q_TILE, KEY_CHUNK
    seq = kib_ref.shape[0]
    index_bits = seq.bit_length()
    n_keys = (i + 1) * tq
    n_chunks = (n_keys + kc - 1) // kc
    qpos = i * tq + lax.broadcasted_iota(I32, (1, tq), 1)
    group = N_HEADS // N_KV_HEADS
    gcols = group * tq

    def score_chunk(c, carry):
        k0 = pl.multiple_of(c * kc, kc)
        kchunk = kib_ref[pl.ds(k0, kc), :]
        acc = jnp.zeros((kc, tq), F32)
        for hg in range(IDX_HEADS // group):
            d = jnp.dot(kchunk, qit_ref[0, :, hg * gcols:(hg + 1) * gcols], preferred_element_type=F32)
            for r in range(group):
                h = hg * group + r
                acc = acc + jnp.maximum(d[:, r * tq:(r + 1) * tq], 0.0) * wt_ref[h:h + 1, :]
        kpos = k0 + lax.broadcasted_iota(I32, (kc, 1), 0)
        sc_ref[pl.ds(k0, kc), :] = _sortable_key(jnp.where(kpos <= qpos, acc, -jnp.inf))
        return carry

    lax.fori_loop(0, n_chunks, score_chunk, 0)

    thr_ref[...] = jnp.full((1, tq), NEG_INF_KEY + 1, I32)
    tie_ref[...] = jnp.full((1, tq), seq, I32)

    @pl.when(n_keys > topk)
    def _():
        def count(pred):
            def blk(c, acc):
                k0 = pl.multiple_of(c * kc, kc)
                keys = sc_ref[pl.ds(k0, kc), :].reshape(kc // SUBLANES, SUBLANES, tq)
                kpos = k0 + lax.broadcasted_iota(I32, (kc, 1), 0).reshape(kc // SUBLANES, SUBLANES, 1)
                return acc + _fold_slabs(jnp.where(pred(keys, kpos), 1.0, 0.0), jnp.add)
            acc = lax.fori_loop(0, n_chunks, blk, jnp.zeros((SUBLANES, tq), F32))
            return jnp.sum(acc, axis=0, keepdims=True)

        def count_ge(cand):
            candb = jnp.broadcast_to(cand, (SUBLANES, tq))
            return count(lambda keys, kpos: keys >= candb)

        t = _topk_threshold(count_ge, jnp.full((1, tq), INT_MIN, I32), float(topk))
        t = jnp.where(qpos + 1 > topk, t, NEG_INF_KEY + 1)
        thr_ref[...] = t
        n_above = count_ge(t + 1)
        need = float(topk) - n_above
        tied = (count_ge(t) - n_above > need) & (qpos + 1 > topk)

        @pl.when(jnp.max(jnp.where(tied, 1.0, 0.0)) > 0.0)
        def _():
            tb = jnp.broadcast_to(t, (SUBLANES, tq))

            def step(it, d):
                cand = d + jnp.left_shift(jnp.int32(1), index_bits - 1 - it)
                candb = jnp.broadcast_to(cand, (SUBLANES, tq))
                before = count(lambda keys, kpos: (keys == tb) & (kpos < candb))
                return jnp.where(before < need, cand, d)
            last = lax.fori_loop(0, index_bits, step, jnp.zeros((1, tq), I32))
            tie_ref[...] = jnp.where(tied, last, seq)

    m_ref[...] = jnp.full(m_ref.shape, -jnp.inf, F32)
    l_ref[...] = jnp.zeros(l_ref.shape, F32)
    acc_ref[...] = jnp.zeros(acc_ref.shape, F32)

    def attn_chunk(c, carry):
        k0 = pl.multiple_of(c * kc, kc)
        keys = sc_ref[pl.ds(k0, kc), :]
        kpos = k0 + lax.broadcasted_iota(I32, (kc, 1), 0)
        sel = (keys > thr_ref[...]) | ((keys == thr_ref[...]) & (kpos <= tie_ref[...]))
        for g in range(N_KV_HEADS):
            pair = slice((g // 2) * LANES, (g // 2 + 1) * LANES)
            cols = slice(g * gcols, (g + 1) * gcols)
            s = jnp.dot(kb_ref[pl.ds(k0, kc), pair], qt_ref[0, :, cols], preferred_element_type=F32)
            s = jnp.concatenate([jnp.where(sel, s[:, r * tq:(r + 1) * tq], -jnp.inf) for r in range(group)], axis=1)
            m_old = m_ref[:, cols]
            m_new = jnp.maximum(m_old, jnp.max(s, axis=0, keepdims=True))
            m_safe = jnp.where(m_new == -jnp.inf, 0.0, m_new)
            alpha = jnp.exp(m_old - m_safe)
            p = jnp.exp(s - m_safe)
            l_ref[:, cols] = alpha * l_ref[:, cols] + jnp.sum(p, axis=0, keepdims=True)
            pv = jnp.dot(vt_ref[pair, pl.ds(k0, kc)], p.astype(BF16), preferred_element_type=F32)
            acc_ref[g] = alpha * acc_ref[g] + pv
            m_ref[:, cols] = m_new
        return carry

    lax.fori_loop(0, n_chunks, attn_chunk, 0)
    l_all = l_ref[...]

    for j in range(N_HEADS // 2):
        halves = []
        for h in (2 * j, 2 * j + 1):
            g, r = h // group, h % group
            rows = slice((g % 2) * HEAD_DIM, (g % 2 + 1) * HEAD_DIM)
            cols = slice(r * tq, (r + 1) * tq)
            halves.append(acc_ref[g, rows, cols] / l_all[:, g * gcols + r * tq:g * gcols + (r + 1) * tq])
        o_ref[:, j * LANES:(j + 1) * LANES] = jnp.concatenate(halves, axis=0).T.astype(BF16)


def _prompt_attention(qit, wt, qt, kib, kb, vt, *, batch, seq):
    nq = seq // Q_TILE
    topk = min(TOPK_MAX, seq // 4)
    kv_w = N_KV_HEADS * HEAD_DIM
    per_q = lambda b, i: (b * nq + i, 0, 0)
    body = functools.partial(_prompt_attn_body, topk=topk)
    return pl.pallas_call(
        body,
        grid=(batch, nq),
        in_specs=[
            pl.BlockSpec((1, LANES, IDX_HEADS * Q_TILE), per_q),
            pl.BlockSpec((IDX_HEADS, Q_TILE), lambda b, i: (0, b * nq + i)),
            pl.BlockSpec((1, LANES, N_HEADS * Q_TILE), per_q),
            pl.BlockSpec((seq, LANES), lambda b, i: (b, 0)),
            pl.BlockSpec((seq, kv_w), lambda b, i: (b, 0)),
            pl.BlockSpec((kv_w, seq), lambda b, i: (0, b)),
        ],
        out_specs=pl.BlockSpec((Q_TILE, N_HEADS * HEAD_DIM), lambda b, i: (b * nq + i, 0)),
        out_shape=jax.ShapeDtypeStruct((batch * seq, N_HEADS * HEAD_DIM), BF16),
        scratch_shapes=[
            pltpu.VMEM((seq, Q_TILE), I32),
            pltpu.VMEM((1, Q_TILE), I32),
            pltpu.VMEM((1, Q_TILE), I32),
            pltpu.VMEM((1, N_HEADS * Q_TILE), F32),
            pltpu.VMEM((1, N_HEADS * Q_TILE), F32),
            pltpu.VMEM((N_KV_HEADS, LANES, (N_HEADS // N_KV_HEADS) * Q_TILE), F32),
        ],
        compiler_params=pltpu.CompilerParams(dimension_semantics=("arbitrary", "arbitrary"),
                                             vmem_limit_bytes=VMEM_LIMIT),
        name="prompt_attention",
    )(qit, wt, qt, kib, kb, vt)


SAMPLE_PAGES_PER_CHUNK = 16


def _sample_attn_body(pt_ref, qi_ref, w_ref, qs_ref, kicur_ref, kcur_ref, vcur_ref, ckidx_hbm, ck_hbm, cv_hbm,
                      o_ref, kidx_buf, k_buf, v_buf, sc_ref, sem_idx, sem_k, sem_v, *, n_pages, topk):
    b = pl.program_id(0)
    final_step = pl.num_programs(0) - 1
    ppc = SAMPLE_PAGES_PER_CHUNK
    n_chunks = n_pages // ppc
    ck = ppc * PAGE_SIZE
    seg = sc_ref.shape[1]
    segs_per_chunk = ck // seg

    def idx_copy(seq, p):
        return pltpu.make_async_copy(ckidx_hbm.at[pt_ref[seq, p]],
                                     kidx_buf.at[seq % 2, :, pl.ds(p * PAGE_SIZE, PAGE_SIZE)], sem_idx.at[seq % 2])

    def kv_copies(seq, c, p):
        slot = c % 2
        page = pt_ref[seq, c * ppc + p]
        window = pl.ds(p * PAGE_SIZE, PAGE_SIZE)
        return (pltpu.make_async_copy(ck_hbm.at[page], k_buf.at[slot, :, window], sem_k.at[slot]),
                pltpu.make_async_copy(cv_hbm.at[page], v_buf.at[slot, :, window], sem_v.at[slot]))

    def start_idx(seq):
        for p in range(n_pages):
            idx_copy(seq, p).start()

    def start_kv(seq, c):
        for p in range(ppc):
            for cp in kv_copies(seq, c, p):
                cp.start()

    def wait_kv(seq, c):
        for p in range(ppc):
            for cp in kv_copies(seq, c, p):
                cp.wait()

    @pl.when(b == 0)
    def _():
        start_idx(b)
        start_kv(b, 0)

    for p in range(n_pages):
        idx_copy(b, p).wait()

    qi = qi_ref[0][:, :IDX_DIM]
    w = w_ref[0]

    def head_mix(d):
        return jnp.sum(jnp.maximum(d, 0.0) * w, axis=0, keepdims=True)

    def bf16_products(a, row):
        return jnp.sum(a.astype(F32) * row.astype(BF16).astype(F32), axis=1, keepdims=True)

    for c in range(n_chunks):
        d = jnp.dot(qi, kidx_buf[b % 2, :, c * ck:(c + 1) * ck].astype(BF16), preferred_element_type=F32)
        keys = _sortable_key(head_mix(d))
        for r in range(segs_per_chunk):
            sc_ref[c * segs_per_chunk + r:c * segs_per_chunk + r + 1, :] = keys[:, r * seg:(r + 1) * seg]
    cur_key = _sortable_key(head_mix(bf16_products(qi, kicur_ref[0][:, :IDX_DIM])))

    vreg = (SUBLANES, LANES)
    past = n_pages * PAGE_SIZE
    if past + 1 > topk:
        cur_key_v = jnp.broadcast_to(cur_key, vreg)

        def total(hits):
            return jnp.broadcast_to(jnp.sum(jnp.sum(hits, axis=1, keepdims=True), axis=0, keepdims=True), vreg)

        def count_ge(cand):
            hits = jnp.where(cur_key_v >= cand, 1.0 / (SUBLANES * LANES), 0.0)
            for j in range(seg // LANES):
                hits = hits + jnp.where(sc_ref[:, j * LANES:(j + 1) * LANES] >= cand, 1.0, 0.0)
            return total(hits)

        thr_v = _topk_threshold(count_ge, jnp.full(vreg, INT_MIN, I32), float(topk))
        need = float(topk) - count_ge(thr_v + 1)
        pos0 = lax.broadcasted_iota(I32, vreg, 0) * seg + lax.broadcasted_iota(I32, vreg, 1)
        index_bits = (past + 1).bit_length()

        def step(it, d):
            cand = d + jnp.left_shift(jnp.int32(1), index_bits - 1 - it)
            hits = jnp.zeros(vreg, F32)
            for j in range(seg // LANES):
                tied = (sc_ref[:, j * LANES:(j + 1) * LANES] == thr_v) & (pos0 + j * LANES < cand)
                hits = hits + jnp.where(tied, 1.0, 0.0)
            return jnp.where(total(hits) < need, cand, d)
        thr = thr_v[0:1, 0:1]
        tied = jnp.max(jnp.where(count_ge(thr_v) - count_ge(thr_v + 1) > need, 1.0, 0.0)) > 0.0
        last = lax.cond(tied, lambda: lax.fori_loop(0, index_bits, step, jnp.zeros(vreg, I32)),
                        lambda: jnp.full(vreg, past, I32))[0:1, 0:1]
    else:
        thr = jnp.full((1, 1), NEG_INF_KEY + 1, I32)
        last = jnp.full((1, 1), past, I32)

    def selected(keys, kpos):
        return (keys > thr) | ((keys == thr) & (kpos <= last))

    qs = qs_ref[0]
    m = jnp.full((N_HEADS, 1), -jnp.inf, F32)
    l = jnp.zeros((N_HEADS, 1), F32)
    acc = jnp.zeros((N_HEADS, N_KV_HEADS * HEAD_DIM), F32)

    def update(m, l, acc, s, sel, pv):
        s = jnp.where(sel, s, -jnp.inf)
        m_new = jnp.maximum(m, jnp.max(s, axis=1, keepdims=True))
        m_safe = jnp.where(m_new == -jnp.inf, 0.0, m_new)
        alpha = jnp.exp(m - m_safe)
        p = jnp.exp(s - m_safe)
        l = alpha * l + jnp.sum(p, axis=1, keepdims=True)
        acc = alpha * acc + pv(p.astype(BF16))
        return m_new, l, acc

    def prefetch_next_sequence():
        @pl.when(b < final_step)
        def _():
            start_idx(b + 1)
            start_kv(b + 1, 0)

    slot0_free_early = n_chunks % 2 == 0
    for c in range(n_chunks):
        if c + 1 < n_chunks:
            start_kv(b, c + 1)
        elif slot0_free_early:
            prefetch_next_sequence()
        wait_kv(b, c)
        slot = c % 2
        s = jnp.dot(qs, k_buf[slot].astype(BF16), preferred_element_type=F32)
        vt = v_buf[slot].astype(BF16)
        keys = jnp.concatenate([sc_ref[c * segs_per_chunk + r:c * segs_per_chunk + r + 1, :]
                                for r in range(segs_per_chunk)], axis=1)
        sel = selected(keys, c * ck + lax.broadcasted_iota(I32, (1, ck), 1))
        m, l, acc = update(m, l, acc, s, sel, lambda p: lax.dot_general(p, vt, NT_DIMS, preferred_element_type=F32))
    if not slot0_free_early:
        prefetch_next_sequence()
    v_cur = vcur_ref[0].astype(BF16).astype(F32)
    m, l, acc = update(m, l, acc, bf16_products(qs, kcur_ref[0]), selected(cur_key, past),
                       lambda p: p.astype(F32) * v_cur)
    o_ref[0] = acc / l


def _keys_minor(cache):
    pages, page_size = cache.shape[:2]
    return jnp.moveaxis(cache, 1, -1).reshape(pages, -1, page_size)


def _sample_attention(page_table, qi_s, w_s, qs, kicur, kcur, vcur, cache_kidx, cache_k, cache_v):
    dec_batch, n_pages = page_table.shape
    past = n_pages * PAGE_SIZE
    topk = min(TOPK_MAX, (past + 1) // 4)
    kv_w = N_KV_HEADS * HEAD_DIM
    ppc = SAMPLE_PAGES_PER_CHUNK
    per_b = lambda b, pt: (b, 0, 0)
    any_spec = pl.BlockSpec(memory_space=pl.ANY)
    grid_spec = pltpu.PrefetchScalarGridSpec(
        num_scalar_prefetch=1,
        grid=(dec_batch,),
        in_specs=[
            pl.BlockSpec((1, IDX_HEADS, LANES), per_b),
            pl.BlockSpec((1, IDX_HEADS, 1), per_b),
            pl.BlockSpec((1, N_HEADS, kv_w), per_b),
            pl.BlockSpec((1, 1, LANES), per_b),
            pl.BlockSpec((1, 1, kv_w), per_b),
            pl.BlockSpec((1, 1, kv_w), per_b),
            any_spec, any_spec, any_spec,
        ],
        out_specs=pl.BlockSpec((1, N_HEADS, kv_w), per_b),
        scratch_shapes=[
            pltpu.VMEM((2, IDX_DIM, past), F32),
            pltpu.VMEM((2, kv_w, ppc * PAGE_SIZE), F32),
            pltpu.VMEM((2, kv_w, ppc * PAGE_SIZE), F32),
            pltpu.VMEM((SUBLANES, past // SUBLANES), I32),
            pltpu.SemaphoreType.DMA((2,)),
            pltpu.SemaphoreType.DMA((2,)),
            pltpu.SemaphoreType.DMA((2,)),
        ],
    )
    body = functools.partial(_sample_attn_body, n_pages=n_pages, topk=topk)
    return pl.pallas_call(
        body,
        grid_spec=grid_spec,
        out_shape=jax.ShapeDtypeStruct((dec_batch, N_HEADS, kv_w), F32),
        compiler_params=pltpu.CompilerParams(dimension_semantics=("arbitrary",), vmem_limit_bytes=VMEM_LIMIT),
        name="sample_attention",
    )(page_table, qi_s, w_s, qs, kicur, kcur, vcur, cache_kidx, cache_k, cache_v)


def _mm(a, b, precise, *, nt=False):
    dims = NT_DIMS if nt else (((1,), (0,)), ((), ()))
    if precise:
        return lax.dot_general(a.astype(F32), b, dims, precision=lax.Precision.HIGHEST, preferred_element_type=F32)
    return lax.dot_general(a.astype(BF16), b, dims, preferred_element_type=F32)


def _operand_dtype(precise):
    return F32 if precise else BF16


MERGE_COL_CHUNK = 512


def _merge_body(x_ref, a_ref, o_ref, wgat_ref, wgbt_ref, wba_ref, wbb_ref, out_ref, *, precise):
    x, a, o = x_ref[...], a_ref[...], o_ref[...]
    cc = MERGE_COL_CHUNK
    for c in range(out_ref.shape[1] // cc):
        cols = slice(c * cc, (c + 1) * cc)
        ga = _mm(x, wgat_ref[cols, :], precise, nt=True)
        gb = _mm(x, wgbt_ref[cols, :], precise, nt=True)
        ya = _mm(a, wba_ref[:, cols], precise)
        yo = _mm(o, wbb_ref[:, cols], precise)
        out_ref[:, cols] = (jax.nn.sigmoid(ga) * ya + jax.nn.sigmoid(gb) * yo).astype(out_ref.dtype)


def _merge(x, a, o, wgat, wgbt, wba, wbb, *, tm, col_block, precise):
    n_rows, d_model = x.shape
    mode = dict(pipeline_mode=pl.Buffered(1)) if col_block == d_model else {}
    gate_spec = pl.BlockSpec((col_block, d_model), lambda i, j: (j, 0), **mode)
    branch_spec = lambda w: pl.BlockSpec((w.shape[0], col_block), lambda i, j: (0, j), **mode)
    row = lambda width: pl.BlockSpec((tm, width), lambda i, j: (i, 0))
    body = functools.partial(_merge_body, precise=precise)
    return pl.pallas_call(
        body,
        grid=(n_rows // tm, d_model // col_block),
        in_specs=[row(d_model), row(a.shape[1]), row(o.shape[1]), gate_spec, gate_spec,
                  branch_spec(wba), branch_spec(wbb)],
        out_specs=pl.BlockSpec((tm, col_block), lambda i, j: (i, j)),
        out_shape=jax.ShapeDtypeStruct((n_rows, d_model), _operand_dtype(precise)),
        compiler_params=pltpu.CompilerParams(dimension_semantics=("arbitrary", "arbitrary"),
                                             vmem_limit_bytes=VMEM_LIMIT),
        name="branch_merge",
    )(x, a, o, wgat, wgbt, wba, wbb)


def _layer_norm(x, g, b):
    mu = jnp.mean(x, axis=-1, keepdims=True)
    var = jnp.mean(jnp.square(x - mu), axis=-1, keepdims=True)
    return (x - mu) * lax.rsqrt(var + LN_EPS) * g + b


def _route(logits):
    lane = lax.broadcasted_iota(I32, logits.shape, 1)
    lane_f = lane.astype(F32)
    big = float(LANES)

    def masked_softmax(mask):
        x = jnp.where(mask, logits, -jnp.inf)
        e = jnp.exp(x - jnp.max(x, axis=1, keepdims=True))
        return jnp.where(mask, e / jnp.sum(e, axis=1, keepdims=True), -1.0)

    def top1(p):
        best = jnp.max(p, axis=1, keepdims=True)
        idx = jnp.min(jnp.where(p == best, lane_f, big), axis=1, keepdims=True)
        return best, idx

    gp = masked_softmax((lane >= N_EXPERTS) & (lane < N_EXPERTS + N_GROUPS))
    g_p, g_lane = top1(gp)
    g_idx = g_lane.astype(I32) - N_EXPERTS
    ep = masked_softmax((lane < N_EXPERTS) & ((lane >> int(math.log2(EXPERTS_PER_GROUP))) == g_idx))
    p1, i1 = top1(ep)
    p2, i2 = top1(jnp.where(lane_f == i1, -1.0, ep))
    denom = p1 + p2
    return jnp.where(lane_f == i1, g_p * p1 / denom, 0.0) + jnp.where(lane_f == i2, g_p * p2 / denom, 0.0)


def _store_token_major(ref, x):
    chunks = x.shape[1] // LANES
    for c in range(chunks):
        ref[pl.ds(c, x.shape[0], stride=chunks), :] = x[:, c * LANES:(c + 1) * LANES]


def _load_token_major(ref, n_tokens):
    chunks = ref.shape[0] // n_tokens
    return jnp.concatenate([ref[pl.ds(c, n_tokens, stride=chunks), :] for c in range(chunks)], axis=1)


def _outproj_body(m_ref, x_ref, wout_ref, g_ref, b_ref, wr_ref, h_ref, dense_ref, *, precise, n_real):
    i = pl.program_id(0)

    @pl.when(i < n_real)
    def _():
        mo = _mm(m_ref[...], wout_ref[...], precise)
        h = _layer_norm(ALPHA * x_ref[...] + mo, g_ref[...], b_ref[...])
        _store_token_major(h_ref, h)
        dense_ref[...] = _route(_mm(h, wr_ref[...], precise))

    @pl.when(i >= n_real)
    def _():
        h_ref[...] = jnp.zeros(h_ref.shape, F32)
        dense_ref[...] = jnp.zeros(dense_ref.shape, F32)


def _out_projection(merged, x, wout, ln_g, ln_b, w_route, *, tm, precise, zero_tiles=0):
    n_rows, d_model = x.shape
    n_real = n_rows // tm
    chunks = d_model // LANES
    body = functools.partial(_outproj_body, precise=precise, n_real=n_real)
    in_row = lambda width: pl.BlockSpec((tm, width), lambda i: (jnp.minimum(i, n_real - 1), 0))
    n_out = n_rows + zero_tiles * tm
    return pl.pallas_call(
        body,
        grid=(n_real + zero_tiles,),
        in_specs=[in_row(d_model), in_row(d_model), _const_spec(wout.shape),
                  _const_spec(ln_g.shape), _const_spec(ln_b.shape), _const_spec(w_route.shape)],
        out_specs=[pl.BlockSpec((tm * chunks, LANES), lambda i: (i, 0)), pl.BlockSpec((tm, LANES), lambda i: (i, 0))],
        out_shape=[jax.ShapeDtypeStruct((n_out * chunks, LANES), F32),
                   jax.ShapeDtypeStruct((n_out, LANES), F32)],
        compiler_params=pltpu.CompilerParams(dimension_semantics=("arbitrary",), vmem_limit_bytes=VMEM_LIMIT),
        name="out_projection",
    )(merged, x, wout, ln_g, ln_b, w_route)


EXPERT_ROW_TILE = 256
TOP_K_EXPERTS = 2


def _route_plan(dense, tm):
    n_tok = dense.shape[0]
    n_asg = TOP_K_EXPERTS * n_tok
    n_tiles = (n_asg + N_EXPERTS * (tm - 1)) // tm + 1
    w2, e2 = lax.top_k(dense[:, :N_EXPERTS], TOP_K_EXPERTS)
    eflat = e2.reshape(n_asg).astype(I32)
    w_bits = lax.bitcast_convert_type(w2.reshape(n_asg), I32)
    e_sorted, a_sorted, w_sorted = lax.sort((eflat, jnp.arange(n_asg, dtype=I32), w_bits), num_keys=1, is_stable=True)
    starts = jnp.searchsorted(e_sorted, jnp.arange(N_EXPERTS + 1, dtype=I32), side="left").astype(I32)
    counts = starts[1:] - starts[:-1]
    tiles_e = (counts + tm - 1) // tm
    tile_end = jnp.cumsum(tiles_e).astype(I32)
    tile_start = tile_end - tiles_e
    tile = jnp.arange(n_tiles, dtype=I32)
    tile_expert = jnp.minimum(jnp.searchsorted(tile_end, tile, side="right"), N_EXPERTS - 1).astype(I32)
    tile_used = (tile < tile_end[-1]).astype(I32)
    q = ((tile - tile_start[tile_expert]) * tm)[:, None] + jnp.arange(tm, dtype=I32)[None, :]
    valid = ((q < counts[tile_expert][:, None]) & (tile_used[:, None] > 0)).reshape(n_tiles * tm)
    pos = jnp.clip(starts[tile_expert][:, None] + q, 0, n_asg - 1).reshape(n_tiles * tm)
    picked = jnp.stack([a_sorted, w_sorted], axis=1)[pos]
    a = picked[:, 0]
    src = jnp.where(valid, a // TOP_K_EXPERTS, 0)
    slot = a % TOP_K_EXPERTS
    weight = jnp.where(valid, lax.bitcast_convert_type(picked[:, 1], F32), 0.0)
    return src, slot, valid, weight, tile_expert, tile_used


def _expert_body(te_ref, used_ref, src_ref, dst_ref, cw_ref, wg_ref, wu_ref, wd_ref, h_hbm, y_hbm,
                 x_buf, y_buf, sem_in, sem_out, *, tm):
    j = pl.program_id(0)
    chunks = x_buf.shape[0] // tm

    def token_rows(t):
        return pl.ds(pl.multiple_of(t * chunks, chunks), chunks)

    def gather(r):
        return pltpu.make_async_copy(h_hbm.at[token_rows(src_ref[0, 0, r]), :], x_buf.at[token_rows(r), :],
                                     sem_in.at[0])

    def scatter(r):
        return pltpu.make_async_copy(y_buf.at[token_rows(r), :], y_hbm.at[token_rows(dst_ref[0, 0, r]), :],
                                     sem_out.at[0])

    @pl.when(j == 0)
    def _():
        y_buf[...] = jnp.zeros(y_buf.shape, F32)
        pad_rows = pltpu.make_async_copy(y_buf, y_hbm.at[pl.ds(y_hbm.shape[0] - tm * chunks, tm * chunks), :],
                                         sem_out.at[0])
        pad_rows.start()
        pad_rows.wait()

    @pl.when(used_ref[j] > 0)
    def _():
        for r in range(tm):
            gather(r).start()
        for r in range(tm):
            gather(r).wait()
        xb = _load_token_major(x_buf, tm).astype(BF16)
        gate = jnp.dot(xb, wg_ref[0].astype(BF16), preferred_element_type=F32)
        up = jnp.dot(xb, wu_ref[0].astype(BF16), preferred_element_type=F32)
        hh = (gate * jax.nn.sigmoid(gate)) * up * cw_ref[0]
        _store_token_major(y_buf, jnp.dot(hh.astype(BF16), wd_ref[0].astype(BF16), preferred_element_type=F32))
        for r in range(tm):
            scatter(r).start()
        for r in range(tm):
            scatter(r).wait()


def _experts(h_all, plan, wg, wu, wd, *, tm, slot_stride):
    src, slot, valid, weight, tile_expert, tile_used = plan
    n_experts, d_model, d_expert = wg.shape
    chunks = d_model // LANES
    n_tiles = tile_expert.shape[0]
    pad_row = 2 * slot_stride + (jnp.arange(n_tiles * tm, dtype=I32) % tm)
    dst = jnp.where(valid, slot * slot_stride + src, pad_row)
    per_tile = lambda a: a.reshape(n_tiles, 1, tm)
    smem_rows = pl.BlockSpec((1, 1, tm), lambda j, te, used: (j, 0, 0), memory_space=pltpu.SMEM)
    any_spec = pl.BlockSpec(memory_space=pl.ANY)
    grid_spec = pltpu.PrefetchScalarGridSpec(
        num_scalar_prefetch=2,
        grid=(n_tiles,),
        in_specs=[smem_rows, smem_rows,
                  pl.BlockSpec((1, tm, 1), lambda j, te, used: (j, 0, 0)),
                  pl.BlockSpec((1, d_model, d_expert), lambda j, te, used: (te[j], 0, 0)),
                  pl.BlockSpec((1, d_model, d_expert), lambda j, te, used: (te[j], 0, 0)),
                  pl.BlockSpec((1, d_expert, d_model), lambda j, te, used: (te[j], 0, 0)),
                  any_spec],
        out_specs=any_spec,
        scratch_shapes=[pltpu.VMEM((tm * chunks, LANES), F32), pltpu.VMEM((tm * chunks, LANES), F32),
                        pltpu.SemaphoreType.DMA((1,)), pltpu.SemaphoreType.DMA((1,))],
    )
    return pl.pallas_call(
        functools.partial(_expert_body, tm=tm),
        grid_spec=grid_spec,
        out_shape=jax.ShapeDtypeStruct(((2 * slot_stride + tm) * chunks, LANES), F32),
        compiler_params=pltpu.CompilerParams(dimension_semantics=("arbitrary",), vmem_limit_bytes=VMEM_LIMIT),
        name="expert_mlp",
    )(tile_expert, tile_used, per_tile(src), per_tile(dst), weight.reshape(n_tiles, tm, 1), wg, wu, wd, h_all)


def _combine_body(y0_ref, y1_ref, h_ref, g_ref, b_ref, out_ref):
    tm = out_ref.shape[0]
    f = _load_token_major(y0_ref, tm) + _load_token_major(y1_ref, tm)
    out_ref[...] = _layer_norm(ALPHA * _load_token_major(h_ref, tm) + f, g_ref[...], b_ref[...])


def _combine(y, h_all, ln_g, ln_b, *, tm, tile0, n_tiles, slot_stride):
    d_model = ln_g.shape[1]
    chunks = d_model // LANES
    slot = lambda k: pl.BlockSpec((tm * chunks, LANES), lambda i: (i + tile0 + k * (slot_stride // tm), 0))
    return pl.pallas_call(
        _combine_body,
        grid=(n_tiles,),
        in_specs=[slot(0), slot(1), slot(0), _const_spec(ln_g.shape), _const_spec(ln_b.shape)],
        out_specs=pl.BlockSpec((tm, d_model), lambda i: (i, 0)),
        out_shape=jax.ShapeDtypeStruct((n_tiles * tm, d_model), F32),
        compiler_params=pltpu.CompilerParams(dimension_semantics=("arbitrary",), vmem_limit_bytes=VMEM_LIMIT),
        name="expert_combine",
    )(y, y, h_all, ln_g, ln_b)


def _layer(x_prompt, x_sample, cache_k, cache_v, cache_kidx, state_pool, page_table, w_in, w_pool, pool_scale,
           w_branch_a, w_branch_b, w_out, ln1_g, ln1_b, w_group, w_expert_router, w_gate, w_up, w_down,
           ln2_g, ln2_b):
    batch, seq, d_model = x_prompt.shape
    dec_batch = x_sample.shape[0]
    n_pages = page_table.shape[1]
    past = n_pages * PAGE_SIZE
    n_prompt = batch * seq
    kv_w = N_KV_HEADS * HEAD_DIM
    attn_w = N_HEADS * HEAD_DIM
    pool_width = w_pool.shape[0] * w_pool.shape[1]
    group = N_HEADS // N_KV_HEADS
    assert x_sample.shape[1] == 1 and seq % KEY_CHUNK == 0 and seq % MOE_ROW_TILE == 0
    assert dec_batch % 16 == 0 and n_pages % SAMPLE_PAGES_PER_CHUNK == 0
    assert (SAMPLE_PAGES_PER_CHUNK * PAGE_SIZE) % (past // SUBLANES) == 0

    x_p = x_prompt.reshape(n_prompt, d_model)
    x_s = x_sample.reshape(dec_batch, d_model)

    w_t = jnp.transpose(w_in)
    a_width = w_in.shape[1] - 2 * d_model
    wt_a = jnp.concatenate([w_t[:a_width], jnp.zeros((-a_width % LANES, d_model), F32)], axis=0).astype(BF16)
    wgat, wgbt = w_t[a_width:a_width + d_model], w_t[a_width + d_model:]
    wpool_b, pscale = w_pool.astype(BF16), pool_scale.reshape(1, pool_width)

    utail, a_p, qt, qit, kb, kib, kt, vt, vtb, kit, wt = _in_projection(
        x_p, wt_a, _rope_tables(np.arange(seq)), wpool_b, pscale, batch=batch, seq=seq)
    o_prompt = _prompt_attention(qit, wt, qt, kib, kb, vtb, batch=batch, seq=seq)

    u_s, a_s, q_s, qi_s, k_s, v_s, kiwi_s = _in_projection_sample(
        x_s, wt_a, _rope_tables(np.full((dec_batch,), past)), wpool_b, pscale, jnp.transpose(state_pool, (1, 0, 2)),
        sample_pos=past)
    qi_h = qi_s.reshape(dec_batch, IDX_HEADS, IDX_DIM)
    qi_h = jnp.concatenate([qi_h, jnp.zeros_like(qi_h)], axis=-1)
    in_group = (jnp.arange(N_HEADS)[:, None] // group == jnp.arange(N_KV_HEADS)[None, :])[None, :, :, None]
    qs = jnp.where(in_group, q_s.reshape(dec_batch, N_HEADS, 1, HEAD_DIM), 0).reshape(dec_batch, N_HEADS, kv_w)
    w_s = kiwi_s[:, IDX_DIM:IDX_DIM + IDX_HEADS].reshape(dec_batch, IDX_HEADS, 1)
    o_s = _sample_attention(page_table, qi_h, w_s, qs, kiwi_s.reshape(dec_batch, 1, LANES),
                            k_s.reshape(dec_batch, 1, kv_w), v_s.reshape(dec_batch, 1, kv_w),
                            _keys_minor(cache_kidx), _keys_minor(cache_k), _keys_minor(cache_v))
    o_s = o_s.reshape(dec_batch, N_KV_HEADS, group, N_KV_HEADS, HEAD_DIM)
    o_s = jnp.transpose(jnp.diagonal(o_s, axis1=1, axis2=3), (0, 3, 1, 2)).reshape(dec_batch, attn_w)
    o_sample = o_s.astype(BF16)

    w_route = jnp.concatenate([w_expert_router, w_group,
                               jnp.zeros((d_model, LANES - N_EXPERTS - N_GROUPS), F32)], axis=1)
    g1, b1 = ln1_g.reshape(1, d_model), ln1_b.reshape(1, d_model)
    g2, b2 = ln2_g.reshape(1, d_model), ln2_b.reshape(1, d_model)

    merged_p = _merge(x_p, a_p, o_prompt, wgat.astype(BF16), wgbt.astype(BF16), w_branch_a.astype(BF16),
                      w_branch_b.astype(BF16), tm=ROW_TILE, col_block=d_model, precise=False)
    h_all, dense_all = _out_projection(merged_p, x_p, w_out.astype(BF16), g1, b1, w_route.astype(BF16),
                                       tm=ROW_TILE, precise=False, zero_tiles=1)
    merged_s = _merge(x_s, a_s, o_sample, wgat, wgbt, w_branch_a, w_branch_b, tm=dec_batch,
                      col_block=MERGE_COL_CHUNK, precise=True)
    h_s, dense_s = _out_projection(merged_s, x_s, w_out, g1, b1, w_route, tm=dec_batch, precise=True)

    h_all = lax.dynamic_update_slice(h_all, h_s, (n_prompt * (d_model // LANES), 0))
    dense_all = lax.dynamic_update_slice(dense_all, dense_s, (n_prompt, 0))
    slot_stride = dense_all.shape[0]
    plan = _route_plan(dense_all, EXPERT_ROW_TILE)
    y2 = _experts(h_all, plan, w_gate, w_up, w_down, tm=EXPERT_ROW_TILE, slot_stride=slot_stride)
    y_prompt = _combine(y2, h_all, g2, b2, tm=ROW_TILE, tile0=0, n_tiles=n_prompt // ROW_TILE,
                        slot_stride=slot_stride)
    y_sample = _combine(y2, h_all, g2, b2, tm=dec_batch, tile0=n_prompt // dec_batch, n_tiles=1,
                        slot_stride=slot_stride)

    heads_t = lambda t: jnp.transpose(t.reshape(batch, N_KV_HEADS, HEAD_DIM, seq), (0, 3, 1, 2))
    pool_sample = jnp.concatenate([state_pool[:, 1:], u_s[:, None, :]], axis=1)
    return (y_prompt.reshape(batch, seq, d_model), y_sample.reshape(dec_batch, 1, d_model),
            heads_t(kt), heads_t(vt), jnp.transpose(kit, (0, 2, 1)), utail[:, 16 - POOL_STATE:],
            k_s.reshape(dec_batch, 1, N_KV_HEADS, HEAD_DIM), v_s.reshape(dec_batch, 1, N_KV_HEADS, HEAD_DIM),
            kiwi_s[:, :IDX_DIM].reshape(dec_batch, 1, IDX_DIM), pool_sample)


def kernel(x_prompt, x_sample, cache_k, cache_v, cache_kidx, state_pool, page_table, w_in, w_pool, pool_scale,
           w_branch_a, w_branch_b, w_out, ln1_g, ln1_b, w_group, w_expert_router, w_gate, w_up, w_down,
           ln2_g, ln2_b):
    assert w_in.shape[0] == DEPTH
    outs = _layer(x_prompt, x_sample, cache_k[0], cache_v[0], cache_kidx[0], state_pool[0], page_table, w_in[0],
                  w_pool[0], pool_scale[0], w_branch_a[0], w_branch_b[0], w_out[0], ln1_g[0], ln1_b[0],
                  w_group[0], w_expert_router[0], w_gate[0], w_up[0], w_down[0], ln2_g[0], ln2_b[0])
    y_p, y_s = outs[0], outs[1]
    return (y_p, y_s) + tuple(o[None] for o in outs[2:])
```

```python
import functools
import math

import jax
import jax.numpy as jnp
import numpy as np
from jax import lax
from jax.experimental import pallas as pl
from jax.experimental.pallas import tpu as pltpu

BF16 = jnp.bfloat16
F32 = jnp.float32
I32 = jnp.int32

PAGE_SIZE = 128
POOL_WINDOWS = (2, 4, 8, 16)
POOL_STATE = 15
N_HEADS = 16
N_KV_HEADS = 4
HEAD_DIM = 64
ROT_DIM = HEAD_DIM // 4
ROPE_THETA = 500000.0
IDX_HEADS = 16
IDX_DIM = 64
IDX_W_SCALE = (IDX_HEADS * IDX_DIM) ** -0.5
TOPK_MAX = 256
N_GROUPS = 4
EXPERTS_PER_GROUP = 8
N_EXPERTS = N_GROUPS * EXPERTS_PER_GROUP
LN_EPS = 1e-5
DEPTH = 1
ALPHA = (2 * DEPTH) ** 0.25

LANES = 128
SUBLANES = 8
Q_TILE = 128
KEY_CHUNK = 512
ROW_TILE = 512
MOE_ROW_TILE = 512
VMEM_LIMIT = 56 * 1024 * 1024

INT_MIN = -2 ** 31
NEG_INF_KEY = int(np.int32(np.uint32(0xFF800000) ^ np.uint32(0x7FFFFFFF)))
NT_DIMS = (((1,), (1,)), ((), ()))


def _sortable_key(x):
    bits = lax.bitcast_convert_type(x, I32)
    return bits ^ ((bits >> 31) & 0x7FFFFFFF)


def _const_spec(shape):
    nd = len(shape)
    return pl.BlockSpec(shape, lambda *_: (0,) * nd, pipeline_mode=pl.Buffered(1))


def _project(xb, wt_ref, cos, nsin, psin, pool_width):
    tm = xb.shape[0]
    lo = lax.broadcasted_iota(I32, (tm, LANES), 1) < HEAD_DIM
    attn_w, kv_w, idx_w = N_HEADS * HEAD_DIM, N_KV_HEADS * HEAD_DIM, IDX_HEADS * IDX_DIM
    off_q = pool_width
    off_k = off_q + attn_w
    off_v = off_k + kv_w
    off_qi = off_v + kv_w
    off_kiwi = off_qi + idx_w

    def proj(c0, width):
        return lax.dot_general(xb, wt_ref[c0:c0 + width, :], NT_DIMS, preferred_element_type=F32)

    def rope(z, c, s1, s2):
        return z * c + pltpu.roll(z, LANES - ROT_DIM // 2, 1) * s1 + pltpu.roll(z, ROT_DIM // 2, 1) * s2

    def rope_tiles(z):
        return [rope(z[:, j * LANES:(j + 1) * LANES], cos, nsin, psin) for j in range(z.shape[1] // LANES)]

    u = proj(0, pool_width)
    q = [t * (HEAD_DIM ** -0.5) for t in rope_tiles(proj(off_q, attn_w))]
    qi = rope_tiles(proj(off_qi, idx_w))
    k = rope_tiles(proj(off_k, kv_w))
    v = proj(off_v, kv_w)
    kw = rope(proj(off_kiwi, LANES), jnp.where(lo, cos, IDX_W_SCALE), jnp.where(lo, nsin, 0.0),
              jnp.where(lo, psin, 0.0))
    return u, q, qi, k, v, kw


def _pool_mix(d, g, wpool_ref, pscale_ref, gw):
    z = jnp.dot(d.astype(BF16), wpool_ref[g], preferred_element_type=F32)
    return z * pscale_ref[:, g * gw:(g + 1) * gw]


def _inproj_body(x_ref, wt_ref, tc_ref, ts1_ref, ts2_ref, wpool_ref, pscale_ref,
                 utail_ref, a_ref, qt_ref, qit_ref, kb_ref, kib_ref, kt_ref, vt_ref, vtb_ref, kit_ref, wt_out_ref,
                 uext_ref, *, tm, tiles_per_batch, pool_width):
    i = pl.program_id(0)
    u, q, qi, k, v, kw = _project(x_ref[...].astype(BF16), wt_ref, tc_ref[...], ts1_ref[...], ts2_ref[...],
                                  pool_width)
    lo = lax.broadcasted_iota(I32, (tm, LANES), 1) < HEAD_DIM
    gw = pool_width // len(POOL_WINDOWS)

    def blocks(tile):
        return [(slice(blk * Q_TILE, (blk + 1) * Q_TILE), tile[blk * Q_TILE:(blk + 1) * Q_TILE].T)
                for blk in range(tm // Q_TILE)]

    def store_head_t(dst_ref, h, tile):
        for blk, (_, t) in enumerate(blocks(tile)):
            dst_ref[blk, :, h * Q_TILE:(h + 1) * Q_TILE] = t.astype(BF16)

    for j, p in enumerate(q):
        pr = pltpu.roll(p, HEAD_DIM, 1)
        if (j // 2) % 2 == 0:
            store_head_t(qt_ref, 2 * j, jnp.where(lo, p, 0.0))
            store_head_t(qt_ref, 2 * j + 1, jnp.where(lo, pr, 0.0))
        else:
            store_head_t(qt_ref, 2 * j, jnp.where(lo, 0.0, pr))
            store_head_t(qt_ref, 2 * j + 1, jnp.where(lo, 0.0, p))
    for j, p in enumerate(qi):
        store_head_t(qit_ref, 2 * j, jnp.where(lo, p, 0.0))
        store_head_t(qit_ref, 2 * j + 1, jnp.where(lo, pltpu.roll(p, IDX_DIM, 1), 0.0))

    for j, kr in enumerate(k):
        lanes = slice(j * LANES, (j + 1) * LANES)
        kb_ref[:, lanes] = kr.astype(BF16)
        for cols, t in blocks(kr):
            kt_ref[0, lanes, cols] = t
        for cols, t in blocks(v[:, lanes]):
            vt_ref[0, lanes, cols] = t
            vtb_ref[lanes, cols] = t.astype(BF16)
    kib_ref[...] = kw.astype(BF16)
    for cols, t in blocks(kw):
        kit_ref[0, :, cols] = t[:IDX_DIM, :]
        wt_out_ref[:, cols] = t[IDX_DIM:IDX_DIM + IDX_HEADS, :]

    first = (i % tiles_per_batch) == 0

    @pl.when(first)
    def _():
        uext_ref[0:16, :] = jnp.zeros((16, pool_width), F32)

    @pl.when(jnp.logical_not(first))
    def _():
        uext_ref[0:16, :] = uext_ref[tm:tm + 16, :]

    uext_ref[16:16 + tm, :] = u
    utail_ref[0] = u[tm - 16:tm, :]
    pos = (i % tiles_per_batch) * tm + lax.broadcasted_iota(I32, (tm, 1), 0)
    for g, w in enumerate(POOL_WINDOWS):
        lanes = slice(g * gw, (g + 1) * gw)
        acc = u[:, lanes]
        for jj in range(1, w):
            acc = acc + uext_ref[16 - jj:16 - jj + tm, lanes]
        cnt = jnp.minimum(w, pos + 1).astype(F32)
        d = acc / cnt - u[:, lanes]
        a_ref[:, lanes] = _pool_mix(d, g, wpool_ref, pscale_ref, gw).astype(BF16)


def _in_projection(x, wt_a, tabs, wpool, pscale, *, batch, seq):
    n_rows, d_model = x.shape
    tm = ROW_TILE
    pool_width = wpool.shape[0] * wpool.shape[1]
    nblk = n_rows // Q_TILE
    kv_w = N_KV_HEADS * HEAD_DIM
    tpb = seq // tm
    row = lambda width: pl.BlockSpec((tm, width), lambda i: (i, 0))
    col = lambda height: pl.BlockSpec((height, tm), lambda i: (0, i))
    per_seq = lambda height: pl.BlockSpec((1, height, tm), lambda i: (i // tpb, 0, i % tpb))
    tab = pl.BlockSpec((tm, LANES), lambda i: (i % tpb, 0))
    hm = pl.BlockSpec((tm // Q_TILE, LANES, N_HEADS * Q_TILE), lambda i: (i, 0, 0))
    body = functools.partial(_inproj_body, tm=tm, tiles_per_batch=tpb, pool_width=pool_width)
    return pl.pallas_call(
        body,
        grid=(n_rows // tm,),
        in_specs=[row(d_model), _const_spec(wt_a.shape), tab, tab, tab,
                  _const_spec(wpool.shape), _const_spec(pscale.shape)],
        out_specs=[pl.BlockSpec((1, 16, pool_width), lambda i: (i // tpb, 0, 0)),
                   row(pool_width), hm, hm, row(kv_w), row(LANES), per_seq(kv_w), per_seq(kv_w), col(kv_w),
                   per_seq(IDX_DIM), col(IDX_HEADS)],
        out_shape=[
            jax.ShapeDtypeStruct((batch, 16, pool_width), F32),
            jax.ShapeDtypeStruct((n_rows, pool_width), BF16),
            jax.ShapeDtypeStruct((nblk, LANES, N_HEADS * Q_TILE), BF16),
            jax.ShapeDtypeStruct((nblk, LANES, IDX_HEADS * Q_TILE), BF16),
            jax.ShapeDtypeStruct((n_rows, kv_w), BF16),
            jax.ShapeDtypeStruct((n_rows, LANES), BF16),
            jax.ShapeDtypeStruct((batch, kv_w, seq), F32),
            jax.ShapeDtypeStruct((batch, kv_w, seq), F32),
            jax.ShapeDtypeStruct((kv_w, n_rows), BF16),
            jax.ShapeDtypeStruct((batch, IDX_DIM, seq), F32),
            jax.ShapeDtypeStruct((IDX_HEADS, n_rows), F32),
        ],
        scratch_shapes=[pltpu.VMEM((tm + 16, pool_width), F32)],
        compiler_params=pltpu.CompilerParams(dimension_semantics=("arbitrary",), vmem_limit_bytes=VMEM_LIMIT),
        name="in_projection",
    )(x, wt_a, *tabs, wpool, pscale)


def _inproj_sample_body(x_ref, wt_ref, tc_ref, ts1_ref, ts2_ref, wpool_ref, pscale_ref, state_ref,
                        u_ref, a_ref, q_ref, qi_ref, k_ref, v_ref, kiwi_ref, *, pool_width, sample_pos):
    u, q, qi, k, v, kw = _project(x_ref[...].astype(BF16), wt_ref, tc_ref[...], ts1_ref[...], ts2_ref[...],
                                  pool_width)
    gw = pool_width // len(POOL_WINDOWS)
    u_ref[...] = u
    for j, t in enumerate(q):
        q_ref[:, j * LANES:(j + 1) * LANES] = t.astype(BF16)
    for j, t in enumerate(qi):
        qi_ref[:, j * LANES:(j + 1) * LANES] = t.astype(BF16)
    for j, t in enumerate(k):
        k_ref[:, j * LANES:(j + 1) * LANES] = t
    v_ref[...] = v
    kiwi_ref[...] = kw
    for g, w in enumerate(POOL_WINDOWS):
        lanes = slice(g * gw, (g + 1) * gw)
        acc = u[:, lanes]
        for jj in range(1, w):
            acc = acc + state_ref[POOL_STATE - jj, :, lanes]
        d = acc / float(min(w, sample_pos + 1)) - u[:, lanes]
        a_ref[:, lanes] = _pool_mix(d, g, wpool_ref, pscale_ref, gw).astype(BF16)


def _in_projection_sample(x, wt_a, tabs, wpool, pscale, state_t, *, sample_pos):
    n_rows, d_model = x.shape
    pool_width = wpool.shape[0] * wpool.shape[1]
    kv_w = N_KV_HEADS * HEAD_DIM
    full = lambda shape: pl.BlockSpec(shape, lambda i: (0,) * len(shape))
    body = functools.partial(_inproj_sample_body, pool_width=pool_width, sample_pos=sample_pos)
    out_shape = [
        jax.ShapeDtypeStruct((n_rows, pool_width), F32),
        jax.ShapeDtypeStruct((n_rows, pool_width), BF16),
        jax.ShapeDtypeStruct((n_rows, N_HEADS * HEAD_DIM), BF16),
        jax.ShapeDtypeStruct((n_rows, IDX_HEADS * IDX_DIM), BF16),
        jax.ShapeDtypeStruct((n_rows, kv_w), F32),
        jax.ShapeDtypeStruct((n_rows, kv_w), F32),
        jax.ShapeDtypeStruct((n_rows, LANES), F32),
    ]
    return pl.pallas_call(
        body,
        grid=(1,),
        in_specs=[full(x.shape), _const_spec(wt_a.shape), full(tabs[0].shape), full(tabs[1].shape),
                  full(tabs[2].shape), _const_spec(wpool.shape), _const_spec(pscale.shape), full(state_t.shape)],
        out_specs=[full(s.shape) for s in out_shape],
        out_shape=out_shape,
        compiler_params=pltpu.CompilerParams(dimension_semantics=("arbitrary",), vmem_limit_bytes=VMEM_LIMIT),
        name="in_projection_sample",
    )(x, wt_a, *tabs, wpool, pscale, state_t)


def _rope_tables(positions):
    pos = np.asarray(positions, np.float32)
    n = pos.shape[0]
    half = ROT_DIM // 2
    inv = np.exp(-np.arange(half, dtype=np.float32) * np.float32(math.log(ROPE_THETA) / half)).astype(np.float32)
    ang = (pos[:, None] * inv[None, :]).astype(np.float32).astype(np.float64)
    cos, sin = np.cos(ang).astype(np.float32), np.sin(ang).astype(np.float32)
    zeros = np.zeros((n, half), np.float32)
    rest = HEAD_DIM - ROT_DIM
    c = np.concatenate([cos, cos, np.ones((n, rest), np.float32)], axis=1)
    s1 = np.concatenate([-sin, zeros, np.zeros((n, rest), np.float32)], axis=1)
    s2 = np.concatenate([zeros, sin, np.zeros((n, rest), np.float32)], axis=1)
    return tuple(jnp.asarray(np.concatenate([t, t], axis=1)) for t in (c, s1, s2))


def _topk_threshold(count_ge, init, topk):
    def step(it, t):
        cand = t + jnp.left_shift(jnp.int32(1), 31 - it)
        return jnp.where(count_ge(cand) >= topk, cand, t)
    return lax.fori_loop(0, 32, step, init)


def _fold_slabs(x, op):
    assert x.shape[0] & (x.shape[0] - 1) == 0
    while x.shape[0] > 1:
        half = x.shape[0] // 2
        x = op(x[:half], x[half:])
    return x[0]


def _prompt_attn_body(qit_ref, wt_ref, qt_ref, kib_ref, kb_ref, vt_ref, o_ref,
                      sc_ref, thr_ref, tie_ref, m_ref, l_ref, acc_ref, *, topk):
    i = pl.program_id(1)
    tq, kc = Q_TILE, KEY_CHUNK
    seq = kib_ref.shape[0]
    index_bits = seq.bit_length()
    n_keys = (i + 1) * tq
    n_chunks = (n_keys + kc - 1) // kc
    qpos = i * tq + lax.broadcasted_iota(I32, (1, tq), 1)
    group = N_HEADS // N_KV_HEADS
    gcols = group * tq

    def score_chunk(c, carry):
        k0 = pl.multiple_of(c * kc, kc)
        kchunk = kib_ref[pl.ds(k0, kc), :]
        acc = jnp.zeros((kc, tq), F32)
        for hg in range(IDX_HEADS // group):
            d = jnp.dot(kchunk, qit_ref[0, :, hg * gcols:(hg + 1) * gcols], preferred_element_type=F32)
            for r in range(group):
                h = hg * group + r
                acc = acc + jnp.maximum(d[:, r * tq:(r + 1) * tq], 0.0) * wt_ref[h:h + 1, :]
        kpos = k0 + lax.broadcasted_iota(I32, (kc, 1), 0)
        sc_ref[pl.ds(k0, kc), :] = _sortable_key(jnp.where(kpos <= qpos, acc, -jnp.inf))
        return carry

    lax.fori_loop(0, n_chunks, score_chunk, 0)

    thr_ref[...] = jnp.full((1, tq), NEG_INF_KEY + 1, I32)
    tie_ref[...] = jnp.full((1, tq), seq, I32)

    @pl.when(n_keys > topk)
    def _():
        def count(pred):
            def blk(c, acc):
                k0 = pl.multiple_of(c * kc, kc)
                keys = sc_ref[pl.ds(k0, kc), :].reshape(kc // SUBLANES, SUBLANES, tq)
                kpos = k0 + lax.broadcasted_iota(I32, (kc, 1), 0).reshape(kc // SUBLANES, SUBLANES, 1)
                return acc + _fold_slabs(jnp.where(pred(keys, kpos), 1.0, 0.0), jnp.add)
            acc = lax.fori_loop(0, n_chunks, blk, jnp.zeros((SUBLANES, tq), F32))
            return jnp.sum(acc, axis=0, keepdims=True)

        def count_ge(cand):
            candb = jnp.broadcast_to(cand, (SUBLANES, tq))
            return count(lambda keys, kpos: keys >= candb)

        t = _topk_threshold(count_ge, jnp.full((1, tq), INT_MIN, I32), float(topk))
        t = jnp.where(qpos + 1 > topk, t, NEG_INF_KEY + 1)
        thr_ref[...] = t
        n_above = count_ge(t + 1)
        need = float(topk) - n_above
        tied = (count_ge(t) - n_above > need) & (qpos + 1 > topk)

        @pl.when(jnp.max(jnp.where(tied, 1.0, 0.0)) > 0.0)
        def _():
            tb = jnp.broadcast_to(t, (SUBLANES, tq))

            def step(it, d):
                cand = d + jnp.left_shift(jnp.int32(1), index_bits - 1 - it)
                candb = jnp.broadcast_to(cand, (SUBLANES, tq))
                before = count(lambda keys, kpos: (keys == tb) & (kpos < candb))
                return jnp.where(before < need, cand, d)
            last = lax.fori_loop(0, index_bits, step, jnp.zeros((1, tq), I32))
            tie_ref[...] = jnp.where(tied, last, seq)

    m_ref[...] = jnp.full(m_ref.shape, -jnp.inf, F32)
    l_ref[...] = jnp.zeros(l_ref.shape, F32)
    acc_ref[...] = jnp.zeros(acc_ref.shape, F32)

    def attn_chunk(c, carry):
        k0 = pl.multiple_of(c * kc, kc)
        keys = sc_ref[pl.ds(k0, kc), :]
        kpos = k0 + lax.broadcasted_iota(I32, (kc, 1), 0)
        sel = (keys > thr_ref[...]) | ((keys == thr_ref[...]) & (kpos <= tie_ref[...]))
        for g in range(N_KV_HEADS):
            pair = slice((g // 2) * LANES, (g // 2 + 1) * LANES)
            cols = slice(g * gcols, (g + 1) * gcols)
            s = jnp.dot(kb_ref[pl.ds(k0, kc), pair], qt_ref[0, :, cols], preferred_element_type=F32)
            s = jnp.concatenate([jnp.where(sel, s[:, r * tq:(r + 1) * tq], -jnp.inf) for r in range(group)], axis=1)
            m_old = m_ref[:, cols]
            m_new = jnp.maximum(m_old, jnp.max(s, axis=0, keepdims=True))
            m_safe = jnp.where(m_new == -jnp.inf, 0.0, m_new)
            alpha = jnp.exp(m_old - m_safe)
            p = jnp.exp(s - m_safe)
            l_ref[:, cols] = alpha * l_ref[:, cols] + jnp.sum(p, axis=0, keepdims=True)
            pv = jnp.dot(vt_ref[pair, pl.ds(k0, kc)], p.astype(BF16), preferred_element_type=F32)
            acc_ref[g] = alpha * acc_ref[g] + pv
            m_ref[:, cols] = m_new
        return carry

    lax.fori_loop(0, n_chunks, attn_chunk, 0)
    l_all = l_ref[...]

    for j in range(N_HEADS // 2):
        halves = []
        for h in (2 * j, 2 * j + 1):
            g, r = h // group, h % group
            rows = slice((g % 2) * HEAD_DIM, (g % 2 + 1) * HEAD_DIM)
            cols = slice(r * tq, (r + 1) * tq)
            halves.append(acc_ref[g, rows, cols] / l_all[:, g * gcols + r * tq:g * gcols + (r + 1) * tq])
        o_ref[:, j * LANES:(j + 1) * LANES] = jnp.concatenate(halves, axis=0).T.astype(BF16)


def _prompt_attention(qit, wt, qt, kib, kb, vt, *, batch, seq):
    nq = seq // Q_TILE
    topk = min(TOPK_MAX, seq // 4)
    kv_w = N_KV_HEADS * HEAD_DIM
    per_q = lambda b, i: (b * nq + i, 0, 0)
    body = functools.partial(_prompt_attn_body, topk=topk)
    return pl.pallas_call(
        body,
        grid=(batch, nq),
        in_specs=[
            pl.BlockSpec((1, LANES, IDX_HEADS * Q_TILE), per_q),
            pl.BlockSpec((IDX_HEADS, Q_TILE), lambda b, i: (0, b * nq + i)),
            pl.BlockSpec((1, LANES, N_HEADS * Q_TILE), per_q),
            pl.BlockSpec((seq, LANES), lambda b, i: (b, 0)),
            pl.BlockSpec((seq, kv_w), lambda b, i: (b, 0)),
            pl.BlockSpec((kv_w, seq), lambda b, i: (0, b)),
        ],
        out_specs=pl.BlockSpec((Q_TILE, N_HEADS * HEAD_DIM), lambda b, i: (b * nq + i, 0)),
        out_shape=jax.ShapeDtypeStruct((batch * seq, N_HEADS * HEAD_DIM), BF16),
        scratch_shapes=[
            pltpu.VMEM((seq, Q_TILE), I32),
            pltpu.VMEM((1, Q_TILE), I32),
            pltpu.VMEM((1, Q_TILE), I32),
            pltpu.VMEM((1, N_HEADS * Q_TILE), F32),
            pltpu.VMEM((1, N_HEADS * Q_TILE), F32),
            pltpu.VMEM((N_KV_HEADS, LANES, (N_HEADS // N_KV_HEADS) * Q_TILE), F32),
        ],
        compiler_params=pltpu.CompilerParams(dimension_semantics=("arbitrary", "arbitrary"),
                                             vmem_limit_bytes=VMEM_LIMIT),
        name="prompt_attention",
    )(qit, wt, qt, kib, kb, vt)


SAMPLE_PAGES_PER_CHUNK = 16


def _sample_attn_body(pt_ref, qi_ref, w_ref, qs_ref, kicur_ref, kcur_ref, vcur_ref, ckidx_hbm, ck_hbm, cv_hbm,
                      o_ref, kidx_buf, k_buf, v_buf, sc_ref, sem_idx, sem_k, sem_v, *, n_pages, topk):
    b = pl.program_id(0)
    final_step = pl.num_programs(0) - 1
    ppc = SAMPLE_PAGES_PER_CHUNK
    n_chunks = n_pages // ppc
    ck = ppc * PAGE_SIZE
    seg = sc_ref.shape[1]
    segs_per_chunk = ck // seg

    def idx_copy(seq, p):
        return pltpu.make_async_copy(ckidx_hbm.at[pt_ref[seq, p]],
                                     kidx_buf.at[seq % 2, :, pl.ds(p * PAGE_SIZE, PAGE_SIZE)], sem_idx.at[seq % 2])

    def kv_copies(seq, c, p):
        slot = c % 2
        page = pt_ref[seq, c * ppc + p]
        window = pl.ds(p * PAGE_SIZE, PAGE_SIZE)
        return (pltpu.make_async_copy(ck_hbm.at[page], k_buf.at[slot, :, window], sem_k.at[slot]),
                pltpu.make_async_copy(cv_hbm.at[page], v_buf.at[slot, :, window], sem_v.at[slot]))

    def start_idx(seq):
        for p in range(n_pages):
            idx_copy(seq, p).start()

    def start_kv(seq, c):
        for p in range(ppc):
            for cp in kv_copies(seq, c, p):
                cp.start()

    def wait_kv(seq, c):
        for p in range(ppc):
            for cp in kv_copies(seq, c, p):
                cp.wait()

    @pl.when(b == 0)
    def _():
        start_idx(b)
        start_kv(b, 0)

    for p in range(n_pages):
        idx_copy(b, p).wait()

    qi = qi_ref[0][:, :IDX_DIM]
    w = w_ref[0]

    def head_mix(d):
        return jnp.sum(jnp.maximum(d, 0.0) * w, axis=0, keepdims=True)

    def bf16_products(a, row):
        return jnp.sum(a.astype(F32) * row.astype(BF16).astype(F32), axis=1, keepdims=True)

    for c in range(n_chunks):
        d = jnp.dot(qi, kidx_buf[b % 2, :, c * ck:(c + 1) * ck].astype(BF16), preferred_element_type=F32)
        keys = _sortable_key(head_mix(d))
        for r in range(segs_per_chunk):
            sc_ref[c * segs_per_chunk + r:c * segs_per_chunk + r + 1, :] = keys[:, r * seg:(r + 1) * seg]
    cur_key = _sortable_key(head_mix(bf16_products(qi, kicur_ref[0][:, :IDX_DIM])))

    vreg = (SUBLANES, LANES)
    past = n_pages * PAGE_SIZE
    if past + 1 > topk:
        cur_key_v = jnp.broadcast_to(cur_key, vreg)

        def total(hits):
            return jnp.broadcast_to(jnp.sum(jnp.sum(hits, axis=1, keepdims=True), axis=0, keepdims=True), vreg)

        def count_ge(cand):
            hits = jnp.where(cur_key_v >= cand, 1.0 / (SUBLANES * LANES), 0.0)
            for j in range(seg // LANES):
                hits = hits + jnp.where(sc_ref[:, j * LANES:(j + 1) * LANES] >= cand, 1.0, 0.0)
            return total(hits)

        thr_v = _topk_threshold(count_ge, jnp.full(vreg, INT_MIN, I32), float(topk))
        need = float(topk) - count_ge(thr_v + 1)
        pos0 = lax.broadcasted_iota(I32, vreg, 0) * seg + lax.broadcasted_iota(I32, vreg, 1)
        index_bits = (past + 1).bit_length()

        def step(it, d):
            cand = d + jnp.left_shift(jnp.int32(1), index_bits - 1 - it)
            hits = jnp.zeros(vreg, F32)
            for j in range(seg // LANES):
                tied = (sc_ref[:, j * LANES:(j + 1) * LANES] == thr_v) & (pos0 + j * LANES < cand)
                hits = hits + jnp.where(tied, 1.0, 0.0)
            return jnp.where(total(hits) < need, cand, d)
        thr = thr_v[0:1, 0:1]
        tied = jnp.max(jnp.where(count_ge(thr_v) - count_ge(thr_v + 1) > need, 1.0, 0.0)) > 0.0
        last = lax.cond(tied, lambda: lax.fori_loop(0, index_bits, step, jnp.zeros(vreg, I32)),
                        lambda: jnp.full(vreg, past, I32))[0:1, 0:1]
    else:
        thr = jnp.full((1, 1), NEG_INF_KEY + 1, I32)
        last = jnp.full((1, 1), past, I32)

    def selected(keys, kpos):
        return (keys > thr) | ((keys == thr) & (kpos <= last))

    qs = qs_ref[0]
    m = jnp.full((N_HEADS, 1), -jnp.inf, F32)
    l = jnp.zeros((N_HEADS, 1), F32)
    acc = jnp.zeros((N_HEADS, N_KV_HEADS * HEAD_DIM), F32)

    def update(m, l, acc, s, sel, pv):
        s = jnp.where(sel, s, -jnp.inf)
        m_new = jnp.maximum(m, jnp.max(s, axis=1, keepdims=True))
        m_safe = jnp.where(m_new == -jnp.inf, 0.0, m_new)
        alpha = jnp.exp(m - m_safe)
        p = jnp.exp(s - m_safe)
        l = alpha * l + jnp.sum(p, axis=1, keepdims=True)
        acc = alpha * acc + pv(p.astype(BF16))
        return m_new, l, acc

    def prefetch_next_sequence():
        @pl.when(b < final_step)
        def _():
            start_idx(b + 1)
            start_kv(b + 1, 0)

    slot0_free_early = n_chunks % 2 == 0
    for c in range(n_chunks):
        if c + 1 < n_chunks:
            start_kv(b, c + 1)
        elif slot0_free_early:
            prefetch_next_sequence()
        wait_kv(b, c)
        slot = c % 2
        s = jnp.dot(qs, k_buf[slot].astype(BF16), preferred_element_type=F32)
        vt = v_buf[slot].astype(BF16)
        keys = jnp.concatenate([sc_ref[c * segs_per_chunk + r:c * segs_per_chunk + r + 1, :]
                                for r in range(segs_per_chunk)], axis=1)
        sel = selected(keys, c * ck + lax.broadcasted_iota(I32, (1, ck), 1))
        m, l, acc = update(m, l, acc, s, sel, lambda p: lax.dot_general(p, vt, NT_DIMS, preferred_element_type=F32))
    if not slot0_free_early:
        prefetch_next_sequence()
    v_cur = vcur_ref[0].astype(BF16).astype(F32)
    m, l, acc = update(m, l, acc, bf16_products(qs, kcur_ref[0]), selected(cur_key, past),
                       lambda p: p.astype(F32) * v_cur)
    o_ref[0] = acc / l


def _keys_minor(cache):
    pages, page_size = cache.shape[:2]
    return jnp.moveaxis(cache, 1, -1).reshape(pages, -1, page_size)


def _sample_attention(page_table, qi_s, w_s, qs, kicur, kcur, vcur, cache_kidx, cache_k, cache_v):
    dec_batch, n_pages = page_table.shape
    past = n_pages * PAGE_SIZE
    topk = min(TOPK_MAX, (past + 1) // 4)
    kv_w = N_KV_HEADS * HEAD_DIM
    ppc = SAMPLE_PAGES_PER_CHUNK
    per_b = lambda b, pt: (b, 0, 0)
    any_spec = pl.BlockSpec(memory_space=pl.ANY)
    grid_spec = pltpu.PrefetchScalarGridSpec(
        num_scalar_prefetch=1,
        grid=(dec_batch,),
        in_specs=[
            pl.BlockSpec((1, IDX_HEADS, LANES), per_b),
            pl.BlockSpec((1, IDX_HEADS, 1), per_b),
            pl.BlockSpec((1, N_HEADS, kv_w), per_b),
            pl.BlockSpec((1, 1, LANES), per_b),
            pl.BlockSpec((1, 1, kv_w), per_b),
            pl.BlockSpec((1, 1, kv_w), per_b),
            any_spec, any_spec, any_spec,
        ],
        out_specs=pl.BlockSpec((1, N_HEADS, kv_w), per_b),
        scratch_shapes=[
            pltpu.VMEM((2, IDX_DIM, past), F32),
            pltpu.VMEM((2, kv_w, ppc * PAGE_SIZE), F32),
            pltpu.VMEM((2, kv_w, ppc * PAGE_SIZE), F32),
            pltpu.VMEM((SUBLANES, past // SUBLANES), I32),
            pltpu.SemaphoreType.DMA((2,)),
            pltpu.SemaphoreType.DMA((2,)),
            pltpu.SemaphoreType.DMA((2,)),
        ],
    )
    body = functools.partial(_sample_attn_body, n_pages=n_pages, topk=topk)
    return pl.pallas_call(
        body,
        grid_spec=grid_spec,
        out_shape=jax.ShapeDtypeStruct((dec_batch, N_HEADS, kv_w), F32),
        compiler_params=pltpu.CompilerParams(dimension_semantics=("arbitrary",), vmem_limit_bytes=VMEM_LIMIT),
        name="sample_attention",
    )(page_table, qi_s, w_s, qs, kicur, kcur, vcur, cache_kidx, cache_k, cache_v)


def _mm(a, b, precise, *, nt=False):
    dims = NT_DIMS if nt else (((1,), (0,)), ((), ()))
    if precise:
        return lax.dot_general(a.astype(F32), b, dims, precision=lax.Precision.HIGHEST, preferred_element_type=F32)
    return lax.dot_general(a.astype(BF16), b, dims, preferred_element_type=F32)


def _operand_dtype(precise):
    return F32 if precise else BF16


MERGE_COL_CHUNK = 512


def _merge_body(x_ref, a_ref, o_ref, wgat_ref, wgbt_ref, wba_ref, wbb_ref, out_ref, *, precise):
    x, a, o = x_ref[...], a_ref[...], o_ref[...]
    cc = MERGE_COL_CHUNK
    for c in range(out_ref.shape[1] // cc):
        cols = slice(c * cc, (c + 1) * cc)
        ga = _mm(x, wgat_ref[cols, :], precise, nt=True)
        gb = _mm(x, wgbt_ref[cols, :], precise, nt=True)
        ya = _mm(a, wba_ref[:, cols], precise)
        yo = _mm(o, wbb_ref[:, cols], precise)
        out_ref[:, cols] = (jax.nn.sigmoid(ga) * ya + jax.nn.sigmoid(gb) * yo).astype(out_ref.dtype)


def _merge(x, a, o, wgat, wgbt, wba, wbb, *, tm, col_block, precise):
    n_rows, d_model = x.shape
    mode = dict(pipeline_mode=pl.Buffered(1)) if col_block == d_model else {}
    gate_spec = pl.BlockSpec((col_block, d_model), lambda i, j: (j, 0), **mode)
    branch_spec = lambda w: pl.BlockSpec((w.shape[0], col_block), lambda i, j: (0, j), **mode)
    row = lambda width: pl.BlockSpec((tm, width), lambda i, j: (i, 0))
    body = functools.partial(_merge_body, precise=precise)
    return pl.pallas_call(
        body,
        grid=(n_rows // tm, d_model // col_block),
        in_specs=[row(d_model), row(a.shape[1]), row(o.shape[1]), gate_spec, gate_spec,
                  branch_spec(wba), branch_spec(wbb)],
        out_specs=pl.BlockSpec((tm, col_block), lambda i, j: (i, j)),
        out_shape=jax.ShapeDtypeStruct((n_rows, d_model), _operand_dtype(precise)),
        compiler_params=pltpu.CompilerParams(dimension_semantics=("arbitrary", "arbitrary"),
                                             vmem_limit_bytes=VMEM_LIMIT),
        name="branch_merge",
    )(x, a, o, wgat, wgbt, wba, wbb)


def _layer_norm(x, g, b):
    mu = jnp.mean(x, axis=-1, keepdims=True)
    var = jnp.mean(jnp.square(x - mu), axis=-1, keepdims=True)
    return (x - mu) * lax.rsqrt(var + LN_EPS) * g + b


def _route(logits):
    lane = lax.broadcasted_iota(I32, logits.shape, 1)
    lane_f = lane.astype(F32)
    big = float(LANES)

    def masked_softmax(mask):
        x = jnp.where(mask, logits, -jnp.inf)
        e = jnp.exp(x - jnp.max(x, axis=1, keepdims=True))
        return jnp.where(mask, e / jnp.sum(e, axis=1, keepdims=True), -1.0)

    def top1(p):
        best = jnp.max(p, axis=1, keepdims=True)
        idx = jnp.min(jnp.where(p == best, lane_f, big), axis=1, keepdims=True)
        return best, idx

    gp = masked_softmax((lane >= N_EXPERTS) & (lane < N_EXPERTS + N_GROUPS))
    g_p, g_lane = top1(gp)
    g_idx = g_lane.astype(I32) - N_EXPERTS
    ep = masked_softmax((lane < N_EXPERTS) & ((lane >> int(math.log2(EXPERTS_PER_GROUP))) == g_idx))
    p1, i1 = top1(ep)
    p2, i2 = top1(jnp.where(lane_f == i1, -1.0, ep))
    denom = p1 + p2
    return jnp.where(lane_f == i1, g_p * p1 / denom, 0.0) + jnp.where(lane_f == i2, g_p * p2 / denom, 0.0)


def _store_token_major(ref, x):
    chunks = x.shape[1] // LANES
    for c in range(chunks):
        ref[pl.ds(c, x.shape[0], stride=chunks), :] = x[:, c * LANES:(c + 1) * LANES]


def _load_token_major(ref, n_tokens):
    chunks = ref.shape[0] // n_tokens
    return jnp.concatenate([ref[pl.ds(c, n_tokens, stride=chunks), :] for c in range(chunks)], axis=1)


def _outproj_body(m_ref, x_ref, wout_ref, g_ref, b_ref, wr_ref, h_ref, dense_ref, *, precise, n_real):
    i = pl.program_id(0)

    @pl.when(i < n_real)
    def _():
        mo = _mm(m_ref[...], wout_ref[...], precise)
        h = _layer_norm(ALPHA * x_ref[...] + mo, g_ref[...], b_ref[...])
        _store_token_major(h_ref, h)
        dense_ref[...] = _route(_mm(h, wr_ref[...], precise))

    @pl.when(i >= n_real)
    def _():
        h_ref[...] = jnp.zeros(h_ref.shape, F32)
        dense_ref[...] = jnp.zeros(dense_ref.shape, F32)


def _out_projection(merged, x, wout, ln_g, ln_b, w_route, *, tm, precise, zero_tiles=0):
    n_rows, d_model = x.shape
    n_real = n_rows // tm
    chunks = d_model // LANES
    body = functools.partial(_outproj_body, precise=precise, n_real=n_real)
    in_row = lambda width: pl.BlockSpec((tm, width), lambda i: (jnp.minimum(i, n_real - 1), 0))
    n_out = n_rows + zero_tiles * tm
    return pl.pallas_call(
        body,
        grid=(n_real + zero_tiles,),
        in_specs=[in_row(d_model), in_row(d_model), _const_spec(wout.shape),
                  _const_spec(ln_g.shape), _const_spec(ln_b.shape), _const_spec(w_route.shape)],
        out_specs=[pl.BlockSpec((tm * chunks, LANES), lambda i: (i, 0)), pl.BlockSpec((tm, LANES), lambda i: (i, 0))],
        out_shape=[jax.ShapeDtypeStruct((n_out * chunks, LANES), F32),
                   jax.ShapeDtypeStruct((n_out, LANES), F32)],
        compiler_params=pltpu.CompilerParams(dimension_semantics=("arbitrary",), vmem_limit_bytes=VMEM_LIMIT),
        name="out_projection",
    )(merged, x, wout, ln_g, ln_b, w_route)


EXPERT_ROW_TILE = 256
TOP_K_EXPERTS = 2


def _route_plan(dense, tm):
    n_tok = dense.shape[0]
    n_asg = TOP_K_EXPERTS * n_tok
    n_tiles = (n_asg + N_EXPERTS * (tm - 1)) // tm + 1
    w2, e2 = lax.top_k(dense[:, :N_EXPERTS], TOP_K_EXPERTS)
    eflat = e2.reshape(n_asg).astype(I32)
    w_bits = lax.bitcast_convert_type(w2.reshape(n_asg), I32)
    e_sorted, a_sorted, w_sorted = lax.sort((eflat, jnp.arange(n_asg, dtype=I32), w_bits), num_keys=1, is_stable=True)
    starts = jnp.searchsorted(e_sorted, jnp.arange(N_EXPERTS + 1, dtype=I32), side="left").astype(I32)
    counts = starts[1:] - starts[:-1]
    tiles_e = (counts + tm - 1) // tm
    tile_end = jnp.cumsum(tiles_e).astype(I32)
    tile_start = tile_end - tiles_e
    tile = jnp.arange(n_tiles, dtype=I32)
    tile_expert = jnp.minimum(jnp.searchsorted(tile_end, tile, side="right"), N_EXPERTS - 1).astype(I32)
    tile_used = (tile < tile_end[-1]).astype(I32)
    q = ((tile - tile_start[tile_expert]) * tm)[:, None] + jnp.arange(tm, dtype=I32)[None, :]
    valid = ((q < counts[tile_expert][:, None]) & (tile_used[:, None] > 0)).reshape(n_tiles * tm)
    pos = jnp.clip(starts[tile_expert][:, None] + q, 0, n_asg - 1).reshape(n_tiles * tm)
    picked = jnp.stack([a_sorted, w_sorted], axis=1)[pos]
    a = picked[:, 0]
    src = jnp.where(valid, a // TOP_K_EXPERTS, 0)
    slot = a % TOP_K_EXPERTS
    weight = jnp.where(valid, lax.bitcast_convert_type(picked[:, 1], F32), 0.0)
    return src, slot, valid, weight, tile_expert, tile_used


def _expert_body(te_ref, used_ref, src_ref, dst_ref, cw_ref, wg_ref, wu_ref, wd_ref, h_hbm, y_hbm,
                 x_buf, y_buf, sem_in, sem_out, *, tm):
    j = pl.program_id(0)
    chunks = x_buf.shape[0] // tm

    def token_rows(t):
        return pl.ds(pl.multiple_of(t * chunks, chunks), chunks)

    def gather(r):
        return pltpu.make_async_copy(h_hbm.at[token_rows(src_ref[0, 0, r]), :], x_buf.at[token_rows(r), :],
                                     sem_in.at[0])

    def scatter(r):
        return pltpu.make_async_copy(y_buf.at[token_rows(r), :], y_hbm.at[token_rows(dst_ref[0, 0, r]), :],
                                     sem_out.at[0])

    @pl.when(j == 0)
    def _():
        y_buf[...] = jnp.zeros(y_buf.shape, F32)
        pad_rows = pltpu.make_async_copy(y_buf, y_hbm.at[pl.ds(y_hbm.shape[0] - tm * chunks, tm * chunks), :],
                                         sem_out.at[0])
        pad_rows.start()
        pad_rows.wait()

    @pl.when(used_ref[j] > 0)
    def _():
        for r in range(tm):
            gather(r).start()
        for r in range(tm):
            gather(r).wait()
        xb = _load_token_major(x_buf, tm).astype(BF16)
        gate = jnp.dot(xb, wg_ref[0].astype(BF16), preferred_element_type=F32)
        up = jnp.dot(xb, wu_ref[0].astype(BF16), preferred_element_type=F32)
        hh = (gate * jax.nn.sigmoid(gate)) * up * cw_ref[0]
        _store_token_major(y_buf, jnp.dot(hh.astype(BF16), wd_ref[0].astype(BF16), preferred_element_type=F32))
        for r in range(tm):
            scatter(r).start()
        for r in range(tm):
            scatter(r).wait()


def _experts(h_all, plan, wg, wu, wd, *, tm, slot_stride):
    src, slot, valid, weight, tile_expert, tile_used = plan
    n_experts, d_model, d_expert = wg.shape
    chunks = d_model // LANES
    n_tiles = tile_expert.shape[0]
    pad_row = 2 * slot_stride + (jnp.arange(n_tiles * tm, dtype=I32) % tm)
    dst = jnp.where(valid, slot * slot_stride + src, pad_row)
    per_tile = lambda a: a.reshape(n_tiles, 1, tm)
    smem_rows = pl.BlockSpec((1, 1, tm), lambda j, te, used: (j, 0, 0), memory_space=pltpu.SMEM)
    any_spec = pl.BlockSpec(memory_space=pl.ANY)
    grid_spec = pltpu.PrefetchScalarGridSpec(
        num_scalar_prefetch=2,
        grid=(n_tiles,),
        in_specs=[smem_rows, smem_rows,
                  pl.BlockSpec((1, tm, 1), lambda j, te, used: (j, 0, 0)),
                  pl.BlockSpec((1, d_model, d_expert), lambda j, te, used: (te[j], 0, 0)),
                  pl.BlockSpec((1, d_model, d_expert), lambda j, te, used: (te[j], 0, 0)),
                  pl.BlockSpec((1, d_expert, d_model), lambda j, te, used: (te[j], 0, 0)),
                  any_spec],
        out_specs=any_spec,
        scratch_shapes=[pltpu.VMEM((tm * chunks, LANES), F32), pltpu.VMEM((tm * chunks, LANES), F32),
                        pltpu.SemaphoreType.DMA((1,)), pltpu.SemaphoreType.DMA((1,))],
    )
    return pl.pallas_call(
        functools.partial(_expert_body, tm=tm),
        grid_spec=grid_spec,
        out_shape=jax.ShapeDtypeStruct(((2 * slot_stride + tm) * chunks, LANES), F32),
        compiler_params=pltpu.CompilerParams(dimension_semantics=("arbitrary",), vmem_limit_bytes=VMEM_LIMIT),
        name="expert_mlp",
    )(tile_expert, tile_used, per_tile(src), per_tile(dst), weight.reshape(n_tiles, tm, 1), wg, wu, wd, h_all)


def _combine_body(y0_ref, y1_ref, h_ref, g_ref, b_ref, out_ref):
    tm = out_ref.shape[0]
    f = _load_token_major(y0_ref, tm) + _load_token_major(y1_ref, tm)
    out_ref[...] = _layer_norm(ALPHA * _load_token_major(h_ref, tm) + f, g_ref[...], b_ref[...])


def _combine(y, h_all, ln_g, ln_b, *, tm, tile0, n_tiles, slot_stride):
    d_model = ln_g.shape[1]
    chunks = d_model // LANES
    slot = lambda k: pl.BlockSpec((tm * chunks, LANES), lambda i: (i + tile0 + k * (slot_stride // tm), 0))
    return pl.pallas_call(
        _combine_body,
        grid=(n_tiles,),
        in_specs=[slot(0), slot(1), slot(0), _const_spec(ln_g.shape), _const_spec(ln_b.shape)],
        out_specs=pl.BlockSpec((tm, d_model), lambda i: (i, 0)),
        out_shape=jax.ShapeDtypeStruct((n_tiles * tm, d_model), F32),
        compiler_params=pltpu.CompilerParams(dimension_semantics=("arbitrary",), vmem_limit_bytes=VMEM_LIMIT),
        name="expert_combine",
    )(y, y, h_all, ln_g, ln_b)


def _layer(x_prompt, x_sample, cache_k, cache_v, cache_kidx, state_pool, page_table, w_in, w_pool, pool_scale,
           w_branch_a, w_branch_b, w_out, ln1_g, ln1_b, w_group, w_expert_router, w_gate, w_up, w_down,
           ln2_g, ln2_b):
    batch, seq, d_model = x_prompt.shape
    dec_batch = x_sample.shape[0]
    n_pages = page_table.shape[1]
    past = n_pages * PAGE_SIZE
    n_prompt = batch * seq
    kv_w = N_KV_HEADS * HEAD_DIM
    attn_w = N_HEADS * HEAD_DIM
    pool_width = w_pool.shape[0] * w_pool.shape[1]
    group = N_HEADS // N_KV_HEADS
    assert x_sample.shape[1] == 1 and seq % KEY_CHUNK == 0 and seq % MOE_ROW_TILE == 0
    assert dec_batch % 16 == 0 and n_pages % SAMPLE_PAGES_PER_CHUNK == 0
    assert (SAMPLE_PAGES_PER_CHUNK * PAGE_SIZE) % (past // SUBLANES) == 0

    x_p = x_prompt.reshape(n_prompt, d_model)
    x_s = x_sample.reshape(dec_batch, d_model)

    w_t = jnp.transpose(w_in)
    a_width = w_in.shape[1] - 2 * d_model
    wt_a = jnp.concatenate([w_t[:a_width], jnp.zeros((-a_width % LANES, d_model), F32)], axis=0).astype(BF16)
    wgat, wgbt = w_t[a_width:a_width + d_model], w_t[a_width + d_model:]
    wpool_b, pscale = w_pool.astype(BF16), pool_scale.reshape(1, pool_width)

    utail, a_p, qt, qit, kb, kib, kt, vt, vtb, kit, wt = _in_projection(
        x_p, wt_a, _rope_tables(np.arange(seq)), wpool_b, pscale, batch=batch, seq=seq)
    o_prompt = _prompt_attention(qit, wt, qt, kib, kb, vtb, batch=batch, seq=seq)

    u_s, a_s, q_s, qi_s, k_s, v_s, kiwi_s = _in_projection_sample(
        x_s, wt_a, _rope_tables(np.full((dec_batch,), past)), wpool_b, pscale, jnp.transpose(state_pool, (1, 0, 2)),
        sample_pos=past)
    qi_h = qi_s.reshape(dec_batch, IDX_HEADS, IDX_DIM)
    qi_h = jnp.concatenate([qi_h, jnp.zeros_like(qi_h)], axis=-1)
    in_group = (jnp.arange(N_HEADS)[:, None] // group == jnp.arange(N_KV_HEADS)[None, :])[None, :, :, None]
    qs = jnp.where(in_group, q_s.reshape(dec_batch, N_HEADS, 1, HEAD_DIM), 0).reshape(dec_batch, N_HEADS, kv_w)
    w_s = kiwi_s[:, IDX_DIM:IDX_DIM + IDX_HEADS].reshape(dec_batch, IDX_HEADS, 1)
    o_s = _sample_attention(page_table, qi_h, w_s, qs, kiwi_s.reshape(dec_batch, 1, LANES),
                            k_s.reshape(dec_batch, 1, kv_w), v_s.reshape(dec_batch, 1, kv_w),
                            _keys_minor(cache_kidx), _keys_minor(cache_k), _keys_minor(cache_v))
    o_s = o_s.reshape(dec_batch, N_KV_HEADS, group, N_KV_HEADS, HEAD_DIM)
    o_s = jnp.transpose(jnp.diagonal(o_s, axis1=1, axis2=3), (0, 3, 1, 2)).reshape(dec_batch, attn_w)
    o_sample = o_s.astype(BF16)

    w_route = jnp.concatenate([w_expert_router, w_group,
                               jnp.zeros((d_model, LANES - N_EXPERTS - N_GROUPS), F32)], axis=1)
    g1, b1 = ln1_g.reshape(1, d_model), ln1_b.reshape(1, d_model)
    g2, b2 = ln2_g.reshape(1, d_model), ln2_b.reshape(1, d_model)

    merged_p = _merge(x_p, a_p, o_prompt, wgat.astype(BF16), wgbt.astype(BF16), w_branch_a.astype(BF16),
                      w_branch_b.astype(BF16), tm=ROW_TILE, col_block=d_model, precise=False)
    h_all, dense_all = _out_projection(merged_p, x_p, w_out.astype(BF16), g1, b1, w_route.astype(BF16),
                                       tm=ROW_TILE, precise=False, zero_tiles=1)
    merged_s = _merge(x_s, a_s, o_sample, wgat, wgbt, w_branch_a, w_branch_b, tm=dec_batch,
                      col_block=MERGE_COL_CHUNK, precise=True)
    h_s, dense_s = _out_projection(merged_s, x_s, w_out, g1, b1, w_route, tm=dec_batch, precise=True)

    h_all = lax.dynamic_update_slice(h_all, h_s, (n_prompt * (d_model // LANES), 0))
    dense_all = lax.dynamic_update_slice(dense_all, dense_s, (n_prompt, 0))
    slot_stride = dense_all.shape[0]
    plan = _route_plan(dense_all, EXPERT_ROW_TILE)
    y2 = _experts(h_all, plan, w_gate, w_up, w_down, tm=EXPERT_ROW_TILE, slot_stride=slot_stride)
    y_prompt = _combine(y2, h_all, g2, b2, tm=ROW_TILE, tile0=0, n_tiles=n_prompt // ROW_TILE,
                        slot_stride=slot_stride)
    y_sample = _combine(y2, h_all, g2, b2, tm=dec_batch, tile0=n_prompt // dec_batch, n_tiles=1,
                        slot_stride=slot_stride)

    heads_t = lambda t: jnp.transpose(t.reshape(batch, N_KV_HEADS, HEAD_DIM, seq), (0, 3, 1, 2))
    pool_sample = jnp.concatenate([state_pool[:, 1:], u_s[:, None, :]], axis=1)
    return (y_prompt.reshape(batch, seq, d_model), y_sample.reshape(dec_batch, 1, d_model),
            heads_t(kt), heads_t(vt), jnp.transpose(kit, (0, 2, 1)), utail[:, 16 - POOL_STATE:],
            k_s.reshape(dec_batch, 1, N_KV_HEADS, HEAD_DIM), v_s.reshape(dec_batch, 1, N_KV_HEADS, HEAD_DIM),
            kiwi_s[:, :IDX_DIM].reshape(dec_batch, 1, IDX_DIM), pool_sample)


def kernel(x_prompt, x_sample, cache_k, cache_v, cache_kidx, state_pool, page_table, w_in, w_pool, pool_scale,
           w_branch_a, w_branch_b, w_out, ln1_g, ln1_b, w_group, w_expert_router, w_gate, w_up, w_down,
           ln2_g, ln2_b):
    assert w_in.shape[0] == DEPTH
    outs = _layer(x_prompt, x_sample, cache_k[0], cache_v[0], cache_kidx[0], state_pool[0], page_table, w_in[0],
                  w_pool[0], pool_scale[0], w_branch_a[0], w_branch_b[0], w_out[0], ln1_g[0], ln1_b[0],
                  w_group[0], w_expert_router[0], w_gate[0], w_up[0], w_down[0], ln2_g[0], ln2_b[0])
    y_p, y_s = outs[0], outs[1]
    return (y_p, y_s) + tuple(o[None] for o in outs[2:])
```
